```python
import math
import jax, jax.numpy as jnp
from jax import lax
import numpy as np

D_MODEL = 4096
BATCH = 1
SEQ = 8192
DEPTH = 4

HEAD_DIM = 128
DIFF_HEADS = 4
DIFF_QK_DIM = HEAD_DIM // 2
MOBA_HEADS = 4
MOBA_BLOCK = 256
MOBA_TOP_K = 3
DIL_PATTERNS = ((128, 1), (512, 4), (2048, 16))
DIL_HEADS = 4
N_BRANCH = 3
BRANCH_WIDTH = 512
Q_BLOCK = 128
ROPE_THETA = 10000.0
FFN_HIDDEN = -(-8 * D_MODEL // (3 * 256)) * 256
NORM_EPS = 1e-6
SUBLN_EPS = 1e-5
NEG_INF = -1e30

A_QK_COLS = DIFF_HEADS * 2 * DIFF_QK_DIM
A_V_COLS = DIFF_HEADS * HEAD_DIM
B_COLS = MOBA_HEADS * HEAD_DIM
C_COLS = len(DIL_PATTERNS) * DIL_HEADS * HEAD_DIM
SPLIT_SIZES = (A_QK_COLS, A_QK_COLS, A_V_COLS, B_COLS, B_COLS, B_COLS, C_COLS, C_COLS, C_COLS)
IN_WIDTH = sum(SPLIT_SIZES)

kernel_name = "hybrid_diff_moba_dilated_gated_block"


def rms_norm(x, g, eps):
    xf = x.astype(jnp.float32)
    y = xf * lax.rsqrt(jnp.mean(xf * xf, axis=-1, keepdims=True) + eps)
    return (y * g.astype(jnp.float32)).astype(x.dtype)


def rope_tables(seq, dim):
    inv = ROPE_THETA ** (-jnp.arange(0, dim, 2, dtype=jnp.float32) / dim)
    ang = jnp.arange(seq, dtype=jnp.float32)[:, None] * inv[None, :]
    return jnp.cos(ang), jnp.sin(ang)


def apply_rope(x, cos, sin):
    shape = (1, x.shape[1]) + (1,) * (x.ndim - 3) + (cos.shape[-1],)
    c = cos.reshape(shape)
    s = sin.reshape(shape)
    xf = x.astype(jnp.float32)
    half = x.shape[-1] // 2
    x1, x2 = xf[..., :half], xf[..., half:]
    return jnp.concatenate([x1 * c - x2 * s, x2 * c + x1 * s], axis=-1).astype(x.dtype)


def diff_attention(q, k, v, lam, g_sub, lam_init):
    b, s, h, _, dqk = q.shape
    nq = s // Q_BLOCK
    scale = dqk ** -0.5
    qb = q.reshape(b, nq, Q_BLOCK, h, 2, dqk).transpose(1, 0, 3, 4, 2, 5)
    kt = k.transpose(0, 2, 3, 1, 4)
    vt = v.transpose(0, 2, 1, 3)
    kpos = jnp.arange(s)

    def block(args):
        qc, c = args
        sc = jnp.einsum('bhiqd,bhikd->bhiqk', qc, kt).astype(jnp.float32) * scale
        qpos = c * Q_BLOCK + jnp.arange(Q_BLOCK)
        sc = jnp.where(kpos[None, :] <= qpos[:, None], sc, NEG_INF)
        p = jax.nn.softmax(sc, axis=-1)
        a = p[:, :, 0] - lam * p[:, :, 1]
        return jnp.einsum('bhqk,bhkd->bhqd', a.astype(vt.dtype), vt)

    o = lax.map(block, (qb, jnp.arange(nq)))
    o = o.transpose(1, 0, 3, 2, 4).reshape(b, s, h, -1)
    return rms_norm(o, g_sub, SUBLN_EPS) * (1.0 - lam_init)


def moba_attention(q, k, v):
    b, s, h, dh = q.shape
    sp = -(-s // MOBA_BLOCK) * MOBA_BLOCK
    nb = sp // MOBA_BLOCK
    nq = s // Q_BLOCK
    n_sel = min(MOBA_TOP_K, nb)
    n_s = n_sel * MOBA_BLOCK
    scale = dh ** -0.5
    pad = ((0, 0), (0, 0), (0, sp - s), (0, 0))
    qt = q.transpose(0, 2, 1, 3)
    kt = jnp.pad(k.transpose(0, 2, 1, 3), pad)
    vt = jnp.pad(v.transpose(0, 2, 1, 3), pad)
    kb = kt.reshape(b, h, nb, MOBA_BLOCK, dh)
    vb = vt.reshape(b, h, nb, MOBA_BLOCK, dh)
    k_mean = jnp.mean(kb.astype(jnp.float32), axis=3)
    qb = qt.reshape(b, h, nq, Q_BLOCK, dh).transpose(2, 0, 1, 3, 4)
    bi = jnp.arange(b)[:, None, None, None]
    hi = jnp.arange(h)[None, :, None, None]
    blk_ids = jnp.arange(nb)
    offs = jnp.arange(MOBA_BLOCK)

    def block(args):
        qc, c = args
        own = (c * Q_BLOCK) // MOBA_BLOCK
        qpos = c * Q_BLOCK + jnp.arange(Q_BLOCK)
        gate = jnp.einsum('bhqd,bhnd->bhqn', qc.astype(jnp.float32), k_mean)
        gate = jnp.where(blk_ids < own, gate, NEG_INF)
        _, idx = lax.top_k(gate, n_sel)
        valid = idx < own
        ks = kb[bi, hi, idx]
        vs = vb[bi, hi, idx]
        s_sel = jnp.einsum('bhqd,bhqnkd->bhqnk', qc, ks).astype(jnp.float32) * scale
        s_sel = jnp.where(valid[..., None], s_sel, NEG_INF).reshape(b, h, Q_BLOCK, n_s)
        start = own * MOBA_BLOCK
        k_own = lax.dynamic_slice_in_dim(kt, start, MOBA_BLOCK, axis=2)
        v_own = lax.dynamic_slice_in_dim(vt, start, MOBA_BLOCK, axis=2)
        s_own = jnp.einsum('bhqd,bhkd->bhqk', qc, k_own).astype(jnp.float32) * scale
        s_own = jnp.where(start + offs[None, :] <= qpos[:, None], s_own, NEG_INF)
        p = jax.nn.softmax(jnp.concatenate([s_sel, s_own], axis=-1), axis=-1).astype(v.dtype)
        p_sel = p[..., :n_s].reshape(b, h, Q_BLOCK, n_sel, MOBA_BLOCK)
        return (jnp.einsum('bhqnk,bhqnkd->bhqd', p_sel, vs)
                + jnp.einsum('bhqk,bhkd->bhqd', p[..., n_s:], v_own))

    o = lax.map(block, (qb, jnp.arange(nq)))
    return o.transpose(1, 0, 3, 2, 4).reshape(b, s, h, dh)


def dilated_group(q, k, v, window, dilation):
    b, s, h, dh = q.shape
    band = window // dilation
    unit = band * dilation
    sp = -(-s // unit) * unit
    length = sp // dilation
    nblk = length // band

    def to_sub(t):
        t = jnp.pad(t, ((0, 0), (0, sp - s), (0, 0), (0, 0)))
        return t.reshape(b, length, dilation, h, dh).transpose(0, 3, 2, 1, 4).reshape(b, h, dilation, nblk, band, dh)

    def with_prev(t):
        prev = jnp.concatenate([jnp.zeros_like(t[:, :, :, :1]), t[:, :, :, :-1]], axis=3)
        return jnp.concatenate([prev, t], axis=4)

    qs = to_sub(q)
    kk = with_prev(to_sub(k))
    vv = with_prev(to_sub(v))
    sc = jnp.einsum('bhrnqd,bhrnkd->bhrnqk', qs, kk).astype(jnp.float32) * (dh ** -0.5)
    qi = jnp.arange(band)
    kj = jnp.arange(2 * band) - band
    rel = qi[:, None] - kj[None, :]
    kabs = jnp.arange(nblk)[:, None, None] * band + kj[None, None, :]
    valid = (rel >= 0) & (rel <= band) & (kabs >= 0)
    sc = jnp.where(valid, sc, NEG_INF)
    m = jnp.max(sc, axis=-1, keepdims=True)
    e = jnp.exp(sc - m)
    den = jnp.sum(e, axis=-1, keepdims=True)
    o = jnp.einsum('bhrnqk,bhrnkd->bhrnqd', (e / den).astype(v.dtype), vv)
    lse = (m + jnp.log(den))[..., 0]
    o = o.reshape(b, h, dilation, length, dh).transpose(0, 3, 2, 1, 4).reshape(b, sp, h, dh)[:, :s]
    lse = lse.reshape(b, h, dilation, length).transpose(0, 3, 2, 1).reshape(b, sp, h)[:, :s]
    return o, lse


def dilated_mixture(q, k, v):
    outs, lses = [], []
    for g, (window, dilation) in enumerate(DIL_PATTERNS):
        o, lse = dilated_group(q[:, :, g], k[:, :, g], v[:, :, g], window, dilation)
        outs.append(o)
        lses.append(lse)
    wts = jax.nn.softmax(jnp.stack(lses, axis=0), axis=0)
    return jnp.einsum('gbsh,gbshd->bshd', wts.astype(q.dtype), jnp.stack(outs, axis=0))


def setup_inputs(seed: int = 0) -> dict:
    key = jax.random.key(seed)
    ks = jax.random.split(key, 18)
    f32 = jnp.float32
    d = D_MODEL

    def nrm(k, shape, scale):
        return jax.random.normal(k, shape, f32) * scale

    return {
        "x": nrm(ks[0], (BATCH, SEQ, d), 1.0),
        "g_mix": 1.0 + nrm(ks[1], (DEPTH, d), 0.01),
        "w_in": nrm(ks[2], (DEPTH, d, IN_WIDTH), d ** -0.5),
        "lam_q1": nrm(ks[3], (DEPTH, DIFF_QK_DIM), 0.1),
        "lam_k1": nrm(ks[4], (DEPTH, DIFF_QK_DIM), 0.1),
        "lam_q2": nrm(ks[5], (DEPTH, DIFF_QK_DIM), 0.1),
        "lam_k2": nrm(ks[6], (DEPTH, DIFF_QK_DIM), 0.1),
        "g_subln": 1.0 + nrm(ks[7], (DEPTH, HEAD_DIM), 0.01),
        "w_gate": nrm(ks[8], (DEPTH, d, N_BRANCH * d), d ** -0.5),
        "b_gate": nrm(ks[9], (DEPTH, N_BRANCH * d), 0.01),
        "w_branch": nrm(ks[10], (DEPTH, N_BRANCH, BRANCH_WIDTH, d), BRANCH_WIDTH ** -0.5),
        "w_out": nrm(ks[11], (DEPTH, d, d), d ** -0.5),
        "g_ffn": 1.0 + nrm(ks[12], (DEPTH, d), 0.01),
        "w_ffn_gate": nrm(ks[13], (DEPTH, d, FFN_HIDDEN), d ** -0.5),
        "w_ffn_up": nrm(ks[14], (DEPTH, d, FFN_HIDDEN), d ** -0.5),
        "w_ffn_down": nrm(ks[15], (DEPTH, FFN_HIDDEN, d), FFN_HIDDEN ** -0.5),
        "g_final": 1.0 + nrm(ks[16], (d,), 0.01),
    }


def reference(x, g_mix, w_in, lam_q1, lam_k1, lam_q2, lam_k2, g_subln, w_gate, b_gate,
              w_branch, w_out, g_ffn, w_ffn_gate, w_ffn_up, w_ffn_down, g_final):
    b, s, d = x.shape
    f32 = jnp.float32
    cos_a, sin_a = rope_tables(s, DIFF_QK_DIM)
    cos_h, sin_h = rope_tables(s, HEAD_DIM)
    split_points = np.cumsum(SPLIT_SIZES)[:-1].tolist()
    n_dil = len(DIL_PATTERNS)
    for l in range(DEPTH):
        h = rms_norm(x, g_mix[l], NORM_EPS)
        proj = h @ w_in[l]
        aq, ak, av, bq, bk, bv, cq, ck, cv = jnp.split(proj, split_points, axis=-1)
        aq = apply_rope(aq.reshape(b, s, DIFF_HEADS, 2, DIFF_QK_DIM), cos_a, sin_a)
        ak = apply_rope(ak.reshape(b, s, DIFF_HEADS, 2, DIFF_QK_DIM), cos_a, sin_a)
        av = av.reshape(b, s, DIFF_HEADS, HEAD_DIM)
        lam_init = 0.8 - 0.6 * math.exp(-0.3 * l)
        lam = (jnp.exp(jnp.sum(lam_q1[l].astype(f32) * lam_k1[l].astype(f32)))
               - jnp.exp(jnp.sum(lam_q2[l].astype(f32) * lam_k2[l].astype(f32))) + lam_init)
        o_a = diff_attention(aq, ak, av, lam, g_subln[l], lam_init)
        bq = apply_rope(bq.reshape(b, s, MOBA_HEADS, HEAD_DIM), cos_h, sin_h)
        bk = apply_rope(bk.reshape(b, s, MOBA_HEADS, HEAD_DIM), cos_h, sin_h)
        o_b = moba_attention(bq, bk, bv.reshape(b, s, MOBA_HEADS, HEAD_DIM))
        cq = apply_rope(cq.reshape(b, s, n_dil, DIL_HEADS, HEAD_DIM), cos_h, sin_h)
        ck = apply_rope(ck.reshape(b, s, n_dil, DIL_HEADS, HEAD_DIM), cos_h, sin_h)
        o_c = dilated_mixture(cq, ck, cv.reshape(b, s, n_dil, DIL_HEADS, HEAD_DIM))
        o_all = jnp.stack([o_a.reshape(b, s, BRANCH_WIDTH), o_b.reshape(b, s, BRANCH_WIDTH),
                           o_c.reshape(b, s, BRANCH_WIDTH)], axis=2)
        y = jnp.einsum('bsnc,ncd->bsnd', o_all, w_branch[l])
        gates = jax.nn.sigmoid((h @ w_gate[l] + b_gate[l]).astype(f32)).astype(x.dtype).reshape(b, s, N_BRANCH, d)
        x = x + jnp.sum(gates * y, axis=2) @ w_out[l]
        h2 = rms_norm(x, g_ffn[l], NORM_EPS)
        x = x + (jax.nn.silu(h2 @ w_ffn_gate[l]) * (h2 @ w_ffn_up[l])) @ w_ffn_down[l]
    return rms_norm(x, g_final, NORM_EPS)
```

```python
import functools
import math

import jax
import jax.numpy as jnp
from jax import lax
from jax.experimental import pallas as pl
from jax.experimental.pallas import tpu as pltpu

HEAD_DIM = 128
DIFF_HEADS = 4
DIFF_QK_DIM = HEAD_DIM // 2
MOBA_HEADS = 4
MOBA_BLOCK = 256
MOBA_TOP_K = 3
DIL_PATTERNS = ((128, 1), (512, 4), (2048, 16))
DIL_HEADS = 4
N_BRANCH = 3
BRANCH_WIDTH = 512
ROPE_THETA = 10000.0
NORM_EPS = 1e-6
SUBLN_EPS = 1e-5
NEG_INF = -1e30

N_DIL = len(DIL_PATTERNS)
HEADS = 4
SLAB_AQ, SLAB_AK, SLAB_AV = 0, 4, 8
SLAB_BQ, SLAB_BK, SLAB_BV = 12, 16, 20
SLAB_CQ, SLAB_CK, SLAB_CV = 24, 36, 48
N_SLABS = 60
PROJ_TILE = 512

LANES = 128
VMEM_LIMIT_BYTES = 56 * 1024 * 1024

F32 = jnp.float32
BF16 = jnp.bfloat16


def _params(sem, vmem=VMEM_LIMIT_BYTES):
    return pltpu.CompilerParams(dimension_semantics=sem, vmem_limit_bytes=vmem)


def _tile(n, prefs):
    for p in prefs:
        if n % p == 0:
            return p
    return n


def _rmsnorm_kernel(x_ref, g_ref, o_ref, *, eps):
    x = x_ref[...]
    ms = jnp.mean(x * x, axis=-1, keepdims=True)
    o_ref[...] = (x * lax.rsqrt(ms + eps) * g_ref[...]).astype(o_ref.dtype)


def rmsnorm(x, g, eps, out_dtype):
    s, d = x.shape
    tm = _tile(s, (256, 128, 8))
    return pl.pallas_call(
        functools.partial(_rmsnorm_kernel, eps=eps),
        out_shape=jax.ShapeDtypeStruct((s, d), out_dtype),
        grid=(s // tm,),
        in_specs=[pl.BlockSpec((tm, d), lambda i: (i, 0)),
                  pl.BlockSpec((1, d), lambda i: (0, 0))],
        out_specs=pl.BlockSpec((tm, d), lambda i: (i, 0)),
        compiler_params=_params(("parallel",)),
        name="rmsnorm",
    )(x, g.reshape(1, d))


_ROPE64_Q, _ROPE64_K = (0,), (1,)
_ROPE128_Q, _ROPE128_K = (3, 6, 7, 8), (4, 9, 10, 11)


def _rot_half_128(x):
    return pltpu.roll(x, 64, 1)


def _rot_half_64(x):
    lane = lax.broadcasted_iota(jnp.int32, x.shape, 1)
    return jnp.where((lane & 32) == 0, pltpu.roll(x, 96, 1), pltpu.roll(x, 32, 1))


def _in_proj_kernel(h_ref, w_ref, c128_ref, s128_ref, c64_ref, s64_ref, o_ref):
    j = pl.program_id(1)
    acc = jnp.dot(h_ref[...], w_ref[...], preferred_element_type=F32)

    def emit(fn):
        for c in range(PROJ_TILE // LANES):
            o_ref[c] = fn(acc[:, c * LANES:(c + 1) * LANES]).astype(o_ref.dtype)

    def is_in(tiles):
        cond = j == tiles[0]
        for t in tiles[1:]:
            cond = jnp.logical_or(cond, j == t)
        return cond

    def rope128(scale):
        def fn(xs):
            return (xs * c128_ref[...] + _rot_half_128(xs) * s128_ref[...]) * scale
        return fn

    def rope64(scale):
        def fn(xs):
            return (xs * c64_ref[...] + _rot_half_64(xs) * s64_ref[...]) * scale
        return fn

    roped = _ROPE64_Q + _ROPE64_K + _ROPE128_Q + _ROPE128_K
    pl.when(is_in(_ROPE64_Q))(lambda: emit(rope64(DIFF_QK_DIM ** -0.5)))
    pl.when(is_in(_ROPE64_K))(lambda: emit(rope64(1.0)))
    pl.when(is_in(_ROPE128_Q))(lambda: emit(rope128(HEAD_DIM ** -0.5)))
    pl.when(is_in(_ROPE128_K))(lambda: emit(rope128(1.0)))
    pl.when(jnp.logical_not(is_in(roped)))(lambda: emit(lambda xs: xs))


def in_proj(h, w, tabs):
    s, d = h.shape
    n = w.shape[1]
    tm = _tile(s, (1024, 512, 256))
    tab_spec = pl.BlockSpec((tm, LANES), lambda i, j: (i, 0))
    return pl.pallas_call(
        _in_proj_kernel,
        out_shape=jax.ShapeDtypeStruct((n // LANES, s, LANES), BF16),
        grid=(s // tm, n // PROJ_TILE),
        in_specs=[pl.BlockSpec((tm, d), lambda i, j: (i, 0)),
                  pl.BlockSpec((d, PROJ_TILE), lambda i, j: (0, j)),
                  tab_spec, tab_spec, tab_spec, tab_spec],
        out_specs=pl.BlockSpec((PROJ_TILE // LANES, tm, LANES), lambda i, j: (j, i, 0)),
        compiler_params=_params(("parallel", "arbitrary")),
        name="in_proj",
    )(h, w, *tabs)


def rope_tables(seq):
    def table(dim):
        inv = ROPE_THETA ** (-jnp.arange(0, dim, 2, dtype=F32) / dim)
        ang = jnp.arange(seq, dtype=F32)[:, None] * inv[None, :]
        cos, sin = jnp.cos(ang), jnp.sin(ang)
        reps = LANES // dim
        return (jnp.tile(jnp.concatenate([cos, cos], axis=1), (1, reps)),
                jnp.tile(jnp.concatenate([-sin, sin], axis=1), (1, reps)))
    c128, s128 = table(HEAD_DIM)
    c64, s64 = table(DIFF_QK_DIM)
    return c128, s128, c64, s64


def _transpose_bf16(x):
    return x.astype(F32).T.astype(BF16)


def _dot_nt(a, b):
    return lax.dot_general(a, b, (((1,), (1,)), ((), ())), preferred_element_type=F32)


def _fill_v_transposed(v_ref, vt_ref, chunk):
    seq = v_ref.shape[1]
    for c in range(seq // chunk):
        vt_ref[:, c * chunk:(c + 1) * chunk] = _transpose_bf16(v_ref[0, c * chunk:(c + 1) * chunk, :])


def _online_softmax_step(st, vt, m_ref, l_ref, acc_ref):
    m_old = m_ref[...]
    m_new = jnp.maximum(m_old, jnp.max(st, axis=0, keepdims=True))
    alpha = jnp.exp(m_old - m_new)
    p = jnp.exp(st - m_new)
    l_ref[...] = alpha * l_ref[...] + jnp.sum(p, axis=0, keepdims=True)
    acc_ref[...] = alpha * acc_ref[...] + jnp.dot(vt, p.astype(BF16), preferred_element_type=F32)
    m_ref[...] = m_new


def _init_stats(m_ref, l_ref, acc_ref):
    m_ref[...] = jnp.full(m_ref.shape, NEG_INF, F32)
    l_ref[...] = jnp.zeros(l_ref.shape, F32)
    acc_ref[...] = jnp.zeros(acc_ref.shape, F32)


def _diff_attn_kernel(q_ref, k_ref, v_ref, lam_ref, gsub_ref, o_ref,
                      vt_ref, qs_ref, m_ref, l_ref, acc_ref, *, tq):
    i = pl.program_id(1)

    @pl.when(i == 0)
    def _():
        _fill_v_transposed(v_ref, vt_ref, tq)

    q = q_ref[0]
    lane = lax.broadcasted_iota(jnp.int32, q.shape, 1)
    zero = jnp.zeros_like(q)
    qs_ref[0:tq, :] = jnp.where(lane < DIFF_QK_DIM, q, zero)
    qs_ref[tq:2 * tq, :] = jnp.where(lane >= DIFF_QK_DIM, q, zero)
    _init_stats(m_ref, l_ref, acc_ref)

    def past(j, carry):
        start = pl.multiple_of(j * tq, tq)
        st = _dot_nt(k_ref[0, pl.ds(start, tq), :], qs_ref[...])
        _online_softmax_step(st, vt_ref[:, pl.ds(start, tq)], m_ref, l_ref, acc_ref)
        return carry

    lax.fori_loop(0, i, past, 0)

    start = pl.multiple_of(i * tq, tq)
    st = _dot_nt(k_ref[0, pl.ds(start, tq), :], qs_ref[...])
    krow = lax.broadcasted_iota(jnp.int32, st.shape, 0)
    qcol = lax.broadcasted_iota(jnp.int32, st.shape, 1)
    qcol = jnp.where(qcol >= tq, qcol - tq, qcol)
    st = jnp.where(krow <= qcol, st, NEG_INF)
    _online_softmax_step(st, vt_ref[:, pl.ds(start, tq)], m_ref, l_ref, acc_ref)

    lp = lam_ref[...]
    lam_init = lp[4:5, 0:1]
    lam = (jnp.exp(jnp.sum(lp[0:1] * lp[1:2], axis=1, keepdims=True))
           - jnp.exp(jnp.sum(lp[2:3] * lp[3:4], axis=1, keepdims=True)) + lam_init)
    o = acc_ref[...] / l_ref[...]
    od = o[:, 0:tq] - lam * o[:, tq:2 * tq]
    ms = jnp.mean(od * od, axis=0, keepdims=True)
    y = od * lax.rsqrt(ms + SUBLN_EPS) * gsub_ref[...] * (1.0 - lam_init)
    o_ref[...] = y.T.astype(o_ref.dtype)


def diff_attention(proj, lam_params, g_sub):
    _, s, _ = proj.shape
    tq = _tile(s, (512, 256, 128))
    return pl.pallas_call(
        functools.partial(_diff_attn_kernel, tq=tq),
        out_shape=jax.ShapeDtypeStruct((s, BRANCH_WIDTH), BF16),
        grid=(HEADS, s // tq),
        in_specs=[pl.BlockSpec((1, tq, LANES), lambda h, i: (SLAB_AQ + h, i, 0)),
                  pl.BlockSpec((1, s, LANES), lambda h, i: (SLAB_AK + h, 0, 0)),
                  pl.BlockSpec((1, s, LANES), lambda h, i: (SLAB_AV + h, 0, 0)),
                  pl.BlockSpec((8, LANES), lambda h, i: (0, 0)),
                  pl.BlockSpec((HEAD_DIM, 1), lambda h, i: (0, 0))],
        out_specs=pl.BlockSpec((tq, LANES), lambda h, i: (i, h)),
        scratch_shapes=[pltpu.VMEM((HEAD_DIM, s), BF16),
                        pltpu.VMEM((2 * tq, LANES), BF16),
                        pltpu.VMEM((1, 2 * tq), F32),
                        pltpu.VMEM((1, 2 * tq), F32),
                        pltpu.VMEM((HEAD_DIM, 2 * tq), F32)],
        compiler_params=_params(("arbitrary", "arbitrary")),
        name="diff_attn",
    )(proj, proj, proj, lam_params, g_sub.reshape(HEAD_DIM, 1))


def _moba_attn_kernel(q_ref, k_ref, v_ref, o_ref,
                      vt_ref, kmean_ref, bias_ref, m_ref, l_ref, acc_ref, *, tq, nb):
    i = pl.program_id(1)
    blk = MOBA_BLOCK
    blk_shift = blk.bit_length() - 1
    own_per_tile = tq // blk

    @pl.when(i == 0)
    def _():
        _fill_v_transposed(v_ref, vt_ref, tq)
        for n in range(nb):
            kmean_ref[n:n + 1, :] = jnp.mean(
                k_ref[0, n * blk:(n + 1) * blk, :].astype(F32), axis=0, keepdims=True)

    q = q_ref[0]
    km = kmean_ref[...]
    km_hi = km.astype(BF16)
    km_lo = (km - km_hi.astype(F32)).astype(BF16)
    gate = _dot_nt(km_hi, q) + _dot_nt(km_lo, q)
    bidx = lax.broadcasted_iota(jnp.int32, gate.shape, 0)
    qpos = i * tq + lax.broadcasted_iota(jnp.int32, gate.shape, 1)
    own = jnp.right_shift(qpos, blk_shift)
    past_blk = bidx < own
    gate = jnp.where(past_blk, gate, NEG_INF)
    rank = jnp.zeros(gate.shape, jnp.int32)
    for mth in range(nb):
        gm = gate[mth:mth + 1, :]
        beats = jnp.logical_or(gm > gate, jnp.logical_and(gm == gate, bidx > mth))
        rank = rank + beats.astype(jnp.int32)
    selected = jnp.logical_and(rank < MOBA_TOP_K, past_blk)
    bias_ref[...] = jnp.where(selected, 0.0, NEG_INF).astype(F32)
    _init_stats(m_ref, l_ref, acc_ref)

    def past(n, carry):
        start = pl.multiple_of(n * blk, blk)
        st = _dot_nt(k_ref[0, pl.ds(start, blk), :], q) + bias_ref[pl.ds(n, 1), :]
        _online_softmax_step(st, vt_ref[:, pl.ds(start, blk)], m_ref, l_ref, acc_ref)
        return carry

    first_own = i * own_per_tile
    lax.fori_loop(0, first_own, past, 0)

    own_row = jnp.right_shift(i * tq + lax.broadcasted_iota(jnp.int32, (1, tq), 1), blk_shift)
    for t in range(own_per_tile):
        n = first_own + t
        start = pl.multiple_of(n * blk, blk)
        st = _dot_nt(k_ref[0, pl.ds(start, blk), :], q)
        kpos = n * blk + lax.broadcasted_iota(jnp.int32, st.shape, 0)
        qp = i * tq + lax.broadcasted_iota(jnp.int32, st.shape, 1)
        row_bias = jnp.where(own_row > n, bias_ref[pl.ds(n, 1), :],
                             jnp.where(own_row == n, 0.0, NEG_INF))
        st = jnp.where(kpos <= qp, st, NEG_INF) + row_bias
        _online_softmax_step(st, vt_ref[:, pl.ds(start, blk)], m_ref, l_ref, acc_ref)

    o = acc_ref[...] / l_ref[...]
    o_ref[...] = o.T.astype(o_ref.dtype)


def moba_attention(proj):
    _, s, _ = proj.shape
    assert s % MOBA_BLOCK == 0 and MOBA_BLOCK & (MOBA_BLOCK - 1) == 0
    nb = s // MOBA_BLOCK
    tq = _tile(s, (512, 256))
    return pl.pallas_call(
        functools.partial(_moba_attn_kernel, tq=tq, nb=nb),
        out_shape=jax.ShapeDtypeStruct((s, BRANCH_WIDTH), BF16),
        grid=(HEADS, s // tq),
        in_specs=[pl.BlockSpec((1, tq, LANES), lambda h, i: (SLAB_BQ + h, i, 0)),
                  pl.BlockSpec((1, s, LANES), lambda h, i: (SLAB_BK + h, 0, 0)),
                  pl.BlockSpec((1, s, LANES), lambda h, i: (SLAB_BV + h, 0, 0))],
        out_specs=pl.BlockSpec((tq, LANES), lambda h, i: (i, h)),
        scratch_shapes=[pltpu.VMEM((HEAD_DIM, s), BF16),
                        pltpu.VMEM((nb, LANES), F32),
                        pltpu.VMEM((nb, tq), F32),
                        pltpu.VMEM((1, tq), F32),
                        pltpu.VMEM((1, tq), F32),
                        pltpu.VMEM((HEAD_DIM, tq), F32)],
        compiler_params=_params(("arbitrary", "arbitrary")),
        name="moba_attn",
    )(proj, proj, proj)


def _dilated_kernel(q_ref, kc_ref, vc_ref, kp_ref, vp_ref, o_ref, lse_ref, *, tq, band):
    i = pl.program_id(2)
    q = q_ref[0]
    st_c = _dot_nt(kc_ref[0], q)
    st_p = _dot_nt(kp_ref[0], q)
    qi = lax.broadcasted_iota(jnp.int32, st_c.shape, 1)
    kj = lax.broadcasted_iota(jnp.int32, st_c.shape, 0)
    rel = qi - kj
    st_c = jnp.where(jnp.logical_and(rel >= 0, rel <= band), st_c, NEG_INF)
    qi_p = lax.broadcasted_iota(jnp.int32, st_p.shape, 1)
    kj_p = lax.broadcasted_iota(jnp.int32, st_p.shape, 0) - band
    rel_p = qi_p - kj_p
    ok_p = jnp.logical_and(rel_p <= band, i > 0)
    st_p = jnp.where(ok_p, st_p, NEG_INF)
    m = jnp.maximum(jnp.max(st_c, axis=0, keepdims=True), jnp.max(st_p, axis=0, keepdims=True))
    e_c = jnp.exp(st_c - m)
    e_p = jnp.exp(st_p - m)
    den = jnp.sum(e_c, axis=0, keepdims=True) + jnp.sum(e_p, axis=0, keepdims=True)
    inv = 1.0 / den
    p_c = (e_c * inv).astype(BF16)
    p_p = (e_p * inv).astype(BF16)
    o = (jnp.dot(_transpose_bf16(vc_ref[0]), p_c, preferred_element_type=F32)
         + jnp.dot(_transpose_bf16(vp_ref[0]), p_p, preferred_element_type=F32))
    o_ref[0] = o.T.astype(o_ref.dtype)
    lse = m + jnp.log(den)
    lse_ref[0] = jnp.broadcast_to(lse, (LANES, tq)).T


def dilated_group(proj, g, window, dilation):
    _, s, _ = proj.shape
    band = window // dilation
    assert band == LANES and s % (band * dilation) == 0
    length = s // dilation
    view = proj.reshape(N_SLABS, length, dilation * LANES)
    tq = _tile(length, (512, 256, 128))
    per = tq // band
    sq, sk, sv = SLAB_CQ + g * HEADS, SLAB_CK + g * HEADS, SLAB_CV + g * HEADS
    cur = lambda base: pl.BlockSpec((1, tq, LANES), lambda h, r, i: (base + h, i, r))
    prev = lambda base: pl.BlockSpec(
        (1, band, LANES), lambda h, r, i: (base + h, jnp.maximum(i * per - 1, 0), r))
    out_spec = pl.BlockSpec((1, tq, LANES), lambda h, r, i: (h, i, r))
    o, lse = pl.pallas_call(
        functools.partial(_dilated_kernel, tq=tq, band=band),
        out_shape=(jax.ShapeDtypeStruct((HEADS, length, dilation * LANES), BF16),
                   jax.ShapeDtypeStruct((HEADS, length, dilation * LANES), F32)),
        grid=(HEADS, dilation, length // tq),
        in_specs=[cur(sq), cur(sk), cur(sv), prev(sk), prev(sv)],
        out_specs=(out_spec, out_spec),
        compiler_params=_params(("parallel", "parallel", "parallel")),
        name=f"dilated_d{dilation}",
    )(view, view, view, view, view)
    return o.reshape(HEADS, s, LANES), lse.reshape(HEADS, s, LANES)


def _dil_merge_kernel(*refs):
    o_refs, lse_refs, out_ref = refs[:N_DIL], refs[N_DIL:2 * N_DIL], refs[2 * N_DIL]
    lses = [r[0] for r in lse_refs]
    m = functools.reduce(jnp.maximum, lses)
    es = [jnp.exp(x - m) for x in lses]
    inv = 1.0 / functools.reduce(lambda a, b: a + b, es)
    acc = None
    for e, o_ref in zip(es, o_refs):
        term = (e * inv) * o_ref[0].astype(F32)
        acc = term if acc is None else acc + term
    out_ref[...] = acc.astype(out_ref.dtype)


def dilated_mixture(proj):
    _, s, _ = proj.shape
    outs, lses = [], []
    for g, (window, dilation) in enumerate(DIL_PATTERNS):
        o, lse = dilated_group(proj, g, window, dilation)
        outs.append(o)
        lses.append(lse)
    tm = _tile(s, (1024, 512, 256))
    spec = pl.BlockSpec((1, tm, LANES), lambda h, i: (h, i, 0))
    return pl.pallas_call(
        _dil_merge_kernel,
        out_shape=jax.ShapeDtypeStruct((s, BRANCH_WIDTH), BF16),
        grid=(HEADS, s // tm),
        in_specs=[spec] * (2 * N_DIL),
        out_specs=pl.BlockSpec((tm, LANES), lambda h, i: (i, h)),
        compiler_params=_params(("parallel", "parallel")),
        name="dilated_merge",
    )(*outs, *lses)


def _gate_mix_kernel(h_ref, oa_ref, ob_ref, oc_ref, wg0_ref, wg1_ref, wg2_ref,
                     b0_ref, b1_ref, b2_ref, wb_ref, z_ref):
    h = h_ref[...]
    z = None
    for n, (o_ref, wg_ref, b_ref) in enumerate(((oa_ref, wg0_ref, b0_ref),
                                                (ob_ref, wg1_ref, b1_ref),
                                                (oc_ref, wg2_ref, b2_ref))):
        gate = jax.nn.sigmoid(jnp.dot(h, wg_ref[...], preferred_element_type=F32) + b_ref[...])
        y = jnp.dot(o_ref[...], wb_ref[n], preferred_element_type=F32)
        z = gate * y if z is None else z + gate * y
    z_ref[...] = z.astype(z_ref.dtype)


def gate_mix(h, o_a, o_b, o_c, w_gate, b_gate, w_branch):
    s, d = h.shape
    tm = _tile(s, (1024, 512, 256))
    tn = _tile(d, (256, 128))
    nj = d // tn
    o_spec = pl.BlockSpec((tm, BRANCH_WIDTH), lambda i, j: (i, 0))
    wg_spec = lambda n: pl.BlockSpec((d, tn), lambda i, j: (0, n * nj + j))
    b_spec = lambda n: pl.BlockSpec((1, tn), lambda i, j: (0, n * nj + j))
    b2d = b_gate.reshape(1, N_BRANCH * d)
    return pl.pallas_call(
        _gate_mix_kernel,
        out_shape=jax.ShapeDtypeStruct((s, d), BF16),
        grid=(s // tm, nj),
        in_specs=[pl.BlockSpec((tm, d), lambda i, j: (i, 0)), o_spec, o_spec, o_spec,
                  wg_spec(0), wg_spec(1), wg_spec(2), b_spec(0), b_spec(1), b_spec(2),
                  pl.BlockSpec((N_BRANCH, BRANCH_WIDTH, tn), lambda i, j: (0, 0, j))],
        out_specs=pl.BlockSpec((tm, tn), lambda i, j: (i, j)),
        compiler_params=_params(("parallel", "arbitrary")),
        name="gate_mix",
    )(h, o_a, o_b, o_c, w_gate, w_gate, w_gate, b2d, b2d, b2d, w_branch)


def _residual_matmul_kernel(a_ref, w_ref, x_ref, o_ref):
    o_ref[...] = x_ref[...] + jnp.dot(a_ref[...], w_ref[...], preferred_element_type=F32)


def residual_matmul(a, w, x, tm_prefs, tn_prefs):
    s, k = a.shape
    d = w.shape[1]
    tm = _tile(s, tm_prefs)
    tn = _tile(d, tn_prefs)
    return pl.pallas_call(
        _residual_matmul_kernel,
        out_shape=jax.ShapeDtypeStruct((s, d), F32),
        grid=(s // tm, d // tn),
        in_specs=[pl.BlockSpec((tm, k), lambda i, j: (i, 0)),
                  pl.BlockSpec((k, tn), lambda i, j: (0, j)),
                  pl.BlockSpec((tm, tn), lambda i, j: (i, j))],
        out_specs=pl.BlockSpec((tm, tn), lambda i, j: (i, j)),
        input_output_aliases={2: 0},
        compiler_params=_params(("parallel", "arbitrary")),
        name="residual_matmul",
    )(a, w, x)


def _swiglu_kernel(h_ref, wg_ref, wu_ref, o_ref):
    h = h_ref[...]
    g = jnp.dot(h, wg_ref[...], preferred_element_type=F32)
    u = jnp.dot(h, wu_ref[...], preferred_element_type=F32)
    o_ref[...] = (g * jax.nn.sigmoid(g) * u).astype(o_ref.dtype)


def swiglu_up(h, w_g, w_u):
    s, d = h.shape
    f = w_g.shape[1]
    tm = _tile(s, (1024, 512, 256))
    tn = _tile(f, (256, 128))
    return pl.pallas_call(
        _swiglu_kernel,
        out_shape=jax.ShapeDtypeStruct((s, f), BF16),
        grid=(s // tm, f // tn),
        in_specs=[pl.BlockSpec((tm, d), lambda i, j: (i, 0)),
                  pl.BlockSpec((d, tn), lambda i, j: (0, j)),
                  pl.BlockSpec((d, tn), lambda i, j: (0, j))],
        out_specs=pl.BlockSpec((tm, tn), lambda i, j: (i, j)),
        compiler_params=_params(("parallel", "arbitrary")),
        name="swiglu_up",
    )(h, w_g, w_u)


def kernel(x, g_mix, w_in, lam_q1, lam_k1, lam_q2, lam_k2, g_subln, w_gate, b_gate, w_branch,
           w_out, g_ffn, w_ffn_gate, w_ffn_up, w_ffn_down, g_final):
    b, s, d = x.shape
    depth = w_in.shape[0]
    tabs = rope_tables(s)
    outs = []
    for bi in range(b):
        xr = x[bi]
        for l in range(depth):
            lam_init = 0.8 - 0.6 * math.exp(-0.3 * l)
            pad = lambda v: jnp.pad(v.astype(F32), (0, LANES - DIFF_QK_DIM))
            lam_params = jnp.stack(
                [pad(lam_q1[l]), pad(lam_k1[l]), pad(lam_q2[l]), pad(lam_k2[l]),
                 jnp.full((LANES,), lam_init, F32)]
                + [jnp.zeros((LANES,), F32)] * 3)
            h = rmsnorm(xr, g_mix[l], NORM_EPS, BF16)
            proj = in_proj(h, w_in[l].astype(BF16), tabs)
            o_a = diff_attention(proj, lam_params, g_subln[l])
            o_b = moba_attention(proj)
            o_c = dilated_mixture(proj)
            z = gate_mix(h, o_a, o_b, o_c, w_gate[l].astype(BF16), b_gate[l],
                         w_branch[l].astype(BF16))
            xr = residual_matmul(z, w_out[l].astype(BF16), xr, (1024, 512, 256), (512, 256, 128))
            h2 = rmsnorm(xr, g_ffn[l], NORM_EPS, BF16)
            u = swiglu_up(h2, w_ffn_gate[l].astype(BF16), w_ffn_up[l].astype(BF16))
            xr = residual_matmul(u, w_ffn_down[l].astype(BF16), xr, (512, 256), (512, 256, 128))
        outs.append(rmsnorm(xr, g_final, NORM_EPS, x.dtype))
    return jnp.stack(outs, axis=0)
```

```python
import functools
import math

import jax
import jax.numpy as jnp
from jax import lax
from jax.experimental import pallas as pl
from jax.experimental.pallas import tpu as pltpu

HEAD_DIM = 128
DIFF_HEADS = 4
DIFF_QK_DIM = HEAD_DIM // 2
MOBA_HEADS = 4
MOBA_BLOCK = 256
MOBA_TOP_K = 3
DIL_PATTERNS = ((128, 1), (512, 4), (2048, 16))
DIL_HEADS = 4
N_BRANCH = 3
BRANCH_WIDTH = 512
ROPE_THETA = 10000.0
NORM_EPS = 1e-6
SUBLN_EPS = 1e-5
NEG_INF = -1e30
LOG2_E = math.log2(math.e)
LN_2 = math.log(2.0)

N_DIL = len(DIL_PATTERNS)
HEADS = 4
SLAB_AQ, SLAB_AK, SLAB_AV = 0, 4, 8
SLAB_BQ, SLAB_BK, SLAB_BV = 12, 16, 20
SLAB_CQ, SLAB_CK, SLAB_CV = 24, 36, 48
N_SLABS = 60
PROJ_TILE = 512

LANES = 128
BF16_SUBLANES = 16
VMEM_LIMIT_BYTES = 58 * 1024 * 1024

F32 = jnp.float32
BF16 = jnp.bfloat16


def _params(sem, vmem=VMEM_LIMIT_BYTES):
    return pltpu.CompilerParams(dimension_semantics=sem, vmem_limit_bytes=vmem)


def _tile(n, prefs):
    for p in prefs:
        if n % p == 0:
            return p
    return n


def _resident(block_shape, index_map):
    return pl.BlockSpec(block_shape, index_map, pipeline_mode=pl.Buffered(1))


def _rmsnorm_kernel(x_ref, g_ref, o_ref, *, eps):
    x = x_ref[...]
    ms = jnp.mean(x * x, axis=-1, keepdims=True)
    o_ref[...] = (x * lax.rsqrt(ms + eps) * g_ref[...]).astype(o_ref.dtype)


def rmsnorm(x, g, eps, out_dtype):
    s, d = x.shape
    tm = _tile(s, (256, 128, 8))
    return pl.pallas_call(
        functools.partial(_rmsnorm_kernel, eps=eps),
        out_shape=jax.ShapeDtypeStruct((s, d), out_dtype),
        grid=(s // tm,),
        in_specs=[pl.BlockSpec((tm, d), lambda i: (i, 0)),
                  pl.BlockSpec((1, d), lambda i: (0, 0))],
        out_specs=pl.BlockSpec((tm, d), lambda i: (i, 0)),
        compiler_params=_params(("parallel",)),
        name="rmsnorm",
    )(x, g.reshape(1, d))


_ROPE64_Q, _ROPE64_K = (0,), (1,)
_ROPE128_Q, _ROPE128_K = (3, 6, 7, 8), (4, 9, 10, 11)


def _rot_half_128(x):
    return pltpu.roll(x, 64, 1)


def _rot_half_64(x):
    lane = lax.broadcasted_iota(jnp.int32, x.shape, 1)
    return jnp.where((lane & 32) == 0, pltpu.roll(x, 96, 1), pltpu.roll(x, 32, 1))


def _in_proj_kernel(h_ref, w_ref, c128_ref, s128_ref, c64_ref, s64_ref, o_ref):
    j = pl.program_id(1)
    acc = jnp.dot(h_ref[...], w_ref[...].astype(BF16), preferred_element_type=F32)

    def emit(fn):
        for c in range(PROJ_TILE // LANES):
            o_ref[c] = fn(acc[:, c * LANES:(c + 1) * LANES]).astype(o_ref.dtype)

    def is_in(tiles):
        cond = j == tiles[0]
        for t in tiles[1:]:
            cond = jnp.logical_or(cond, j == t)
        return cond

    def rope128(scale):
        def fn(xs):
            return (xs * c128_ref[...] + _rot_half_128(xs) * s128_ref[...]) * scale
        return fn

    def rope64(scale):
        def fn(xs):
            return (xs * c64_ref[...] + _rot_half_64(xs) * s64_ref[...]) * scale
        return fn

    roped = _ROPE64_Q + _ROPE64_K + _ROPE128_Q + _ROPE128_K
    pl.when(is_in(_ROPE64_Q))(lambda: emit(rope64(DIFF_QK_DIM ** -0.5 * LOG2_E)))
    pl.when(is_in(_ROPE64_K))(lambda: emit(rope64(1.0)))
    pl.when(is_in(_ROPE128_Q))(lambda: emit(rope128(HEAD_DIM ** -0.5 * LOG2_E)))
    pl.when(is_in(_ROPE128_K))(lambda: emit(rope128(1.0)))
    pl.when(jnp.logical_not(is_in(roped)))(lambda: emit(lambda xs: xs))


def in_proj(h, w_all, layer, tabs):
    s, d = h.shape
    n = w_all.shape[2]
    tm = _tile(s, (2048, 1024, 512, 256))
    tab_spec = _resident((tm, LANES), lambda i, j: (i, 0))
    return pl.pallas_call(
        _in_proj_kernel,
        out_shape=jax.ShapeDtypeStruct((n // LANES, s, LANES), BF16),
        grid=(s // tm, n // PROJ_TILE),
        in_specs=[_resident((tm, d), lambda i, j: (i, 0)),
                  pl.BlockSpec((None, d, PROJ_TILE), lambda i, j: (layer, 0, j)),
                  tab_spec, tab_spec, tab_spec, tab_spec],
        out_specs=pl.BlockSpec((PROJ_TILE // LANES, tm, LANES), lambda i, j: (j, i, 0)),
        compiler_params=_params(("parallel", "arbitrary")),
        name="in_proj",
    )(h, w_all, *tabs)


def rope_tables(seq):
    def table(dim):
        inv = ROPE_THETA ** (-jnp.arange(0, dim, 2, dtype=F32) / dim)
        ang = jnp.arange(seq, dtype=F32)[:, None] * inv[None, :]
        cos, sin = jnp.cos(ang), jnp.sin(ang)
        reps = LANES // dim
        return (jnp.tile(jnp.concatenate([cos, cos], axis=1), (1, reps)),
                jnp.tile(jnp.concatenate([-sin, sin], axis=1), (1, reps)))
    c128, s128 = table(HEAD_DIM)
    c64, s64 = table(DIFF_QK_DIM)
    return c128, s128, c64, s64


VT_ROWS = HEAD_DIM + BF16_SUBLANES


def _transpose_bf16(x):
    return x.astype(F32).T.astype(BF16)


def _dot_nt(a, b):
    return lax.dot_general(a, b, (((1,), (1,)), ((), ())), preferred_element_type=F32)


def _fill_v_transposed(v_ref, vt_ref, chunk):
    seq = v_ref.shape[1]
    for c in range(seq // chunk):
        vt_ref[0:HEAD_DIM, c * chunk:(c + 1) * chunk] = _transpose_bf16(
            v_ref[0, c * chunk:(c + 1) * chunk, :])
    vt_ref[HEAD_DIM:VT_ROWS, :] = jnp.ones((BF16_SUBLANES, seq), BF16)


def _online_softmax_step(st, vt, m_ref, acc_ref):
    m_old = m_ref[...]
    m_new = jnp.maximum(m_old, jnp.max(st, axis=0, keepdims=True))
    alpha = jnp.exp2(m_old - m_new)
    p = jnp.exp2(st - m_new).astype(BF16)
    acc_ref[...] = alpha * acc_ref[...] + jnp.dot(vt, p, preferred_element_type=F32)
    m_ref[...] = m_new


def _init_stats(m_ref, acc_ref):
    m_ref[...] = jnp.full(m_ref.shape, NEG_INF, F32)
    acc_ref[...] = jnp.zeros(acc_ref.shape, F32)


def _normalized(acc_ref):
    acc = acc_ref[...]
    return acc[0:HEAD_DIM, :] / acc[HEAD_DIM:HEAD_DIM + 1, :]


def _paired_loop(n, step):
    def pair(t, carry):
        step(2 * t)
        step(2 * t + 1)
        return carry
    lax.fori_loop(0, n // 2, pair, 0)
    pl.when(n % 2 == 1)(lambda: step(n - 1))


def _diff_attn_kernel(q_ref, k_ref, v_ref, lam_ref, gsub_ref, o_ref,
                      vt_ref, qs_ref, m_ref, acc_ref, *, tq):
    i = pl.program_id(1)

    @pl.when(i == 0)
    def _():
        _fill_v_transposed(v_ref, vt_ref, tq)

    q = q_ref[0]
    lane = lax.broadcasted_iota(jnp.int32, q.shape, 1)
    zero = jnp.zeros_like(q)
    qs_ref[0:tq, :] = jnp.where(lane < DIFF_QK_DIM, q, zero)
    qs_ref[tq:2 * tq, :] = jnp.where(lane >= DIFF_QK_DIM, q, zero)
    _init_stats(m_ref, acc_ref)

    def past(j):
        start = pl.multiple_of(j * tq, tq)
        st = _dot_nt(k_ref[0, pl.ds(start, tq), :], qs_ref[...])
        _online_softmax_step(st, vt_ref[:, pl.ds(start, tq)], m_ref, acc_ref)

    _paired_loop(i, past)

    start = pl.multiple_of(i * tq, tq)
    st = _dot_nt(k_ref[0, pl.ds(start, tq), :], qs_ref[...])
    krow = lax.broadcasted_iota(jnp.int32, st.shape, 0)
    qcol = lax.broadcasted_iota(jnp.int32, st.shape, 1)
    qcol = jnp.where(qcol >= tq, qcol - tq, qcol)
    st = jnp.where(krow <= qcol, st, NEG_INF)
    _online_softmax_step(st, vt_ref[:, pl.ds(start, tq)], m_ref, acc_ref)

    lp = lam_ref[...]
    lam_init = lp[4:5, 0:1]
    lam = (jnp.exp(jnp.sum(lp[0:1] * lp[1:2], axis=1, keepdims=True))
           - jnp.exp(jnp.sum(lp[2:3] * lp[3:4], axis=1, keepdims=True)) + lam_init)
    o = _normalized(acc_ref)
    od = o[:, 0:tq] - lam * o[:, tq:2 * tq]
    ms = jnp.mean(od * od, axis=0, keepdims=True)
    y = od * lax.rsqrt(ms + SUBLN_EPS) * gsub_ref[...] * (1.0 - lam_init)
    o_ref[...] = y.T.astype(o_ref.dtype)


def diff_attention(proj, lam_params, g_sub):
    _, s, _ = proj.shape
    tq = _tile(s, (512, 256, 128))
    return pl.pallas_call(
        functools.partial(_diff_attn_kernel, tq=tq),
        out_shape=jax.ShapeDtypeStruct((s, BRANCH_WIDTH), BF16),
        grid=(HEADS, s // tq),
        in_specs=[pl.BlockSpec((1, tq, LANES), lambda h, i: (SLAB_AQ + h, i, 0)),
                  pl.BlockSpec((1, s, LANES), lambda h, i: (SLAB_AK + h, 0, 0)),
                  pl.BlockSpec((1, s, LANES), lambda h, i: (SLAB_AV + h, 0, 0)),
                  pl.BlockSpec((8, LANES), lambda h, i: (0, 0)),
                  pl.BlockSpec((HEAD_DIM, 1), lambda h, i: (0, 0))],
        out_specs=pl.BlockSpec((tq, LANES), lambda h, i: (i, h)),
        scratch_shapes=[pltpu.VMEM((VT_ROWS, s), BF16),
                        pltpu.VMEM((2 * tq, LANES), BF16),
                        pltpu.VMEM((1, 2 * tq), F32),
                        pltpu.VMEM((VT_ROWS, 2 * tq), F32)],
        compiler_params=_params(("arbitrary", "arbitrary")),
        name="diff_attn",
    )(proj, proj, proj, lam_params, g_sub.reshape(HEAD_DIM, 1))


def _moba_attn_kernel(q_ref, k_ref, v_ref, o_ref,
                      vt_ref, kmean_ref, bias_ref, m_ref, acc_ref, *, tq, tk, nb):
    i = pl.program_id(1)
    blk = MOBA_BLOCK
    blk_shift = blk.bit_length() - 1
    blk_per_tile = tk // blk
    own_tiles = tq // tk

    @pl.when(i == 0)
    def _():
        _fill_v_transposed(v_ref, vt_ref, tk)
        for n in range(nb):
            kmean_ref[n:n + 1, :] = jnp.mean(
                k_ref[0, n * blk:(n + 1) * blk, :].astype(F32), axis=0, keepdims=True)

    q = q_ref[0]
    km = kmean_ref[...]
    km_hi = km.astype(BF16)
    km_lo = (km - km_hi.astype(F32)).astype(BF16)
    gate = _dot_nt(km_hi, q) + _dot_nt(km_lo, q)
    bidx = lax.broadcasted_iota(jnp.int32, gate.shape, 0)
    qpos = i * tq + lax.broadcasted_iota(jnp.int32, gate.shape, 1)
    own = jnp.right_shift(qpos, blk_shift)
    past_blk = bidx < own
    gate = jnp.where(past_blk, gate, NEG_INF)
    rank = jnp.zeros(gate.shape, jnp.int32)
    for mth in range(nb):
        gm = gate[mth:mth + 1, :]
        beats = jnp.logical_or(gm > gate, jnp.logical_and(gm == gate, bidx > mth))
        rank = rank + beats.astype(jnp.int32)
    selected = jnp.logical_and(rank < MOBA_TOP_K, past_blk)
    bias_ref[...] = jnp.where(selected, 0.0, NEG_INF).astype(F32)
    _init_stats(m_ref, acc_ref)

    def scores(j):
        start = pl.multiple_of(j * tk, tk)
        st = _dot_nt(k_ref[0, pl.ds(start, tk), :], q)
        return st, vt_ref[:, pl.ds(start, tk)]

    def past(j):
        st, vt = scores(j)
        st = jnp.concatenate(
            [st[b * blk:(b + 1) * blk, :] + bias_ref[pl.ds(j * blk_per_tile + b, 1), :]
             for b in range(blk_per_tile)], axis=0)
        _online_softmax_step(st, vt, m_ref, acc_ref)

    first_own_tile = i * own_tiles
    _paired_loop(first_own_tile, past)

    own_row = jnp.right_shift(i * tq + lax.broadcasted_iota(jnp.int32, (1, tq), 1), blk_shift)
    for t in range(own_tiles):
        j = first_own_tile + t
        st, vt = scores(j)
        kpos = j * tk + lax.broadcasted_iota(jnp.int32, st.shape, 0)
        qp = i * tq + lax.broadcasted_iota(jnp.int32, st.shape, 1)
        st = jnp.where(kpos <= qp, st, NEG_INF)
        parts = []
        for b in range(blk_per_tile):
            n = j * blk_per_tile + b
            row_bias = jnp.where(own_row > n, bias_ref[pl.ds(n, 1), :],
                                 jnp.where(own_row == n, 0.0, NEG_INF))
            parts.append(st[b * blk:(b + 1) * blk, :] + row_bias)
        _online_softmax_step(jnp.concatenate(parts, axis=0), vt, m_ref, acc_ref)

    o_ref[...] = _normalized(acc_ref).T.astype(o_ref.dtype)


def moba_attention(proj):
    _, s, _ = proj.shape
    assert s % MOBA_BLOCK == 0 and MOBA_BLOCK & (MOBA_BLOCK - 1) == 0
    nb = s // MOBA_BLOCK
    tq = _tile(s, (1024, 512, 256))
    tk = _tile(tq, (512, 256))
    return pl.pallas_call(
        functools.partial(_moba_attn_kernel, tq=tq, tk=tk, nb=nb),
        out_shape=jax.ShapeDtypeStruct((s, BRANCH_WIDTH), BF16),
        grid=(HEADS, s // tq),
        in_specs=[pl.BlockSpec((1, tq, LANES), lambda h, i: (SLAB_BQ + h, i, 0)),
                  pl.BlockSpec((1, s, LANES), lambda h, i: (SLAB_BK + h, 0, 0)),
                  pl.BlockSpec((1, s, LANES), lambda h, i: (SLAB_BV + h, 0, 0))],
        out_specs=pl.BlockSpec((tq, LANES), lambda h, i: (i, h)),
        scratch_shapes=[pltpu.VMEM((VT_ROWS, s), BF16),
                        pltpu.VMEM((nb, LANES), F32),
                        pltpu.VMEM((nb, tq), F32),
                        pltpu.VMEM((1, tq), F32),
                        pltpu.VMEM((VT_ROWS, tq), F32)],
        compiler_params=_params(("arbitrary", "arbitrary")),
        name="moba_attn",
    )(proj, proj, proj)


def _dilated_kernel(q_ref, k_ref, v_ref, kp_ref, vp_ref, o_ref, lse_ref,
                    qf_ref, kf_ref, vf_ref, kpf_ref, vpf_ref, *, tq, band, dilation):
    i = pl.program_id(1)
    r = pl.program_id(2)

    @pl.when(r == 0)
    def _():
        qf_ref[...] = q_ref[0].astype(F32)
        kf_ref[...] = k_ref[0].astype(F32)
        vf_ref[...] = v_ref[0].astype(F32)
        kpf_ref[...] = kp_ref[0].astype(F32)
        vpf_ref[...] = vp_ref[0].astype(F32)

    cls = lambda ref, n: ref[pl.ds(r, n, stride=dilation), :]
    q = cls(qf_ref, tq).astype(BF16)
    st_c = _dot_nt(cls(kf_ref, tq).astype(BF16), q)
    st_p = _dot_nt(cls(kpf_ref, band).astype(BF16), q)
    qi = lax.broadcasted_iota(jnp.int32, st_c.shape, 1)
    kj = lax.broadcasted_iota(jnp.int32, st_c.shape, 0)
    rel = qi - kj
    st_c = jnp.where(jnp.logical_and(rel >= 0, rel <= band), st_c, NEG_INF)
    qi_p = lax.broadcasted_iota(jnp.int32, st_p.shape, 1)
    kj_p = lax.broadcasted_iota(jnp.int32, st_p.shape, 0) - band
    ok_p = jnp.logical_and(qi_p - kj_p <= band, i > 0)
    st_p = jnp.where(ok_p, st_p, NEG_INF)
    m = jnp.maximum(jnp.max(st_c, axis=0, keepdims=True), jnp.max(st_p, axis=0, keepdims=True))
    e_c = jnp.exp2(st_c - m)
    e_p = jnp.exp2(st_p - m)
    den = jnp.sum(e_c, axis=0, keepdims=True) + jnp.sum(e_p, axis=0, keepdims=True)
    inv = 1.0 / den
    p_c = (e_c * inv).astype(BF16)
    p_p = (e_p * inv).astype(BF16)
    o = (jnp.dot(cls(vf_ref, tq).T.astype(BF16), p_c, preferred_element_type=F32)
         + jnp.dot(cls(vpf_ref, band).T.astype(BF16), p_p, preferred_element_type=F32))
    o_ref[0, pl.ds(r, tq, stride=dilation), :] = o.T
    lse2 = m + jnp.log2(den)
    lse_ref[0, pl.ds(r, tq, stride=dilation), :] = jnp.broadcast_to(lse2, (LANES, tq)).T


def dilated_group(proj, g, window, dilation):
    _, s, _ = proj.shape
    band = window // dilation
    assert band == LANES and s % (band * dilation) == 0
    length = s // dilation
    tq = _tile(length, (512, 256, 128))
    per = tq // band
    rows, prev_rows = tq * dilation, band * dilation
    sq, sk, sv = SLAB_CQ + g * HEADS, SLAB_CK + g * HEADS, SLAB_CV + g * HEADS
    cur = lambda base: _resident((1, rows, LANES), lambda h, i, r: (base + h, i, 0))
    prev = lambda base: _resident(
        (1, prev_rows, LANES), lambda h, i, r: (base + h, jnp.maximum(i * per - 1, 0), 0))
    out_spec = pl.BlockSpec((1, rows, LANES), lambda h, i, r: (h, i, 0))
    return pl.pallas_call(
        functools.partial(_dilated_kernel, tq=tq, band=band, dilation=dilation),
        out_shape=(jax.ShapeDtypeStruct((HEADS, s, LANES), F32),
                   jax.ShapeDtypeStruct((HEADS, s, LANES), F32)),
        grid=(HEADS, length // tq, dilation),
        in_specs=[cur(sq), cur(sk), cur(sv), prev(sk), prev(sv)],
        out_specs=(out_spec, out_spec),
        scratch_shapes=[pltpu.VMEM((rows, LANES), F32)] * 3 + [pltpu.VMEM((prev_rows, LANES), F32)] * 2,
        compiler_params=_params(("arbitrary", "arbitrary", "arbitrary")),
        name=f"dilated_d{dilation}",
    )(proj, proj, proj, proj, proj)


def _dil_merge_kernel(*refs):
    o_refs, lse_refs, out_ref = refs[:N_DIL], refs[N_DIL:2 * N_DIL], refs[2 * N_DIL]
    lses = [r[0] for r in lse_refs]
    m = functools.reduce(jnp.maximum, lses)
    es = [jnp.exp2(x - m) for x in lses]
    inv = 1.0 / functools.reduce(lambda a, b: a + b, es)
    acc = None
    for e, o_ref in zip(es, o_refs):
        term = (e * inv) * o_ref[0]
        acc = term if acc is None else acc + term
    out_ref[...] = acc.astype(out_ref.dtype)


def dilated_mixture(proj):
    _, s, _ = proj.shape
    outs, lses = [], []
    for g, (window, dilation) in enumerate(DIL_PATTERNS):
        o, lse = dilated_group(proj, g, window, dilation)
        outs.append(o)
        lses.append(lse)
    tm = _tile(s, (1024, 512, 256))
    spec = pl.BlockSpec((1, tm, LANES), lambda h, i: (h, i, 0))
    return pl.pallas_call(
        _dil_merge_kernel,
        out_shape=jax.ShapeDtypeStruct((s, BRANCH_WIDTH), BF16),
        grid=(HEADS, s // tm),
        in_specs=[spec] * (2 * N_DIL),
        out_specs=pl.BlockSpec((tm, LANES), lambda h, i: (i, h)),
        compiler_params=_params(("parallel", "parallel")),
        name="dilated_merge",
    )(*outs, *lses)


def _gate_mix_kernel(h_ref, oa_ref, ob_ref, oc_ref, wg0_ref, wg1_ref, wg2_ref,
                     b0_ref, b1_ref, b2_ref, wb_ref, z_ref):
    h = h_ref[...]
    z = None
    for n, (o_ref, wg_ref, b_ref) in enumerate(((oa_ref, wg0_ref, b0_ref),
                                                (ob_ref, wg1_ref, b1_ref),
                                                (oc_ref, wg2_ref, b2_ref))):
        gate = jax.nn.sigmoid(
            jnp.dot(h, wg_ref[...].astype(BF16), preferred_element_type=F32) + b_ref[...])
        y = jnp.dot(o_ref[...], wb_ref[n].astype(BF16), preferred_element_type=F32)
        z = gate * y if z is None else z + gate * y
    z_ref[...] = z.astype(z_ref.dtype)


def gate_mix(h, o_a, o_b, o_c, w_gate_all, b_gate_all, w_branch_all, layer):
    s, d = h.shape
    tm = _tile(s, (1024, 512, 256))
    tn = _tile(d, (256, 128))
    nj = d // tn
    o_spec = _resident((tm, BRANCH_WIDTH), lambda i, j: (i, 0))
    wg_spec = lambda n: pl.BlockSpec((None, d, tn), lambda i, j: (layer, 0, n * nj + j))
    b_spec = lambda n: pl.BlockSpec((None, 1, tn), lambda i, j: (layer, 0, n * nj + j))
    b3d = b_gate_all.reshape(b_gate_all.shape[0], 1, N_BRANCH * d)
    return pl.pallas_call(
        _gate_mix_kernel,
        out_shape=jax.ShapeDtypeStruct((s, d), BF16),
        grid=(s // tm, nj),
        in_specs=[_resident((tm, d), lambda i, j: (i, 0)), o_spec, o_spec, o_spec,
                  wg_spec(0), wg_spec(1), wg_spec(2), b_spec(0), b_spec(1), b_spec(2),
                  pl.BlockSpec((None, N_BRANCH, BRANCH_WIDTH, tn), lambda i, j: (layer, 0, 0, j))],
        out_specs=pl.BlockSpec((tm, tn), lambda i, j: (i, j)),
        compiler_params=_params(("parallel", "arbitrary")),
        name="gate_mix",
    )(h, o_a, o_b, o_c, w_gate_all, w_gate_all, w_gate_all, b3d, b3d, b3d, w_branch_all)


def _residual_matmul_kernel(a_ref, w_ref, x_ref, o_ref, *, k_chunks):
    kc = a_ref.shape[1] // k_chunks
    acc = x_ref[...]
    for c in range(k_chunks):
        acc = acc + jnp.dot(a_ref[:, c * kc:(c + 1) * kc],
                            w_ref[c * kc:(c + 1) * kc, :].astype(BF16),
                            preferred_element_type=F32)
    o_ref[...] = acc


def residual_matmul(a, w_all, layer, x, tm_prefs, tn_prefs, k_chunks=1):
    s, k = a.shape
    d = w_all.shape[2]
    tm = _tile(s, tm_prefs)
    tn = _tile(d, tn_prefs)
    assert k % (k_chunks * LANES) == 0
    return pl.pallas_call(
        functools.partial(_residual_matmul_kernel, k_chunks=k_chunks),
        out_shape=jax.ShapeDtypeStruct((s, d), F32),
        grid=(s // tm, d // tn),
        in_specs=[_resident((tm, k), lambda i, j: (i, 0)),
                  pl.BlockSpec((None, k, tn), lambda i, j: (layer, 0, j)),
                  pl.BlockSpec((tm, tn), lambda i, j: (i, j))],
        out_specs=pl.BlockSpec((tm, tn), lambda i, j: (i, j)),
        input_output_aliases={2: 0},
        compiler_params=_params(("parallel", "arbitrary")),
        name="residual_matmul",
    )(a, w_all, x)


def _swiglu_kernel(h_ref, wg_ref, wu_ref, o_ref):
    h = h_ref[...]
    g = jnp.dot(h, wg_ref[...].astype(BF16), preferred_element_type=F32)
    u = jnp.dot(h, wu_ref[...].astype(BF16), preferred_element_type=F32)
    o_ref[...] = (g * jax.nn.sigmoid(g) * u).astype(o_ref.dtype)


def swiglu_up(h, w_g_all, w_u_all, layer):
    s, d = h.shape
    f = w_g_all.shape[2]
    tm = _tile(s, (2048, 1024, 512, 256))
    tn = _tile(f, (256, 128))
    w_spec = pl.BlockSpec((None, d, tn), lambda i, j: (layer, 0, j))
    return pl.pallas_call(
        _swiglu_kernel,
        out_shape=jax.ShapeDtypeStruct((s, f), BF16),
        grid=(s // tm, f // tn),
        in_specs=[_resident((tm, d), lambda i, j: (i, 0)), w_spec, w_spec],
        out_specs=pl.BlockSpec((tm, tn), lambda i, j: (i, j)),
        compiler_params=_params(("parallel", "arbitrary")),
        name="swiglu_up",
    )(h, w_g_all, w_u_all)


def kernel(x, g_mix, w_in, lam_q1, lam_k1, lam_q2, lam_k2, g_subln, w_gate, b_gate, w_branch,
           w_out, g_ffn, w_ffn_gate, w_ffn_up, w_ffn_down, g_final):
    b, s, d = x.shape
    depth = w_in.shape[0]
    tabs = rope_tables(s)
    down_chunks = 2 if w_ffn_down.shape[1] % (2 * LANES) == 0 else 1
    outs = []
    for bi in range(b):
        xr = x.reshape(s, d) if b == 1 else x[bi]
        for l in range(depth):
            lam_init = 0.8 - 0.6 * math.exp(-0.3 * l)
            pad = lambda v: jnp.pad(v.astype(F32), (0, LANES - DIFF_QK_DIM))
            lam_params = jnp.stack(
                [pad(lam_q1[l]), pad(lam_k1[l]), pad(lam_q2[l]), pad(lam_k2[l]),
                 jnp.full((LANES,), lam_init, F32)]
                + [jnp.zeros((LANES,), F32)] * 3)
            h = rmsnorm(xr, g_mix[l], NORM_EPS, BF16)
            proj = in_proj(h, w_in, l, tabs)
            o_a = diff_attention(proj, lam_params, g_subln[l])
            o_b = moba_attention(proj)
            o_c = dilated_mixture(proj)
            z = gate_mix(h, o_a, o_b, o_c, w_gate, b_gate, w_branch, l)
            xr = residual_matmul(z, w_out, l, xr, (2048, 1024, 512, 256), (256, 128))
            h2 = rmsnorm(xr, g_ffn[l], NORM_EPS, BF16)
            u = swiglu_up(h2, w_ffn_gate, w_ffn_up, l)
            xr = residual_matmul(u, w_ffn_down, l, xr, (1024, 512, 256), (256, 128), down_chunks)
        outs.append(rmsnorm(xr, g_final, NORM_EPS, x.dtype))
    return outs[0].reshape(b, s, d) if b == 1 else jnp.stack(outs, axis=0)
```

```python
import functools
import math

import jax
import jax.numpy as jnp
from jax import lax
from jax.experimental import pallas as pl
from jax.experimental.pallas import tpu as pltpu

HEAD_DIM = 128
DIFF_HEADS = 4
DIFF_QK_DIM = HEAD_DIM // 2
MOBA_HEADS = 4
MOBA_BLOCK = 256
MOBA_TOP_K = 3
DIL_PATTERNS = ((128, 1), (512, 4), (2048, 16))
DIL_HEADS = 4
N_BRANCH = 3
BRANCH_WIDTH = 512
ROPE_THETA = 10000.0
NORM_EPS = 1e-6
SUBLN_EPS = 1e-5
NEG_INF = -1e30
LOG2_E = math.log2(math.e)

N_DIL = len(DIL_PATTERNS)
HEADS = 4
SLAB_AQ, SLAB_AK, SLAB_AV = 0, 4, 8
SLAB_BQ, SLAB_BK, SLAB_BV = 12, 16, 20
SLAB_CQ, SLAB_CK, SLAB_CV = 24, 36, 48
N_SLABS = 60
PROJ_TILE = 512

LANES = 128
BF16_SUBLANES = 16
VMEM_LIMIT_BYTES = 58 * 1024 * 1024

F32 = jnp.float32
BF16 = jnp.bfloat16


def _params(sem, vmem=VMEM_LIMIT_BYTES):
    return pltpu.CompilerParams(dimension_semantics=sem, vmem_limit_bytes=vmem)


def _tile(n, prefs):
    for p in prefs:
        if n % p == 0:
            return p
    return n


def _resident(block_shape, index_map):
    return pl.BlockSpec(block_shape, index_map, pipeline_mode=pl.Buffered(1))


def _lane_folded_sumsq(x):
    sq = x * x
    out = sq[:, 0:LANES]
    for c in range(1, x.shape[1] // LANES):
        out = out + sq[:, c * LANES:(c + 1) * LANES]
    return out


def _rstd(ssq_ref, d, eps):
    return lax.rsqrt(jnp.sum(ssq_ref[...], axis=1, keepdims=True) * (1.0 / d) + eps)


def _scaled_bf16(w_ref, grep_ref):
    w, g = w_ref[...], grep_ref[...]
    return jnp.concatenate(
        [w[:, c * LANES:(c + 1) * LANES] * g for c in range(w.shape[1] // LANES)], axis=1).astype(BF16)


def _lane_replicated(g_all):
    return jnp.broadcast_to(g_all[:, :, None], g_all.shape + (LANES,))


def _norm_prep_kernel(x_ref, xb_ref, ssq_ref):
    x = x_ref[...]
    xb_ref[...] = x.astype(BF16)
    ssq_ref[...] = _lane_folded_sumsq(x)


def norm_prep(x):
    s, d = x.shape
    tm = _tile(s, (256, 128, 8))
    return pl.pallas_call(
        _norm_prep_kernel,
        out_shape=(jax.ShapeDtypeStruct((s, d), BF16), jax.ShapeDtypeStruct((s, LANES), F32)),
        grid=(s // tm,),
        in_specs=[pl.BlockSpec((tm, d), lambda i: (i, 0))],
        out_specs=(pl.BlockSpec((tm, d), lambda i: (i, 0)), pl.BlockSpec((tm, LANES), lambda i: (i, 0))),
        compiler_params=_params(("parallel",)),
        name="norm_prep",
    )(x)


def _rmsnorm_kernel(x_ref, g_ref, o_ref, *, eps):
    x = x_ref[...]
    ms = jnp.mean(x * x, axis=-1, keepdims=True)
    o_ref[...] = (x * lax.rsqrt(ms + eps) * g_ref[...]).astype(o_ref.dtype)


def rmsnorm(x, g, eps, out_dtype):
    s, d = x.shape
    tm = _tile(s, (256, 128, 8))
    return pl.pallas_call(
        functools.partial(_rmsnorm_kernel, eps=eps),
        out_shape=jax.ShapeDtypeStruct((s, d), out_dtype),
        grid=(s // tm,),
        in_specs=[pl.BlockSpec((tm, d), lambda i: (i, 0)),
                  pl.BlockSpec((1, d), lambda i: (0, 0))],
        out_specs=pl.BlockSpec((tm, d), lambda i: (i, 0)),
        compiler_params=_params(("parallel",)),
        name="rmsnorm",
    )(x, g.reshape(1, d))


_ROPE64_Q, _ROPE64_K = (0,), (1,)
_ROPE128_Q, _ROPE128_K = (3, 6, 7, 8), (4, 9, 10, 11)


def _rot_half_128(x):
    return pltpu.roll(x, 64, 1)


def _rot_half_64(x):
    lane = lax.broadcasted_iota(jnp.int32, x.shape, 1)
    return jnp.where((lane & 32) == 0, pltpu.roll(x, 96, 1), pltpu.roll(x, 32, 1))


def _in_proj_kernel(xb_ref, ssq_ref, grep_ref, w_ref, c128_ref, s128_ref, c64_ref, s64_ref, o_ref):
    j = pl.program_id(1)
    d = xb_ref.shape[1]
    acc = jnp.dot(xb_ref[...], _scaled_bf16(w_ref, grep_ref), preferred_element_type=F32)
    acc = acc * _rstd(ssq_ref, d, NORM_EPS)

    def emit(fn):
        for c in range(PROJ_TILE // LANES):
            o_ref[c] = fn(acc[:, c * LANES:(c + 1) * LANES]).astype(o_ref.dtype)

    def is_in(tiles):
        cond = j == tiles[0]
        for t in tiles[1:]:
            cond = jnp.logical_or(cond, j == t)
        return cond

    def rope128(scale):
        def fn(xs):
            return (xs * c128_ref[...] + _rot_half_128(xs) * s128_ref[...]) * scale
        return fn

    def rope64(scale):
        def fn(xs):
            return (xs * c64_ref[...] + _rot_half_64(xs) * s64_ref[...]) * scale
        return fn

    roped = _ROPE64_Q + _ROPE64_K + _ROPE128_Q + _ROPE128_K
    pl.when(is_in(_ROPE64_Q))(lambda: emit(rope64(DIFF_QK_DIM ** -0.5 * LOG2_E)))
    pl.when(is_in(_ROPE64_K))(lambda: emit(rope64(1.0)))
    pl.when(is_in(_ROPE128_Q))(lambda: emit(rope128(HEAD_DIM ** -0.5 * LOG2_E)))
    pl.when(is_in(_ROPE128_K))(lambda: emit(rope128(1.0)))
    pl.when(jnp.logical_not(is_in(roped)))(lambda: emit(lambda xs: xs))


def in_proj(xb, ssq, g_all, w_all, layer, tabs):
    s, d = xb.shape
    n = w_all.shape[2]
    tm = _tile(s, (2048, 1024, 512, 256))
    tab_spec = _resident((tm, LANES), lambda i, j: (i, 0))
    return pl.pallas_call(
        _in_proj_kernel,
        out_shape=jax.ShapeDtypeStruct((n // LANES, s, LANES), BF16),
        grid=(s // tm, n // PROJ_TILE),
        in_specs=[_resident((tm, d), lambda i, j: (i, 0)),
                  _resident((tm, LANES), lambda i, j: (i, 0)),
                  _resident((None, d, LANES), lambda i, j: (layer, 0, 0)),
                  pl.BlockSpec((None, d, PROJ_TILE), lambda i, j: (layer, 0, j)),
                  tab_spec, tab_spec, tab_spec, tab_spec],
        out_specs=pl.BlockSpec((PROJ_TILE // LANES, tm, LANES), lambda i, j: (j, i, 0)),
        compiler_params=_params(("parallel", "arbitrary")),
        name="in_proj",
    )(xb, ssq, _lane_replicated(g_all), w_all, *tabs)


def rope_tables(seq):
    def table(dim):
        inv = ROPE_THETA ** (-jnp.arange(0, dim, 2, dtype=F32) / dim)
        ang = jnp.arange(seq, dtype=F32)[:, None] * inv[None, :]
        cos, sin = jnp.cos(ang), jnp.sin(ang)
        reps = LANES // dim
        return (jnp.tile(jnp.concatenate([cos, cos], axis=1), (1, reps)),
                jnp.tile(jnp.concatenate([-sin, sin], axis=1), (1, reps)))
    c128, s128 = table(HEAD_DIM)
    c64, s64 = table(DIFF_QK_DIM)
    return c128, s128, c64, s64


VT_ROWS = HEAD_DIM + BF16_SUBLANES


def _transpose_bf16(x):
    return x.astype(F32).T.astype(BF16)


def _dot_nt(a, b):
    return lax.dot_general(a, b, (((1,), (1,)), ((), ())), preferred_element_type=F32)


def _fill_v_transposed(v_ref, vt_ref, chunk):
    seq = v_ref.shape[1]
    for c in range(seq // chunk):
        vt_ref[0:HEAD_DIM, c * chunk:(c + 1) * chunk] = _transpose_bf16(
            v_ref[0, c * chunk:(c + 1) * chunk, :])
    vt_ref[HEAD_DIM:VT_ROWS, :] = jnp.ones((BF16_SUBLANES, seq), BF16)


def _softmax_step(st, vt, carry):
    m_old, acc = carry
    m_new = jnp.maximum(m_old, jnp.max(st, axis=0, keepdims=True))
    alpha = jnp.exp2(m_old - m_new)
    p = jnp.exp2(st - m_new).astype(BF16)
    return m_new, alpha * acc + jnp.dot(vt, p, preferred_element_type=F32)


def _init_carry(nq):
    return jnp.full((1, nq), NEG_INF, F32), jnp.zeros((VT_ROWS, nq), F32)


def _normalized(carry):
    acc = carry[1]
    return acc[0:HEAD_DIM, :] / acc[HEAD_DIM:HEAD_DIM + 1, :]


def _paired_loop(n, step, carry):
    carry = lax.fori_loop(0, n // 2, lambda t, c: step(2 * t + 1, step(2 * t, c)), carry)
    return lax.cond(n % 2 == 1, lambda c: step(n - 1, c), lambda c: c, carry)


def _diff_attn_kernel(q_ref, k_ref, v_ref, lam_ref, gsub_ref, o_ref, vt_ref, qs_ref, *, tq):
    i = pl.program_id(1)

    @pl.when(i == 0)
    def _():
        _fill_v_transposed(v_ref, vt_ref, tq)

    q = q_ref[0]
    lane = lax.broadcasted_iota(jnp.int32, q.shape, 1)
    zero = jnp.zeros_like(q)
    qs_ref[0:tq, :] = jnp.where(lane < DIFF_QK_DIM, q, zero)
    qs_ref[tq:2 * tq, :] = jnp.where(lane >= DIFF_QK_DIM, q, zero)

    def tile(j):
        start = pl.multiple_of(j * tq, tq)
        return _dot_nt(k_ref[0, pl.ds(start, tq), :], qs_ref[...]), vt_ref[:, pl.ds(start, tq)]

    carry = _paired_loop(i, lambda j, c: _softmax_step(*tile(j), c), _init_carry(2 * tq))

    st, vt = tile(i)
    krow = lax.broadcasted_iota(jnp.int32, st.shape, 0)
    qcol = lax.broadcasted_iota(jnp.int32, st.shape, 1)
    qcol = jnp.where(qcol >= tq, qcol - tq, qcol)
    carry = _softmax_step(jnp.where(krow <= qcol, st, NEG_INF), vt, carry)

    lp = lam_ref[...]
    lam_init = lp[4:5, 0:1]
    lam = (jnp.exp(jnp.sum(lp[0:1] * lp[1:2], axis=1, keepdims=True))
           - jnp.exp(jnp.sum(lp[2:3] * lp[3:4], axis=1, keepdims=True)) + lam_init)
    o = _normalized(carry)
    od = o[:, 0:tq] - lam * o[:, tq:2 * tq]
    ms = jnp.mean(od * od, axis=0, keepdims=True)
    y = od * lax.rsqrt(ms + SUBLN_EPS) * gsub_ref[...] * (1.0 - lam_init)
    o_ref[...] = y.T.astype(o_ref.dtype)


def diff_attention(proj, lam_params, g_sub):
    _, s, _ = proj.shape
    tq = _tile(s, (512, 256, 128))
    return pl.pallas_call(
        functools.partial(_diff_attn_kernel, tq=tq),
        out_shape=jax.ShapeDtypeStruct((s, BRANCH_WIDTH), BF16),
        grid=(HEADS, s // tq),
        in_specs=[pl.BlockSpec((1, tq, LANES), lambda h, i: (SLAB_AQ + h, i, 0)),
                  pl.BlockSpec((1, s, LANES), lambda h, i: (SLAB_AK + h, 0, 0)),
                  pl.BlockSpec((1, s, LANES), lambda h, i: (SLAB_AV + h, 0, 0)),
                  pl.BlockSpec((8, LANES), lambda h, i: (0, 0)),
                  pl.BlockSpec((HEAD_DIM, 1), lambda h, i: (0, 0))],
        out_specs=pl.BlockSpec((tq, LANES), lambda h, i: (i, h)),
        scratch_shapes=[pltpu.VMEM((VT_ROWS, s), BF16),
                        pltpu.VMEM((2 * tq, LANES), BF16)],
        compiler_params=_params(("arbitrary", "arbitrary")),
        name="diff_attn",
    )(proj, proj, proj, lam_params, g_sub.reshape(HEAD_DIM, 1))


def _moba_attn_kernel(q_ref, k_ref, v_ref, o_ref, vt_ref, kmean_ref, bias_ref, *, tq, tk, nb):
    i = pl.program_id(1)
    blk = MOBA_BLOCK
    blk_shift = blk.bit_length() - 1
    blk_per_tile = tk // blk
    own_tiles = tq // tk

    @pl.when(i == 0)
    def _():
        _fill_v_transposed(v_ref, vt_ref, tk)
        for n in range(nb):
            kmean_ref[n:n + 1, :] = jnp.mean(
                k_ref[0, n * blk:(n + 1) * blk, :].astype(F32), axis=0, keepdims=True)

    q = q_ref[0]
    km = kmean_ref[...]
    km_hi = km.astype(BF16)
    km_lo = (km - km_hi.astype(F32)).astype(BF16)
    gate = _dot_nt(km_hi, q) + _dot_nt(km_lo, q)
    bidx = lax.broadcasted_iota(jnp.int32, gate.shape, 0)
    qpos = i * tq + lax.broadcasted_iota(jnp.int32, gate.shape, 1)
    own = jnp.right_shift(qpos, blk_shift)
    past_blk = bidx < own
    gate = jnp.where(past_blk, gate, NEG_INF)
    rank = jnp.zeros(gate.shape, jnp.int32)
    for mth in range(nb):
        gm = gate[mth:mth + 1, :]
        beats = jnp.logical_or(gm > gate, jnp.logical_and(gm == gate, bidx > mth))
        rank = rank + beats.astype(jnp.int32)
    selected = jnp.logical_and(rank < MOBA_TOP_K, past_blk)
    bias_ref[...] = jnp.where(selected, 0.0, NEG_INF).astype(F32)

    def scores(j):
        start = pl.multiple_of(j * tk, tk)
        st = _dot_nt(k_ref[0, pl.ds(start, tk), :], q)
        return st, vt_ref[:, pl.ds(start, tk)]

    def past(j, carry):
        st, vt = scores(j)
        st = jnp.concatenate(
            [st[b * blk:(b + 1) * blk, :] + bias_ref[pl.ds(j * blk_per_tile + b, 1), :]
             for b in range(blk_per_tile)], axis=0)
        return _softmax_step(st, vt, carry)

    first_own_tile = i * own_tiles
    carry = _paired_loop(first_own_tile, past, _init_carry(tq))

    own_row = jnp.right_shift(i * tq + lax.broadcasted_iota(jnp.int32, (1, tq), 1), blk_shift)
    for t in range(own_tiles):
        j = first_own_tile + t
        st, vt = scores(j)
        kpos = j * tk + lax.broadcasted_iota(jnp.int32, st.shape, 0)
        qp = i * tq + lax.broadcasted_iota(jnp.int32, st.shape, 1)
        st = jnp.where(kpos <= qp, st, NEG_INF)
        parts = []
        for b in range(blk_per_tile):
            n = j * blk_per_tile + b
            row_bias = jnp.where(own_row > n, bias_ref[pl.ds(n, 1), :],
                                 jnp.where(own_row == n, 0.0, NEG_INF))
            parts.append(st[b * blk:(b + 1) * blk, :] + row_bias)
        carry = _softmax_step(jnp.concatenate(parts, axis=0), vt, carry)

    o_ref[...] = _normalized(carry).T.astype(o_ref.dtype)


def moba_attention(proj):
    _, s, _ = proj.shape
    assert s % MOBA_BLOCK == 0 and MOBA_BLOCK & (MOBA_BLOCK - 1) == 0
    nb = s // MOBA_BLOCK
    tq = _tile(s, (1024, 512, 256))
    tk = _tile(tq, (512, 256))
    return pl.pallas_call(
        functools.partial(_moba_attn_kernel, tq=tq, tk=tk, nb=nb),
        out_shape=jax.ShapeDtypeStruct((s, BRANCH_WIDTH), BF16),
        grid=(HEADS, s // tq),
        in_specs=[pl.BlockSpec((1, tq, LANES), lambda h, i: (SLAB_BQ + h, i, 0)),
                  pl.BlockSpec((1, s, LANES), lambda h, i: (SLAB_BK + h, 0, 0)),
                  pl.BlockSpec((1, s, LANES), lambda h, i: (SLAB_BV + h, 0, 0))],
        out_specs=pl.BlockSpec((tq, LANES), lambda h, i: (i, h)),
        scratch_shapes=[pltpu.VMEM((VT_ROWS, s), BF16),
                        pltpu.VMEM((nb, LANES), F32),
                        pltpu.VMEM((nb, tq), F32)],
        compiler_params=_params(("arbitrary", "arbitrary")),
        name="moba_attn",
    )(proj, proj, proj)


def _dilated_kernel(q_ref, k_ref, v_ref, kp_ref, vp_ref, o_ref, lse_ref,
                    qf_ref, kf_ref, vf_ref, kpf_ref, vpf_ref, *, tq, band, dilation):
    i = pl.program_id(1)
    qf_ref[...] = q_ref[0].astype(F32)
    kf_ref[...] = k_ref[0].astype(F32)
    vf_ref[...] = v_ref[0].astype(F32)
    kpf_ref[...] = kp_ref[0].astype(F32)
    vpf_ref[...] = vp_ref[0].astype(F32)

    def one_class(r, carry):
        cls = lambda ref, n: ref[pl.ds(r, n, stride=dilation), :].astype(BF16)
        q = cls(qf_ref, tq)
        s_c = _dot_nt(q, cls(kf_ref, tq))
        s_p = _dot_nt(q, cls(kpf_ref, band))
        rel = (lax.broadcasted_iota(jnp.int32, s_c.shape, 0)
               - lax.broadcasted_iota(jnp.int32, s_c.shape, 1))
        s_c = jnp.where(jnp.logical_and(rel >= 0, rel <= band), s_c, NEG_INF)
        rel_p = (lax.broadcasted_iota(jnp.int32, s_p.shape, 0)
                 - lax.broadcasted_iota(jnp.int32, s_p.shape, 1) + band)
        ok_p = jnp.logical_and(rel_p <= band, i > 0)
        s_p = jnp.where(ok_p, s_p, NEG_INF)
        m = jnp.maximum(jnp.max(s_c, axis=1, keepdims=True), jnp.max(s_p, axis=1, keepdims=True))
        e_c = jnp.exp2(s_c - m)
        e_p = jnp.exp2(s_p - m)
        den = jnp.sum(e_c, axis=1, keepdims=True) + jnp.sum(e_p, axis=1, keepdims=True)
        o = (jnp.dot(e_c.astype(BF16), cls(vf_ref, tq), preferred_element_type=F32)
             + jnp.dot(e_p.astype(BF16), cls(vpf_ref, band), preferred_element_type=F32))
        o_ref[0, pl.ds(r, tq, stride=dilation), :] = o / den
        lse2 = m + jnp.log2(den)
        lse_ref[0, pl.ds(r, tq, stride=dilation), :] = jnp.broadcast_to(lse2, (tq, LANES))
        return carry

    lax.fori_loop(0, dilation, one_class, 0)


def dilated_group(proj, g, window, dilation):
    _, s, _ = proj.shape
    band = window // dilation
    assert band == LANES and s % (band * dilation) == 0
    length = s // dilation
    tq = _tile(length, (512, 256, 128))
    per = tq // band
    rows, prev_rows = tq * dilation, band * dilation
    sq, sk, sv = SLAB_CQ + g * HEADS, SLAB_CK + g * HEADS, SLAB_CV + g * HEADS
    cur = lambda base: pl.BlockSpec((1, rows, LANES), lambda h, i: (base + h, i, 0))
    prev = lambda base: pl.BlockSpec(
        (1, prev_rows, LANES), lambda h, i: (base + h, jnp.maximum(i * per - 1, 0), 0))
    out_spec = pl.BlockSpec((1, rows, LANES), lambda h, i: (h, i, 0))
    return pl.pallas_call(
        functools.partial(_dilated_kernel, tq=tq, band=band, dilation=dilation),
        out_shape=(jax.ShapeDtypeStruct((HEADS, s, LANES), F32),
                   jax.ShapeDtypeStruct((HEADS, s, LANES), F32)),
        grid=(HEADS, length // tq),
        in_specs=[cur(sq), cur(sk), cur(sv), prev(sk), prev(sv)],
        out_specs=(out_spec, out_spec),
        scratch_shapes=[pltpu.VMEM((rows, LANES), F32)] * 3 + [pltpu.VMEM((prev_rows, LANES), F32)] * 2,
        compiler_params=_params(("parallel", "parallel")),
        name=f"dilated_d{dilation}",
    )(proj, proj, proj, proj, proj)


def _dil_merge_kernel(*refs):
    o_refs, lse_refs, out_ref = refs[:N_DIL], refs[N_DIL:2 * N_DIL], refs[2 * N_DIL]
    lses = [r[0] for r in lse_refs]
    m = functools.reduce(jnp.maximum, lses)
    es = [jnp.exp2(x - m) for x in lses]
    inv = 1.0 / functools.reduce(lambda a, b: a + b, es)
    acc = None
    for e, o_ref in zip(es, o_refs):
        term = (e * inv) * o_ref[0]
        acc = term if acc is None else acc + term
    out_ref[...] = acc.astype(out_ref.dtype)


def dilated_mixture(proj):
    _, s, _ = proj.shape
    outs, lses = [], []
    for g, (window, dilation) in enumerate(DIL_PATTERNS):
        o, lse = dilated_group(proj, g, window, dilation)
        outs.append(o)
        lses.append(lse)
    tm = _tile(s, (1024, 512, 256))
    spec = pl.BlockSpec((1, tm, LANES), lambda h, i: (h, i, 0))
    return pl.pallas_call(
        _dil_merge_kernel,
        out_shape=jax.ShapeDtypeStruct((s, BRANCH_WIDTH), BF16),
        grid=(HEADS, s // tm),
        in_specs=[spec] * (2 * N_DIL),
        out_specs=pl.BlockSpec((tm, LANES), lambda h, i: (i, h)),
        compiler_params=_params(("parallel", "parallel")),
        name="dilated_merge",
    )(*outs, *lses)


def _gate_mix_kernel(xb_ref, ssq_ref, grep_ref, oa_ref, ob_ref, oc_ref, wg0_ref, wg1_ref, wg2_ref,
                     b0_ref, b1_ref, b2_ref, wb_ref, z_ref):
    xb = xb_ref[...]
    rstd = _rstd(ssq_ref, xb.shape[1], NORM_EPS)
    z = None
    for n, (o_ref, wg_ref, b_ref) in enumerate(((oa_ref, wg0_ref, b0_ref),
                                                (ob_ref, wg1_ref, b1_ref),
                                                (oc_ref, wg2_ref, b2_ref))):
        pre = jnp.dot(xb, _scaled_bf16(wg_ref, grep_ref), preferred_element_type=F32)
        gate = jax.nn.sigmoid(pre * rstd + b_ref[...])
        y = jnp.dot(o_ref[...], wb_ref[n].astype(BF16), preferred_element_type=F32)
        z = gate * y if z is None else z + gate * y
    z_ref[...] = z.astype(z_ref.dtype)


def gate_mix(xb, ssq, g_all, o_a, o_b, o_c, w_gate_all, b_gate_all, w_branch_all, layer):
    s, d = xb.shape
    assert N_BRANCH == 3
    tm = _tile(s, (1024, 512, 256))
    tn = _tile(d, (256, 128))
    nj = d // tn
    o_spec = _resident((tm, BRANCH_WIDTH), lambda i, j: (i, 0))
    wg_spec = lambda n: pl.BlockSpec((None, d, tn), lambda i, j: (layer, 0, n * nj + j))
    b_spec = lambda n: pl.BlockSpec((None, 1, tn), lambda i, j: (layer, 0, n * nj + j))
    b3d = b_gate_all.reshape(b_gate_all.shape[0], 1, N_BRANCH * d)
    return pl.pallas_call(
        _gate_mix_kernel,
        out_shape=jax.ShapeDtypeStruct((s, d), BF16),
        grid=(s // tm, nj),
        in_specs=[_resident((tm, d), lambda i, j: (i, 0)),
                  _resident((tm, LANES), lambda i, j: (i, 0)),
                  _resident((None, d, LANES), lambda i, j: (layer, 0, 0)),
                  o_spec, o_spec, o_spec,
                  wg_spec(0), wg_spec(1), wg_spec(2), b_spec(0), b_spec(1), b_spec(2),
                  pl.BlockSpec((None, N_BRANCH, BRANCH_WIDTH, tn), lambda i, j: (layer, 0, 0, j))],
        out_specs=pl.BlockSpec((tm, tn), lambda i, j: (i, j)),
        compiler_params=_params(("parallel", "arbitrary")),
        name="gate_mix",
    )(xb, ssq, _lane_replicated(g_all), o_a, o_b, o_c, w_gate_all, w_gate_all, w_gate_all,
      b3d, b3d, b3d, w_branch_all)


def _residual_matmul_kernel(a_ref, w_ref, x_ref, o_ref, xb_ref, ssq_ref, *, k_chunks):
    j = pl.program_id(1)
    kc = a_ref.shape[1] // k_chunks
    acc = x_ref[...]
    for c in range(k_chunks):
        acc = acc + jnp.dot(a_ref[:, c * kc:(c + 1) * kc],
                            w_ref[c * kc:(c + 1) * kc, :].astype(BF16),
                            preferred_element_type=F32)
    o_ref[...] = acc
    xb_ref[...] = acc.astype(BF16)
    part = _lane_folded_sumsq(acc)

    @pl.when(j == 0)
    def _():
        ssq_ref[...] = part

    @pl.when(j > 0)
    def _():
        ssq_ref[...] += part


def residual_matmul(a, w_all, layer, x, tm_prefs, tn_prefs, k_chunks=1):
    s, k = a.shape
    d = w_all.shape[2]
    tm = _tile(s, tm_prefs)
    tn = _tile(d, tn_prefs)
    assert k % (k_chunks * LANES) == 0
    return pl.pallas_call(
        functools.partial(_residual_matmul_kernel, k_chunks=k_chunks),
        out_shape=(jax.ShapeDtypeStruct((s, d), F32), jax.ShapeDtypeStruct((s, d), BF16),
                   jax.ShapeDtypeStruct((s, LANES), F32)),
        grid=(s // tm, d // tn),
        in_specs=[_resident((tm, k), lambda i, j: (i, 0)),
                  pl.BlockSpec((None, k, tn), lambda i, j: (layer, 0, j)),
                  pl.BlockSpec((tm, tn), lambda i, j: (i, j))],
        out_specs=(pl.BlockSpec((tm, tn), lambda i, j: (i, j)),
                   pl.BlockSpec((tm, tn), lambda i, j: (i, j)),
                   pl.BlockSpec((tm, LANES), lambda i, j: (i, 0))),
        input_output_aliases={2: 0},
        compiler_params=_params(("parallel", "arbitrary")),
        name="residual_matmul",
    )(a, w_all, x)


def _swiglu_kernel(xb_ref, ssq_ref, grep_ref, wg_ref, wu_ref, o_ref):
    xb = xb_ref[...]
    rstd = _rstd(ssq_ref, xb.shape[1], NORM_EPS)
    g = jnp.dot(xb, _scaled_bf16(wg_ref, grep_ref), preferred_element_type=F32) * rstd
    u = jnp.dot(xb, _scaled_bf16(wu_ref, grep_ref), preferred_element_type=F32) * rstd
    o_ref[...] = (g * jax.nn.sigmoid(g) * u).astype(o_ref.dtype)


def swiglu_up(xb, ssq, g_all, w_g_all, w_u_all, layer):
    s, d = xb.shape
    f = w_g_all.shape[2]
    tm = _tile(s, (2048, 1024, 512, 256))
    tn = _tile(f, (256, 128))
    w_spec = pl.BlockSpec((None, d, tn), lambda i, j: (layer, 0, j))
    return pl.pallas_call(
        _swiglu_kernel,
        out_shape=jax.ShapeDtypeStruct((s, f), BF16),
        grid=(s // tm, f // tn),
        in_specs=[_resident((tm, d), lambda i, j: (i, 0)),
                  _resident((tm, LANES), lambda i, j: (i, 0)),
                  _resident((None, d, LANES), lambda i, j: (layer, 0, 0)),
                  w_spec, w_spec],
        out_specs=pl.BlockSpec((tm, tn), lambda i, j: (i, j)),
        compiler_params=_params(("parallel", "arbitrary")),
        name="swiglu_up",
    )(xb, ssq, _lane_replicated(g_all), w_g_all, w_u_all)


def kernel(x, g_mix, w_in, lam_q1, lam_k1, lam_q2, lam_k2, g_subln, w_gate, b_gate, w_branch,
           w_out, g_ffn, w_ffn_gate, w_ffn_up, w_ffn_down, g_final):
    b, s, d = x.shape
    depth = w_in.shape[0]
    tabs = rope_tables(s)
    down_chunks = 2 if w_ffn_down.shape[1] % (2 * LANES) == 0 else 1
    outs = []
    for bi in range(b):
        xr = x.reshape(s, d) if b == 1 else x[bi]
        xb, ssq = norm_prep(xr)
        for l in range(depth):
            lam_init = 0.8 - 0.6 * math.exp(-0.3 * l)
            pad = lambda v: jnp.pad(v.astype(F32), (0, LANES - DIFF_QK_DIM))
            lam_params = jnp.stack(
                [pad(lam_q1[l]), pad(lam_k1[l]), pad(lam_q2[l]), pad(lam_k2[l]),
                 jnp.full((LANES,), lam_init, F32)]
                + [jnp.zeros((LANES,), F32)] * 3)
            proj = in_proj(xb, ssq, g_mix, w_in, l, tabs)
            o_a = diff_attention(proj, lam_params, g_subln[l])
            o_b = moba_attention(proj)
            o_c = dilated_mixture(proj)
            z = gate_mix(xb, ssq, g_mix, o_a, o_b, o_c, w_gate, b_gate, w_branch, l)
            xr, xb, ssq = residual_matmul(z, w_out, l, xr, (2048, 1024, 512, 256), (256, 128))
            u = swiglu_up(xb, ssq, g_ffn, w_ffn_gate, w_ffn_up, l)
            xr, xb, ssq = residual_matmul(u, w_ffn_down, l, xr, (1024, 512, 256), (256, 128),
                                          down_chunks)
        outs.append(rmsnorm(xr, g_final, NORM_EPS, x.dtype))
    return outs[0].reshape(b, s, d) if b == 1 else jnp.stack(outs, axis=0)
```

```python
import functools
import math

import jax
import jax.numpy as jnp
from jax import lax
from jax.experimental import pallas as pl
from jax.experimental.pallas import tpu as pltpu

HEAD_DIM = 128
DIFF_HEADS = 4
DIFF_QK_DIM = HEAD_DIM // 2
MOBA_HEADS = 4
MOBA_BLOCK = 256
MOBA_TOP_K = 3
DIL_PATTERNS = ((128, 1), (512, 4), (2048, 16))
DIL_HEADS = 4
N_BRANCH = 3
BRANCH_WIDTH = 512
ROPE_THETA = 10000.0
NORM_EPS = 1e-6
SUBLN_EPS = 1e-5
NEG_INF = -1e30
LOG2_E = math.log2(math.e)

N_DIL = len(DIL_PATTERNS)
HEADS = 4
SLAB_AQ, SLAB_AK, SLAB_AV = 0, 4, 8
SLAB_BQ, SLAB_BK, SLAB_BV = 12, 16, 20
SLAB_CQ, SLAB_CK, SLAB_CV = 24, 36, 48
N_SLABS = 60
PROJ_TILE = 512

LANES = 128
BF16_SUBLANES = 16
VMEM_LIMIT_BYTES = 58 * 1024 * 1024

F32 = jnp.float32
BF16 = jnp.bfloat16


def _params(sem, vmem=VMEM_LIMIT_BYTES):
    return pltpu.CompilerParams(dimension_semantics=sem, vmem_limit_bytes=vmem)


def _tile(n, prefs):
    for p in prefs:
        if n % p == 0:
            return p
    return n


def _resident(block_shape, index_map):
    return pl.BlockSpec(block_shape, index_map, pipeline_mode=pl.Buffered(1))


def _lane_folded_sumsq(x):
    sq = x * x
    out = sq[:, 0:LANES]
    for c in range(1, x.shape[1] // LANES):
        out = out + sq[:, c * LANES:(c + 1) * LANES]
    return out


def _rstd(ssq_ref, d, eps):
    return lax.rsqrt(jnp.sum(ssq_ref[...], axis=1, keepdims=True) * (1.0 / d) + eps)


def _norm_prep_kernel(x_ref, g_ref, xg_ref, ssq_ref):
    x = x_ref[...]
    xg_ref[...] = (x * g_ref[...]).astype(BF16)
    ssq_ref[...] = _lane_folded_sumsq(x)


def norm_prep(x, g):
    s, d = x.shape
    tm = _tile(s, (256, 128, 8))
    return pl.pallas_call(
        _norm_prep_kernel,
        out_shape=(jax.ShapeDtypeStruct((s, d), BF16), jax.ShapeDtypeStruct((s, LANES), F32)),
        grid=(s // tm,),
        in_specs=[pl.BlockSpec((tm, d), lambda i: (i, 0)), pl.BlockSpec((1, d), lambda i: (0, 0))],
        out_specs=(pl.BlockSpec((tm, d), lambda i: (i, 0)), pl.BlockSpec((tm, LANES), lambda i: (i, 0))),
        compiler_params=_params(("parallel",)),
        name="norm_prep",
    )(x, g.reshape(1, d))


def _rmsnorm_kernel(x_ref, g_ref, o_ref, *, eps):
    x = x_ref[...]
    ms = jnp.mean(x * x, axis=-1, keepdims=True)
    o_ref[...] = (x * lax.rsqrt(ms + eps) * g_ref[...]).astype(o_ref.dtype)


def rmsnorm(x, g, eps, out_dtype):
    s, d = x.shape
    tm = _tile(s, (256, 128, 8))
    return pl.pallas_call(
        functools.partial(_rmsnorm_kernel, eps=eps),
        out_shape=jax.ShapeDtypeStruct((s, d), out_dtype),
        grid=(s // tm,),
        in_specs=[pl.BlockSpec((tm, d), lambda i: (i, 0)),
                  pl.BlockSpec((1, d), lambda i: (0, 0))],
        out_specs=pl.BlockSpec((tm, d), lambda i: (i, 0)),
        compiler_params=_params(("parallel",)),
        name="rmsnorm",
    )(x, g.reshape(1, d))


_ROPE64_Q, _ROPE64_K = (0,), (1,)
_ROPE128_Q, _ROPE128_K = (3, 6, 7, 8), (4, 9, 10, 11)


def _rot_half_128(x):
    return pltpu.roll(x, 64, 1)


def _rot_half_64(x):
    lane = lax.broadcasted_iota(jnp.int32, x.shape, 1)
    return jnp.where((lane & 32) == 0, pltpu.roll(x, 96, 1), pltpu.roll(x, 32, 1))


def _in_proj_kernel(xg_ref, ssq_ref, w_ref, c128_ref, s128_ref, c64_ref, s64_ref, o_ref):
    j = pl.program_id(1)
    d = xg_ref.shape[1]
    acc = jnp.dot(xg_ref[...], w_ref[...].astype(BF16), preferred_element_type=F32)
    acc = acc * _rstd(ssq_ref, d, NORM_EPS)

    def emit(fn):
        for c in range(PROJ_TILE // LANES):
            o_ref[c] = fn(acc[:, c * LANES:(c + 1) * LANES]).astype(o_ref.dtype)

    def is_in(tiles):
        cond = j == tiles[0]
        for t in tiles[1:]:
            cond = jnp.logical_or(cond, j == t)
        return cond

    def rope128(scale):
        def fn(xs):
            return (xs * c128_ref[...] + _rot_half_128(xs) * s128_ref[...]) * scale
        return fn

    def rope64(scale):
        def fn(xs):
            return (xs * c64_ref[...] + _rot_half_64(xs) * s64_ref[...]) * scale
        return fn

    roped = _ROPE64_Q + _ROPE64_K + _ROPE128_Q + _ROPE128_K
    pl.when(is_in(_ROPE64_Q))(lambda: emit(rope64(DIFF_QK_DIM ** -0.5 * LOG2_E)))
    pl.when(is_in(_ROPE64_K))(lambda: emit(rope64(1.0)))
    pl.when(is_in(_ROPE128_Q))(lambda: emit(rope128(HEAD_DIM ** -0.5 * LOG2_E)))
    pl.when(is_in(_ROPE128_K))(lambda: emit(rope128(1.0)))
    pl.when(jnp.logical_not(is_in(roped)))(lambda: emit(lambda xs: xs))


def in_proj(xg, ssq, w_all, layer, tabs):
    s, d = xg.shape
    n = w_all.shape[2]
    tm = _tile(s, (2048, 1024, 512, 256))
    tab_spec = _resident((tm, LANES), lambda i, j: (i, 0))
    return pl.pallas_call(
        _in_proj_kernel,
        out_shape=jax.ShapeDtypeStruct((n // LANES, s, LANES), BF16),
        grid=(s // tm, n // PROJ_TILE),
        in_specs=[_resident((tm, d), lambda i, j: (i, 0)),
                  _resident((tm, LANES), lambda i, j: (i, 0)),
                  pl.BlockSpec((None, d, PROJ_TILE), lambda i, j: (layer, 0, j)),
                  tab_spec, tab_spec, tab_spec, tab_spec],
        out_specs=pl.BlockSpec((PROJ_TILE // LANES, tm, LANES), lambda i, j: (j, i, 0)),
        compiler_params=_params(("parallel", "arbitrary")),
        name="in_proj",
    )(xg, ssq, w_all, *tabs)


def rope_tables(seq):
    def table(dim):
        inv = ROPE_THETA ** (-jnp.arange(0, dim, 2, dtype=F32) / dim)
        ang = jnp.arange(seq, dtype=F32)[:, None] * inv[None, :]
        cos, sin = jnp.cos(ang), jnp.sin(ang)
        reps = LANES // dim
        return (jnp.tile(jnp.concatenate([cos, cos], axis=1), (1, reps)),
                jnp.tile(jnp.concatenate([-sin, sin], axis=1), (1, reps)))
    c128, s128 = table(HEAD_DIM)
    c64, s64 = table(DIFF_QK_DIM)
    return c128, s128, c64, s64


VT_ROWS = HEAD_DIM + BF16_SUBLANES


def _transpose_bf16(x):
    return x.astype(F32).T.astype(BF16)


def _dot_nt(a, b):
    return lax.dot_general(a, b, (((1,), (1,)), ((), ())), preferred_element_type=F32)


def _fill_v_transposed(v_ref, vt_ref, chunk):
    seq = v_ref.shape[1]
    for c in range(seq // chunk):
        vt_ref[0:HEAD_DIM, c * chunk:(c + 1) * chunk] = _transpose_bf16(
            v_ref[0, c * chunk:(c + 1) * chunk, :])
    vt_ref[HEAD_DIM:VT_ROWS, :] = jnp.ones((BF16_SUBLANES, seq), BF16)


def _softmax_step(st, vt, carry):
    m_old, acc = carry
    m_new = jnp.maximum(m_old, jnp.max(st, axis=0, keepdims=True))
    alpha = jnp.exp2(m_old - m_new)
    p = jnp.exp2(st - m_new).astype(BF16)
    return m_new, alpha * acc + jnp.dot(vt, p, preferred_element_type=F32)


def _init_carry(nq):
    return jnp.full((1, nq), NEG_INF, F32), jnp.zeros((VT_ROWS, nq), F32)


def _normalized(carry):
    acc = carry[1]
    return acc[0:HEAD_DIM, :] / acc[HEAD_DIM:HEAD_DIM + 1, :]


def _paired_loop(n, tile, carry):
    def pair(t, c):
        a = tile(2 * t)
        b = tile(2 * t + 1)
        return _softmax_step(*b, _softmax_step(*a, c))
    carry = lax.fori_loop(0, n // 2, pair, carry)
    return lax.cond(n % 2 == 1, lambda c: _softmax_step(*tile(n - 1), c), lambda c: c, carry)


def _diff_attn_kernel(q_ref, k_ref, v_ref, lam_ref, gsub_ref, o_ref, vt_ref, qs_ref, *, tq):
    i = pl.program_id(1)

    @pl.when(i == 0)
    def _():
        _fill_v_transposed(v_ref, vt_ref, tq)

    q = q_ref[0]
    lane = lax.broadcasted_iota(jnp.int32, q.shape, 1)
    zero = jnp.zeros_like(q)
    qs_ref[0:tq, :] = jnp.where(lane < DIFF_QK_DIM, q, zero)
    qs_ref[tq:2 * tq, :] = jnp.where(lane >= DIFF_QK_DIM, q, zero)

    def tile(j):
        start = pl.multiple_of(j * tq, tq)
        return _dot_nt(k_ref[0, pl.ds(start, tq), :], qs_ref[...]), vt_ref[:, pl.ds(start, tq)]

    carry = _paired_loop(i, tile, _init_carry(2 * tq))

    st, vt = tile(i)
    krow = lax.broadcasted_iota(jnp.int32, st.shape, 0)
    qcol = lax.broadcasted_iota(jnp.int32, st.shape, 1)
    qcol = jnp.where(qcol >= tq, qcol - tq, qcol)
    carry = _softmax_step(jnp.where(krow <= qcol, st, NEG_INF), vt, carry)

    lp = lam_ref[...]
    lam_init = lp[4:5, 0:1]
    lam = (jnp.exp(jnp.sum(lp[0:1] * lp[1:2], axis=1, keepdims=True))
           - jnp.exp(jnp.sum(lp[2:3] * lp[3:4], axis=1, keepdims=True)) + lam_init)
    o = _normalized(carry)
    od = o[:, 0:tq] - lam * o[:, tq:2 * tq]
    ms = jnp.mean(od * od, axis=0, keepdims=True)
    y = od * lax.rsqrt(ms + SUBLN_EPS) * gsub_ref[...] * (1.0 - lam_init)
    o_ref[...] = y.T.astype(o_ref.dtype)


def diff_attention(proj, lam_params, g_sub):
    _, s, _ = proj.shape
    tq = _tile(s, (512, 256, 128))
    return pl.pallas_call(
        functools.partial(_diff_attn_kernel, tq=tq),
        out_shape=jax.ShapeDtypeStruct((s, BRANCH_WIDTH), BF16),
        grid=(HEADS, s // tq),
        in_specs=[pl.BlockSpec((1, tq, LANES), lambda h, i: (SLAB_AQ + h, i, 0)),
                  pl.BlockSpec((1, s, LANES), lambda h, i: (SLAB_AK + h, 0, 0)),
                  pl.BlockSpec((1, s, LANES), lambda h, i: (SLAB_AV + h, 0, 0)),
                  pl.BlockSpec((8, LANES), lambda h, i: (0, 0)),
                  pl.BlockSpec((HEAD_DIM, 1), lambda h, i: (0, 0))],
        out_specs=pl.BlockSpec((tq, LANES), lambda h, i: (i, h)),
        scratch_shapes=[pltpu.VMEM((VT_ROWS, s), BF16),
                        pltpu.VMEM((2 * tq, LANES), BF16)],
        compiler_params=_params(("arbitrary", "arbitrary")),
        name="diff_attn",
    )(proj, proj, proj, lam_params, g_sub.reshape(HEAD_DIM, 1))


def _moba_attn_kernel(q_ref, k_ref, v_ref, o_ref, vt_ref, kmean_ref, bias_ref, *, tq, tk, nb):
    i = pl.program_id(1)
    blk = MOBA_BLOCK
    blk_shift = blk.bit_length() - 1
    blk_per_tile = tk // blk
    own_tiles = tq // tk

    @pl.when(i == 0)
    def _():
        _fill_v_transposed(v_ref, vt_ref, tk)
        for n in range(nb):
            kmean_ref[n:n + 1, :] = jnp.mean(
                k_ref[0, n * blk:(n + 1) * blk, :].astype(F32), axis=0, keepdims=True)

    q = q_ref[0]
    km = kmean_ref[...]
    km_hi = km.astype(BF16)
    km_lo = (km - km_hi.astype(F32)).astype(BF16)
    gate = _dot_nt(km_hi, q) + _dot_nt(km_lo, q)
    bidx = lax.broadcasted_iota(jnp.int32, gate.shape, 0)
    qpos = i * tq + lax.broadcasted_iota(jnp.int32, gate.shape, 1)
    own = jnp.right_shift(qpos, blk_shift)
    past_blk = bidx < own
    gate = jnp.where(past_blk, gate, NEG_INF)
    rank = jnp.zeros(gate.shape, jnp.int32)
    for mth in range(nb):
        gm = gate[mth:mth + 1, :]
        beats = jnp.logical_or(gm > gate, jnp.logical_and(gm == gate, bidx > mth))
        rank = rank + beats.astype(jnp.int32)
    selected = jnp.logical_and(rank < MOBA_TOP_K, past_blk)
    bias_ref[...] = jnp.where(selected, 0.0, NEG_INF).astype(F32)

    def scores(j):
        start = pl.multiple_of(j * tk, tk)
        st = _dot_nt(k_ref[0, pl.ds(start, tk), :], q)
        return st, vt_ref[:, pl.ds(start, tk)]

    def past(j):
        st, vt = scores(j)
        st = jnp.concatenate(
            [st[b * blk:(b + 1) * blk, :] + bias_ref[pl.ds(j * blk_per_tile + b, 1), :]
             for b in range(blk_per_tile)], axis=0)
        return st, vt

    first_own_tile = i * own_tiles
    carry = _paired_loop(first_own_tile, past, _init_carry(tq))

    own_row = jnp.right_shift(i * tq + lax.broadcasted_iota(jnp.int32, (1, tq), 1), blk_shift)
    own = [scores(first_own_tile + t) for t in range(own_tiles)]
    for t in range(own_tiles):
        j = first_own_tile + t
        st, vt = own[t]
        kpos = j * tk + lax.broadcasted_iota(jnp.int32, st.shape, 0)
        qp = i * tq + lax.broadcasted_iota(jnp.int32, st.shape, 1)
        st = jnp.where(kpos <= qp, st, NEG_INF)
        parts = []
        for b in range(blk_per_tile):
            n = j * blk_per_tile + b
            row_bias = jnp.where(own_row > n, bias_ref[pl.ds(n, 1), :],
                                 jnp.where(own_row == n, 0.0, NEG_INF))
            parts.append(st[b * blk:(b + 1) * blk, :] + row_bias)
        carry = _softmax_step(jnp.concatenate(parts, axis=0), vt, carry)

    o_ref[...] = _normalized(carry).T.astype(o_ref.dtype)


def moba_attention(proj):
    _, s, _ = proj.shape
    assert s % MOBA_BLOCK == 0 and MOBA_BLOCK & (MOBA_BLOCK - 1) == 0
    nb = s // MOBA_BLOCK
    tq = _tile(s, (1024, 512, 256))
    tk = _tile(tq, (512, 256))
    return pl.pallas_call(
        functools.partial(_moba_attn_kernel, tq=tq, tk=tk, nb=nb),
        out_shape=jax.ShapeDtypeStruct((s, BRANCH_WIDTH), BF16),
        grid=(HEADS, s // tq),
        in_specs=[pl.BlockSpec((1, tq, LANES), lambda h, i: (SLAB_BQ + h, i, 0)),
                  pl.BlockSpec((1, s, LANES), lambda h, i: (SLAB_BK + h, 0, 0)),
                  pl.BlockSpec((1, s, LANES), lambda h, i: (SLAB_BV + h, 0, 0))],
        out_specs=pl.BlockSpec((tq, LANES), lambda h, i: (i, h)),
        scratch_shapes=[pltpu.VMEM((VT_ROWS, s), BF16),
                        pltpu.VMEM((nb, LANES), F32),
                        pltpu.VMEM((nb, tq), F32)],
        compiler_params=_params(("arbitrary", "arbitrary")),
        name="moba_attn",
    )(proj, proj, proj)


def _dilated_kernel(q_ref, k_ref, v_ref, kp_ref, vp_ref, o_ref, lse_ref,
                    qf_ref, kf_ref, vf_ref, kpf_ref, vpf_ref, *, tq, band, dilation):
    i = pl.program_id(1)
    qf_ref[...] = q_ref[0].astype(F32)
    kf_ref[...] = k_ref[0].astype(F32)
    vf_ref[...] = v_ref[0].astype(F32)
    kpf_ref[...] = kp_ref[0].astype(F32)
    vpf_ref[...] = vp_ref[0].astype(F32)

    def one_class(r, carry):
        cls = lambda ref, n: ref[pl.ds(r, n, stride=dilation), :].astype(BF16)
        q = cls(qf_ref, tq)
        s_c = _dot_nt(q, cls(kf_ref, tq))
        s_p = _dot_nt(q, cls(kpf_ref, band))
        rel = (lax.broadcasted_iota(jnp.int32, s_c.shape, 0)
               - lax.broadcasted_iota(jnp.int32, s_c.shape, 1))
        s_c = jnp.where(jnp.logical_and(rel >= 0, rel <= band), s_c, NEG_INF)
        rel_p = (lax.broadcasted_iota(jnp.int32, s_p.shape, 0)
                 - lax.broadcasted_iota(jnp.int32, s_p.shape, 1) + band)
        ok_p = jnp.logical_and(rel_p <= band, i > 0)
        s_p = jnp.where(ok_p, s_p, NEG_INF)
        m = jnp.maximum(jnp.max(s_c, axis=1, keepdims=True), jnp.max(s_p, axis=1, keepdims=True))
        e_c = jnp.exp2(s_c - m)
        e_p = jnp.exp2(s_p - m)
        den = jnp.sum(e_c, axis=1, keepdims=True) + jnp.sum(e_p, axis=1, keepdims=True)
        o = (jnp.dot(e_c.astype(BF16), cls(vf_ref, tq), preferred_element_type=F32)
             + jnp.dot(e_p.astype(BF16), cls(vpf_ref, band), preferred_element_type=F32))
        o_ref[0, pl.ds(r, tq, stride=dilation), :] = o / den
        lse2 = m + jnp.log2(den)
        lse_ref[0, pl.ds(r, tq, stride=dilation), :] = jnp.broadcast_to(lse2, (tq, LANES))
        return carry

    lax.fori_loop(0, dilation, one_class, 0)


def dilated_group(proj, g, window, dilation):
    _, s, _ = proj.shape
    band = window // dilation
    assert band == LANES and s % (band * dilation) == 0
    length = s // dilation
    tq = _tile(length, (512, 256, 128))
    per = tq // band
    rows, prev_rows = tq * dilation, band * dilation
    sq, sk, sv = SLAB_CQ + g * HEADS, SLAB_CK + g * HEADS, SLAB_CV + g * HEADS
    cur = lambda base: pl.BlockSpec((1, rows, LANES), lambda h, i: (base + h, i, 0))
    prev = lambda base: pl.BlockSpec(
        (1, prev_rows, LANES), lambda h, i: (base + h, jnp.maximum(i * per - 1, 0), 0))
    out_spec = pl.BlockSpec((1, rows, LANES), lambda h, i: (h, i, 0))
    return pl.pallas_call(
        functools.partial(_dilated_kernel, tq=tq, band=band, dilation=dilation),
        out_shape=(jax.ShapeDtypeStruct((HEADS, s, LANES), F32),
                   jax.ShapeDtypeStruct((HEADS, s, LANES), F32)),
        grid=(HEADS, length // tq),
        in_specs=[cur(sq), cur(sk), cur(sv), prev(sk), prev(sv)],
        out_specs=(out_spec, out_spec),
        scratch_shapes=[pltpu.VMEM((rows, LANES), F32)] * 3 + [pltpu.VMEM((prev_rows, LANES), F32)] * 2,
        compiler_params=_params(("parallel", "parallel")),
        name=f"dilated_d{dilation}",
    )(proj, proj, proj, proj, proj)


def _dil_merge_kernel(*refs):
    o_refs, lse_refs, out_ref = refs[:N_DIL], refs[N_DIL:2 * N_DIL], refs[2 * N_DIL]
    lses = [r[0] for r in lse_refs]
    m = functools.reduce(jnp.maximum, lses)
    es = [jnp.exp2(x - m) for x in lses]
    inv = 1.0 / functools.reduce(lambda a, b: a + b, es)
    acc = None
    for e, o_ref in zip(es, o_refs):
        term = (e * inv) * o_ref[0]
        acc = term if acc is None else acc + term
    out_ref[...] = acc.astype(out_ref.dtype)


def dilated_mixture(proj):
    _, s, _ = proj.shape
    outs, lses = [], []
    for g, (window, dilation) in enumerate(DIL_PATTERNS):
        o, lse = dilated_group(proj, g, window, dilation)
        outs.append(o)
        lses.append(lse)
    tm = _tile(s, (1024, 512, 256))
    spec = pl.BlockSpec((1, tm, LANES), lambda h, i: (h, i, 0))
    return pl.pallas_call(
        _dil_merge_kernel,
        out_shape=jax.ShapeDtypeStruct((s, BRANCH_WIDTH), BF16),
        grid=(HEADS, s // tm),
        in_specs=[spec] * (2 * N_DIL),
        out_specs=pl.BlockSpec((tm, LANES), lambda h, i: (i, h)),
        compiler_params=_params(("parallel", "parallel")),
        name="dilated_merge",
    )(*outs, *lses)


def _gate_mix_kernel(xg_ref, ssq_ref, oa_ref, ob_ref, oc_ref, wg0_ref, wg1_ref, wg2_ref,
                     b0_ref, b1_ref, b2_ref, wb_ref, z_ref):
    xg = xg_ref[...]
    rstd = _rstd(ssq_ref, xg.shape[1], NORM_EPS)
    z = None
    for n, (o_ref, wg_ref, b_ref) in enumerate(((oa_ref, wg0_ref, b0_ref),
                                                (ob_ref, wg1_ref, b1_ref),
                                                (oc_ref, wg2_ref, b2_ref))):
        pre = jnp.dot(xg, wg_ref[...].astype(BF16), preferred_element_type=F32)
        gate = jax.nn.sigmoid(pre * rstd + b_ref[...])
        y = jnp.dot(o_ref[...], wb_ref[n].astype(BF16), preferred_element_type=F32)
        z = gate * y if z is None else z + gate * y
    z_ref[...] = z.astype(z_ref.dtype)


def gate_mix(xg, ssq, o_a, o_b, o_c, w_gate_all, b_gate_all, w_branch_all, layer):
    s, d = xg.shape
    assert N_BRANCH == 3
    tm = _tile(s, (1024, 512, 256))
    tn = _tile(d, (256, 128))
    nj = d // tn
    o_spec = _resident((tm, BRANCH_WIDTH), lambda i, j: (i, 0))
    wg_spec = lambda n: pl.BlockSpec((None, d, tn), lambda i, j: (layer, 0, n * nj + j))
    b_spec = lambda n: pl.BlockSpec((None, 1, tn), lambda i, j: (layer, 0, n * nj + j))
    b3d = b_gate_all.reshape(b_gate_all.shape[0], 1, N_BRANCH * d)
    return pl.pallas_call(
        _gate_mix_kernel,
        out_shape=jax.ShapeDtypeStruct((s, d), BF16),
        grid=(s // tm, nj),
        in_specs=[_resident((tm, d), lambda i, j: (i, 0)),
                  _resident((tm, LANES), lambda i, j: (i, 0)),
                  o_spec, o_spec, o_spec,
                  wg_spec(0), wg_spec(1), wg_spec(2), b_spec(0), b_spec(1), b_spec(2),
                  pl.BlockSpec((None, N_BRANCH, BRANCH_WIDTH, tn), lambda i, j: (layer, 0, 0, j))],
        out_specs=pl.BlockSpec((tm, tn), lambda i, j: (i, j)),
        compiler_params=_params(("parallel", "arbitrary")),
        name="gate_mix",
    )(xg, ssq, o_a, o_b, o_c, w_gate_all, w_gate_all, w_gate_all,
      b3d, b3d, b3d, w_branch_all)


def _residual_matmul_kernel(a_ref, w_ref, x_ref, g_ref, o_ref, xg_ref, ssq_ref, *, k_chunks):
    j = pl.program_id(1)
    kc = a_ref.shape[1] // k_chunks
    acc = x_ref[...]
    for c in range(k_chunks):
        acc = acc + jnp.dot(a_ref[:, c * kc:(c + 1) * kc],
                            w_ref[c * kc:(c + 1) * kc, :].astype(BF16),
                            preferred_element_type=F32)
    o_ref[...] = acc
    xg_ref[...] = (acc * g_ref[...]).astype(BF16)
    part = _lane_folded_sumsq(acc)

    @pl.when(j == 0)
    def _():
        ssq_ref[...] = part

    @pl.when(j > 0)
    def _():
        ssq_ref[...] += part


def residual_matmul(a, w_all, layer, x, g_next, tm_prefs, tn_prefs, k_chunks=1):
    s, k = a.shape
    d = w_all.shape[2]
    tm = _tile(s, tm_prefs)
    tn = _tile(d, tn_prefs)
    assert k % (k_chunks * LANES) == 0
    return pl.pallas_call(
        functools.partial(_residual_matmul_kernel, k_chunks=k_chunks),
        out_shape=(jax.ShapeDtypeStruct((s, d), F32), jax.ShapeDtypeStruct((s, d), BF16),
                   jax.ShapeDtypeStruct((s, LANES), F32)),
        grid=(s // tm, d // tn),
        in_specs=[_resident((tm, k), lambda i, j: (i, 0)),
                  pl.BlockSpec((None, k, tn), lambda i, j: (layer, 0, j)),
                  pl.BlockSpec((tm, tn), lambda i, j: (i, j)),
                  pl.BlockSpec((1, tn), lambda i, j: (0, j))],
        out_specs=(pl.BlockSpec((tm, tn), lambda i, j: (i, j)),
                   pl.BlockSpec((tm, tn), lambda i, j: (i, j)),
                   pl.BlockSpec((tm, LANES), lambda i, j: (i, 0))),
        input_output_aliases={2: 0},
        compiler_params=_params(("parallel", "arbitrary")),
        name="residual_matmul",
    )(a, w_all, x, g_next.reshape(1, d))


def _swiglu_kernel(xg_ref, ssq_ref, wg_ref, wu_ref, o_ref):
    xg = xg_ref[...]
    rstd = _rstd(ssq_ref, xg.shape[1], NORM_EPS)
    g = jnp.dot(xg, wg_ref[...].astype(BF16), preferred_element_type=F32) * rstd
    u = jnp.dot(xg, wu_ref[...].astype(BF16), preferred_element_type=F32) * rstd
    o_ref[...] = (g * jax.nn.sigmoid(g) * u).astype(o_ref.dtype)


def swiglu_up(xg, ssq, w_g_all, w_u_all, layer):
    s, d = xg.shape
    f = w_g_all.shape[2]
    tm = _tile(s, (2048, 1024, 512, 256))
    tn = _tile(f, (256, 128))
    w_spec = pl.BlockSpec((None, d, tn), lambda i, j: (layer, 0, j))
    return pl.pallas_call(
        _swiglu_kernel,
        out_shape=jax.ShapeDtypeStruct((s, f), BF16),
        grid=(s // tm, f // tn),
        in_specs=[_resident((tm, d), lambda i, j: (i, 0)),
                  _resident((tm, LANES), lambda i, j: (i, 0)),
                  w_spec, w_spec],
        out_specs=pl.BlockSpec((tm, tn), lambda i, j: (i, j)),
        compiler_params=_params(("parallel", "arbitrary")),
        name="swiglu_up",
    )(xg, ssq, w_g_all, w_u_all)


def kernel(x, g_mix, w_in, lam_q1, lam_k1, lam_q2, lam_k2, g_subln, w_gate, b_gate, w_branch,
           w_out, g_ffn, w_ffn_gate, w_ffn_up, w_ffn_down, g_final):
    b, s, d = x.shape
    depth = w_in.shape[0]
    tabs = rope_tables(s)
    down_chunks = 2 if w_ffn_down.shape[1] % (2 * LANES) == 0 else 1
    outs = []
    for bi in range(b):
        xr = x.reshape(s, d) if b == 1 else x[bi]
        xg, ssq = norm_prep(xr, g_mix[0])
        for l in range(depth):
            lam_init = 0.8 - 0.6 * math.exp(-0.3 * l)
            pad = lambda v: jnp.pad(v.astype(F32), (0, LANES - DIFF_QK_DIM))
            lam_params = jnp.stack(
                [pad(lam_q1[l]), pad(lam_k1[l]), pad(lam_q2[l]), pad(lam_k2[l]),
                 jnp.full((LANES,), lam_init, F32)]
                + [jnp.zeros((LANES,), F32)] * 3)
            proj = in_proj(xg, ssq, w_in, l, tabs)
            o_a = diff_attention(proj, lam_params, g_subln[l])
            o_b = moba_attention(proj)
            o_c = dilated_mixture(proj)
            z = gate_mix(xg, ssq, o_a, o_b, o_c, w_gate, b_gate, w_branch, l)
            xr, xg, ssq = residual_matmul(z, w_out, l, xr, g_ffn[l], (2048, 1024, 512, 256), (256, 128))
            u = swiglu_up(xg, ssq, w_ffn_gate, w_ffn_up, l)
            g_next = g_mix[l + 1] if l + 1 < depth else g_final
            xr, xg, ssq = residual_matmul(u, w_ffn_down, l, xr, g_next, (1024, 512, 256), (256, 128),
                                          down_chunks)
        outs.append(rmsnorm(xr, g_final, NORM_EPS, x.dtype))
    return outs[0].reshape(b, s, d) if b == 1 else jnp.stack(outs, axis=0)
```

```python
import functools
import math

import jax
import jax.numpy as jnp
from jax import lax
from jax.experimental import pallas as pl
from jax.experimental.pallas import tpu as pltpu

HEAD_DIM = 128
DIFF_HEADS = 4
DIFF_QK_DIM = HEAD_DIM // 2
MOBA_HEADS = 4
MOBA_BLOCK = 256
MOBA_TOP_K = 3
DIL_PATTERNS = ((128, 1), (512, 4), (2048, 16))
DIL_HEADS = 4
N_BRANCH = 3
BRANCH_WIDTH = 512
ROPE_THETA = 10000.0
NORM_EPS = 1e-6
SUBLN_EPS = 1e-5
NEG_INF = -1e30
LOG2_E = math.log2(math.e)

N_DIL = len(DIL_PATTERNS)
HEADS = 4
SLAB_AQ, SLAB_AK, SLAB_AV = 0, 4, 8
SLAB_BQ, SLAB_BK, SLAB_BV = 12, 16, 20
SLAB_CQ, SLAB_CK, SLAB_CV = 24, 36, 48
N_SLABS = 60
PROJ_TILE = 512

LANES = 128
BF16_SUBLANES = 16
VMEM_LIMIT_BYTES = 58 * 1024 * 1024

F32 = jnp.float32
BF16 = jnp.bfloat16


def _params(sem, vmem=VMEM_LIMIT_BYTES):
    return pltpu.CompilerParams(dimension_semantics=sem, vmem_limit_bytes=vmem)


def _tile(n, prefs):
    for p in prefs:
        if n % p == 0:
            return p
    return n


def _resident(block_shape, index_map):
    return pl.BlockSpec(block_shape, index_map, pipeline_mode=pl.Buffered(1))


def _lane_folded_sumsq(x):
    sq = x * x
    out = sq[:, 0:LANES]
    for c in range(1, x.shape[1] // LANES):
        out = out + sq[:, c * LANES:(c + 1) * LANES]
    return out


def _rstd(ssq, d, eps):
    return lax.rsqrt(jnp.sum(ssq, axis=1, keepdims=True) * (1.0 / d) + eps)


def _norm_prep_kernel(x_ref, g_ref, xg_ref, ssq_ref):
    x = x_ref[...]
    xg_ref[...] = (x * g_ref[...]).astype(BF16)
    ssq_ref[...] = _lane_folded_sumsq(x)


def norm_prep(x, g):
    s, d = x.shape
    tm = _tile(s, (256, 128, 8))
    return pl.pallas_call(
        _norm_prep_kernel,
        out_shape=(jax.ShapeDtypeStruct((s, d), BF16), jax.ShapeDtypeStruct((s, LANES), F32)),
        grid=(s // tm,),
        in_specs=[pl.BlockSpec((tm, d), lambda i: (i, 0)), pl.BlockSpec((1, d), lambda i: (0, 0))],
        out_specs=(pl.BlockSpec((tm, d), lambda i: (i, 0)), pl.BlockSpec((tm, LANES), lambda i: (i, 0))),
        compiler_params=_params(("parallel",)),
        name="norm_prep",
    )(x, g.reshape(1, d))


def _rmsnorm_kernel(x_ref, g_ref, o_ref, *, eps):
    x = x_ref[...]
    ms = jnp.mean(x * x, axis=-1, keepdims=True)
    o_ref[...] = (x * lax.rsqrt(ms + eps) * g_ref[...]).astype(o_ref.dtype)


def rmsnorm(x, g, eps, out_dtype):
    s, d = x.shape
    tm = _tile(s, (256, 128, 8))
    return pl.pallas_call(
        functools.partial(_rmsnorm_kernel, eps=eps),
        out_shape=jax.ShapeDtypeStruct((s, d), out_dtype),
        grid=(s // tm,),
        in_specs=[pl.BlockSpec((tm, d), lambda i: (i, 0)),
                  pl.BlockSpec((1, d), lambda i: (0, 0))],
        out_specs=pl.BlockSpec((tm, d), lambda i: (i, 0)),
        compiler_params=_params(("parallel",)),
        name="rmsnorm",
    )(x, g.reshape(1, d))


KIND_PLAIN, KIND_ROPE128_Q, KIND_ROPE128_K, KIND_ROPE64_Q, KIND_ROPE64_K = range(5)
_TILE_KINDS = (KIND_ROPE64_Q, KIND_ROPE64_K, KIND_PLAIN,
               KIND_ROPE128_Q, KIND_ROPE128_K, KIND_PLAIN,
               KIND_ROPE128_Q, KIND_ROPE128_Q, KIND_ROPE128_Q,
               KIND_ROPE128_K, KIND_ROPE128_K, KIND_ROPE128_K,
               KIND_PLAIN, KIND_PLAIN, KIND_PLAIN)
IN_PROJ_ROW_PARTS = 8


def _tile_kind(j):
    kind = jnp.int32(_TILE_KINDS[-1])
    for t in range(len(_TILE_KINDS) - 2, -1, -1):
        kind = jnp.where(j == t, _TILE_KINDS[t], kind)
    return kind


def _in_proj_kernel(xg_ref, ssq_ref, w_ref, ta_ref, tb_ref, tc_ref, td_ref, o_ref):
    d = xg_ref.shape[1]
    w = w_ref[...].astype(BF16)
    part = xg_ref.shape[0] // IN_PROJ_ROW_PARTS
    for r in range(IN_PROJ_ROW_PARTS):
        rows = slice(r * part, (r + 1) * part)
        acc = jnp.dot(xg_ref[rows, :], w, preferred_element_type=F32)
        acc = acc * _rstd(ssq_ref[rows, :], d, NORM_EPS)
        ta, tb, tc, td = ta_ref[rows, :], tb_ref[rows, :], tc_ref[rows, :], td_ref[rows, :]
        for c in range(PROJ_TILE // LANES):
            xs = acc[:, c * LANES:(c + 1) * LANES]
            y = (xs * ta + pltpu.roll(xs, 64, 1) * tb
                 + pltpu.roll(xs, 32, 1) * tc + pltpu.roll(xs, 96, 1) * td)
            o_ref[c, rows, :] = y.astype(o_ref.dtype)


def in_proj(xg, ssq, w_all, layer, tabs):
    s, d = xg.shape
    n = w_all.shape[2]
    assert n // PROJ_TILE == len(_TILE_KINDS)
    tm = _tile(s, (2048, 1024, 512, 256))
    tab_spec = pl.BlockSpec((None, tm, LANES), lambda i, j: (_tile_kind(j), i, 0))
    return pl.pallas_call(
        _in_proj_kernel,
        out_shape=jax.ShapeDtypeStruct((n // LANES, s, LANES), BF16),
        grid=(s // tm, n // PROJ_TILE),
        in_specs=[_resident((tm, d), lambda i, j: (i, 0)),
                  _resident((tm, LANES), lambda i, j: (i, 0)),
                  pl.BlockSpec((None, d, PROJ_TILE), lambda i, j: (layer, 0, j)),
                  tab_spec, tab_spec, tab_spec, tab_spec],
        out_specs=pl.BlockSpec((PROJ_TILE // LANES, tm, LANES), lambda i, j: (j, i, 0)),
        compiler_params=_params(("parallel", "arbitrary")),
        name="in_proj",
    )(xg, ssq, w_all, *tabs)


def rope_tables(seq):
    def angles(dim):
        inv = ROPE_THETA ** (-jnp.arange(0, dim, 2, dtype=F32) / dim)
        ang = jnp.arange(seq, dtype=F32)[:, None] * inv[None, :]
        reps = LANES // dim
        cos = jnp.tile(jnp.concatenate([jnp.cos(ang)] * 2, axis=1), (1, reps))
        sin = jnp.tile(jnp.concatenate([-jnp.sin(ang), jnp.sin(ang)], axis=1), (1, reps))
        return cos, sin
    c128, s128 = angles(HEAD_DIM)
    c64, s64 = angles(DIFF_QK_DIM)
    zero, one = jnp.zeros((seq, LANES), F32), jnp.ones((seq, LANES), F32)
    low = (jnp.arange(LANES) & (DIFF_QK_DIM // 2)) == 0
    s64_up, s64_down = jnp.where(low, s64, 0.0), jnp.where(low, 0.0, s64)
    q128, q64 = HEAD_DIM ** -0.5 * LOG2_E, DIFF_QK_DIM ** -0.5 * LOG2_E
    kinds = {KIND_PLAIN: (one, zero, zero, zero),
             KIND_ROPE128_Q: (c128 * q128, s128 * q128, zero, zero),
             KIND_ROPE128_K: (c128, s128, zero, zero),
             KIND_ROPE64_Q: (c64 * q64, zero, s64_down * q64, s64_up * q64),
             KIND_ROPE64_K: (c64, zero, s64_down, s64_up)}
    return tuple(jnp.stack([kinds[k][t] for k in range(len(kinds))]) for t in range(4))


VT_ROWS = HEAD_DIM + BF16_SUBLANES


def _transpose_bf16(x):
    return x.astype(F32).T.astype(BF16)


def _dot_nt(a, b):
    return lax.dot_general(a, b, (((1,), (1,)), ((), ())), preferred_element_type=F32)


def _fill_v_transposed(v_ref, vt_ref, chunk):
    seq = v_ref.shape[1]
    for c in range(seq // chunk):
        vt_ref[0:HEAD_DIM, c * chunk:(c + 1) * chunk] = _transpose_bf16(
            v_ref[0, c * chunk:(c + 1) * chunk, :])
    vt_ref[HEAD_DIM:VT_ROWS, :] = jnp.ones((BF16_SUBLANES, seq), BF16)


def _softmax_step(st, vt, carry):
    m_old, acc = carry
    m_new = jnp.maximum(m_old, jnp.max(st, axis=0, keepdims=True))
    alpha = jnp.exp2(m_old - m_new)
    p = jnp.exp2(st - m_new).astype(BF16)
    return m_new, alpha * acc + jnp.dot(vt, p, preferred_element_type=F32)


def _init_carry(nq):
    return jnp.full((1, nq), NEG_INF, F32), jnp.zeros((VT_ROWS, nq), F32)


def _normalized(carry):
    acc = carry[1]
    return acc[0:HEAD_DIM, :] / acc[HEAD_DIM:HEAD_DIM + 1, :]


def _paired_loop(n, tile, carry):
    def pair(t, c):
        a = tile(2 * t)
        b = tile(2 * t + 1)
        return _softmax_step(*b, _softmax_step(*a, c))
    carry = lax.fori_loop(0, n // 2, pair, carry)
    return lax.cond(n % 2 == 1, lambda c: _softmax_step(*tile(n - 1), c), lambda c: c, carry)


def _diff_attn_kernel(q_ref, k_ref, v_ref, lam_ref, gsub_ref, o_ref, vt_ref, qs_ref, *, tq):
    i = pl.program_id(1)

    @pl.when(i == 0)
    def _():
        _fill_v_transposed(v_ref, vt_ref, tq)

    q = q_ref[0]
    lane = lax.broadcasted_iota(jnp.int32, q.shape, 1)
    zero = jnp.zeros_like(q)
    qs_ref[0:tq, :] = jnp.where(lane < DIFF_QK_DIM, q, zero)
    qs_ref[tq:2 * tq, :] = jnp.where(lane >= DIFF_QK_DIM, q, zero)

    def tile(j):
        start = pl.multiple_of(j * tq, tq)
        return _dot_nt(k_ref[0, pl.ds(start, tq), :], qs_ref[...]), vt_ref[:, pl.ds(start, tq)]

    carry = _paired_loop(i, tile, _init_carry(2 * tq))

    st, vt = tile(i)
    krow = lax.broadcasted_iota(jnp.int32, st.shape, 0)
    qcol = lax.broadcasted_iota(jnp.int32, st.shape, 1)
    qcol = jnp.where(qcol >= tq, qcol - tq, qcol)
    carry = _softmax_step(jnp.where(krow <= qcol, st, NEG_INF), vt, carry)

    lp = lam_ref[...]
    lam_init = lp[4:5, 0:1]
    lam = (jnp.exp(jnp.sum(lp[0:1] * lp[1:2], axis=1, keepdims=True))
           - jnp.exp(jnp.sum(lp[2:3] * lp[3:4], axis=1, keepdims=True)) + lam_init)
    o = _normalized(carry)
    od = o[:, 0:tq] - lam * o[:, tq:2 * tq]
    ms = jnp.mean(od * od, axis=0, keepdims=True)
    y = od * lax.rsqrt(ms + SUBLN_EPS) * gsub_ref[...] * (1.0 - lam_init)
    o_ref[...] = y.T.astype(o_ref.dtype)


def diff_attention(proj, lam_params, g_sub):
    _, s, _ = proj.shape
    tq = _tile(s, (512, 256, 128))
    return pl.pallas_call(
        functools.partial(_diff_attn_kernel, tq=tq),
        out_shape=jax.ShapeDtypeStruct((s, BRANCH_WIDTH), BF16),
        grid=(HEADS, s // tq),
        in_specs=[pl.BlockSpec((1, tq, LANES), lambda h, i: (SLAB_AQ + h, i, 0)),
                  pl.BlockSpec((1, s, LANES), lambda h, i: (SLAB_AK + h, 0, 0)),
                  pl.BlockSpec((1, s, LANES), lambda h, i: (SLAB_AV + h, 0, 0)),
                  pl.BlockSpec((8, LANES), lambda h, i: (0, 0)),
                  pl.BlockSpec((HEAD_DIM, 1), lambda h, i: (0, 0))],
        out_specs=pl.BlockSpec((tq, LANES), lambda h, i: (i, h)),
        scratch_shapes=[pltpu.VMEM((VT_ROWS, s), BF16),
                        pltpu.VMEM((2 * tq, LANES), BF16)],
        compiler_params=_params(("arbitrary", "arbitrary")),
        name="diff_attn",
    )(proj, proj, proj, lam_params, g_sub.reshape(HEAD_DIM, 1))


def _moba_attn_kernel(q_ref, k_ref, v_ref, o_ref, vt_ref, kmean_ref, bias_ref, *, tq, tk, nb):
    i = pl.program_id(1)
    blk = MOBA_BLOCK
    blk_shift = blk.bit_length() - 1
    blk_per_tile = tk // blk
    own_tiles = tq // tk

    @pl.when(i == 0)
    def _():
        _fill_v_transposed(v_ref, vt_ref, tk)
        for n in range(nb):
            kmean_ref[n:n + 1, :] = jnp.mean(
                k_ref[0, n * blk:(n + 1) * blk, :].astype(F32), axis=0, keepdims=True)

    q = q_ref[0]
    km = kmean_ref[...]
    km_hi = km.astype(BF16)
    km_lo = (km - km_hi.astype(F32)).astype(BF16)
    gate = _dot_nt(km_hi, q) + _dot_nt(km_lo, q)
    bidx = lax.broadcasted_iota(jnp.int32, gate.shape, 0)
    qpos = i * tq + lax.broadcasted_iota(jnp.int32, gate.shape, 1)
    own = jnp.right_shift(qpos, blk_shift)
    past_blk = bidx < own
    gate = jnp.where(past_blk, gate, NEG_INF)
    rank = jnp.zeros(gate.shape, jnp.int32)
    for mth in range(nb):
        gm = gate[mth:mth + 1, :]
        beats = jnp.logical_or(gm > gate, jnp.logical_and(gm == gate, bidx > mth))
        rank = rank + beats.astype(jnp.int32)
    selected = jnp.logical_and(rank < MOBA_TOP_K, past_blk)
    bias_ref[...] = jnp.where(selected, 0.0, NEG_INF).astype(F32)

    def scores(j):
        start = pl.multiple_of(j * tk, tk)
        st = _dot_nt(k_ref[0, pl.ds(start, tk), :], q)
        return st, vt_ref[:, pl.ds(start, tk)]

    def past(j):
        st, vt = scores(j)
        st = jnp.concatenate(
            [st[b * blk:(b + 1) * blk, :] + bias_ref[pl.ds(j * blk_per_tile + b, 1), :]
             for b in range(blk_per_tile)], axis=0)
        return st, vt

    first_own_tile = i * own_tiles
    carry = _paired_loop(first_own_tile, past, _init_carry(tq))

    own_row = jnp.right_shift(i * tq + lax.broadcasted_iota(jnp.int32, (1, tq), 1), blk_shift)
    own = [scores(first_own_tile + t) for t in range(own_tiles)]
    for t in range(own_tiles):
        j = first_own_tile + t
        st, vt = own[t]
        kpos = j * tk + lax.broadcasted_iota(jnp.int32, st.shape, 0)
        qp = i * tq + lax.broadcasted_iota(jnp.int32, st.shape, 1)
        st = jnp.where(kpos <= qp, st, NEG_INF)
        parts = []
        for b in range(blk_per_tile):
            n = j * blk_per_tile + b
            row_bias = jnp.where(own_row > n, bias_ref[pl.ds(n, 1), :],
                                 jnp.where(own_row == n, 0.0, NEG_INF))
            parts.append(st[b * blk:(b + 1) * blk, :] + row_bias)
        carry = _softmax_step(jnp.concatenate(parts, axis=0), vt, carry)

    o_ref[...] = _normalized(carry).T.astype(o_ref.dtype)


def moba_attention(proj):
    _, s, _ = proj.shape
    assert s % MOBA_BLOCK == 0 and MOBA_BLOCK & (MOBA_BLOCK - 1) == 0
    nb = s // MOBA_BLOCK
    tq = _tile(s, (1024, 512, 256))
    tk = _tile(tq, (512, 256))
    return pl.pallas_call(
        functools.partial(_moba_attn_kernel, tq=tq, tk=tk, nb=nb),
        out_shape=jax.ShapeDtypeStruct((s, BRANCH_WIDTH), BF16),
        grid=(HEADS, s // tq),
        in_specs=[pl.BlockSpec((1, tq, LANES), lambda h, i: (SLAB_BQ + h, i, 0)),
                  pl.BlockSpec((1, s, LANES), lambda h, i: (SLAB_BK + h, 0, 0)),
                  pl.BlockSpec((1, s, LANES), lambda h, i: (SLAB_BV + h, 0, 0))],
        out_specs=pl.BlockSpec((tq, LANES), lambda h, i: (i, h)),
        scratch_shapes=[pltpu.VMEM((VT_ROWS, s), BF16),
                        pltpu.VMEM((nb, LANES), F32),
                        pltpu.VMEM((nb, tq), F32)],
        compiler_params=_params(("arbitrary", "arbitrary")),
        name="moba_attn",
    )(proj, proj, proj)


def _dilated_kernel(q_ref, k_ref, v_ref, kp_ref, vp_ref, o_ref, lse_ref,
                    qf_ref, kf_ref, vf_ref, kpf_ref, vpf_ref, *, tq, band, dilation):
    i = pl.program_id(1)
    qf_ref[...] = q_ref[0].astype(F32)
    kf_ref[...] = k_ref[0].astype(F32)
    vf_ref[...] = v_ref[0].astype(F32)
    kpf_ref[...] = kp_ref[0].astype(F32)
    vpf_ref[...] = vp_ref[0].astype(F32)

    def one_class(r, carry):
        cls = lambda ref, n: ref[pl.ds(r, n, stride=dilation), :].astype(BF16)
        q = cls(qf_ref, tq)
        s_c = _dot_nt(q, cls(kf_ref, tq))
        s_p = _dot_nt(q, cls(kpf_ref, band))
        rel = (lax.broadcasted_iota(jnp.int32, s_c.shape, 0)
               - lax.broadcasted_iota(jnp.int32, s_c.shape, 1))
        s_c = jnp.where(jnp.logical_and(rel >= 0, rel <= band), s_c, NEG_INF)
        rel_p = (lax.broadcasted_iota(jnp.int32, s_p.shape, 0)
                 - lax.broadcasted_iota(jnp.int32, s_p.shape, 1) + band)
        ok_p = jnp.logical_and(rel_p <= band, i > 0)
        s_p = jnp.where(ok_p, s_p, NEG_INF)
        m = jnp.maximum(jnp.max(s_c, axis=1, keepdims=True), jnp.max(s_p, axis=1, keepdims=True))
        e_c = jnp.exp2(s_c - m)
        e_p = jnp.exp2(s_p - m)
        den = jnp.sum(e_c, axis=1, keepdims=True) + jnp.sum(e_p, axis=1, keepdims=True)
        o = (jnp.dot(e_c.astype(BF16), cls(vf_ref, tq), preferred_element_type=F32)
             + jnp.dot(e_p.astype(BF16), cls(vpf_ref, band), preferred_element_type=F32))
        o_ref[0, pl.ds(r, tq, stride=dilation), :] = o / den
        lse2 = m + jnp.log2(den)
        lse_ref[0, pl.ds(r, tq, stride=dilation), :] = jnp.broadcast_to(lse2, (tq, LANES))
        return carry

    lax.fori_loop(0, dilation, one_class, 0)


def dilated_group(proj, g, window, dilation):
    _, s, _ = proj.shape
    band = window // dilation
    assert band == LANES and s % (band * dilation) == 0
    length = s // dilation
    tq = _tile(length, (512, 256, 128))
    per = tq // band
    rows, prev_rows = tq * dilation, band * dilation
    sq, sk, sv = SLAB_CQ + g * HEADS, SLAB_CK + g * HEADS, SLAB_CV + g * HEADS
    cur = lambda base: pl.BlockSpec((1, rows, LANES), lambda h, i: (base + h, i, 0))
    prev = lambda base: pl.BlockSpec(
        (1, prev_rows, LANES), lambda h, i: (base + h, jnp.maximum(i * per - 1, 0), 0))
    out_spec = pl.BlockSpec((1, rows, LANES), lambda h, i: (h, i, 0))
    return pl.pallas_call(
        functools.partial(_dilated_kernel, tq=tq, band=band, dilation=dilation),
        out_shape=(jax.ShapeDtypeStruct((HEADS, s, LANES), F32),
                   jax.ShapeDtypeStruct((HEADS, s, LANES), F32)),
        grid=(HEADS, length // tq),
        in_specs=[cur(sq), cur(sk), cur(sv), prev(sk), prev(sv)],
        out_specs=(out_spec, out_spec),
        scratch_shapes=[pltpu.VMEM((rows, LANES), F32)] * 3 + [pltpu.VMEM((prev_rows, LANES), F32)] * 2,
        compiler_params=_params(("parallel", "parallel")),
        name=f"dilated_d{dilation}",
    )(proj, proj, proj, proj, proj)


def _dil_merge_kernel(*refs):
    o_refs, lse_refs, out_ref = refs[:N_DIL], refs[N_DIL:2 * N_DIL], refs[2 * N_DIL]
    lses = [r[0] for r in lse_refs]
    m = functools.reduce(jnp.maximum, lses)
    es = [jnp.exp2(x - m) for x in lses]
    inv = 1.0 / functools.reduce(lambda a, b: a + b, es)
    acc = None
    for e, o_ref in zip(es, o_refs):
        term = (e * inv) * o_ref[0]
        acc = term if acc is None else acc + term
    out_ref[...] = acc.astype(out_ref.dtype)


def dilated_mixture(proj):
    _, s, _ = proj.shape
    outs, lses = [], []
    for g, (window, dilation) in enumerate(DIL_PATTERNS):
        o, lse = dilated_group(proj, g, window, dilation)
        outs.append(o)
        lses.append(lse)
    tm = _tile(s, (1024, 512, 256))
    spec = pl.BlockSpec((1, tm, LANES), lambda h, i: (h, i, 0))
    return pl.pallas_call(
        _dil_merge_kernel,
        out_shape=jax.ShapeDtypeStruct((s, BRANCH_WIDTH), BF16),
        grid=(HEADS, s // tm),
        in_specs=[spec] * (2 * N_DIL),
        out_specs=pl.BlockSpec((tm, LANES), lambda h, i: (i, h)),
        compiler_params=_params(("parallel", "parallel")),
        name="dilated_merge",
    )(*outs, *lses)


def _gate_mix_kernel(xg_ref, ssq_ref, oa_ref, ob_ref, oc_ref, wg0_ref, wg1_ref, wg2_ref,
                     b0_ref, b1_ref, b2_ref, wb_ref, z_ref):
    xg = xg_ref[...]
    rstd = _rstd(ssq_ref[...], xg.shape[1], NORM_EPS)
    z = None
    for n, (o_ref, wg_ref, b_ref) in enumerate(((oa_ref, wg0_ref, b0_ref),
                                                (ob_ref, wg1_ref, b1_ref),
                                                (oc_ref, wg2_ref, b2_ref))):
        pre = jnp.dot(xg, wg_ref[...].astype(BF16), preferred_element_type=F32)
        gate = jax.nn.sigmoid(pre * rstd + b_ref[...])
        y = jnp.dot(o_ref[...], wb_ref[n].astype(BF16), preferred_element_type=F32)
        z = gate * y if z is None else z + gate * y
    z_ref[...] = z.astype(z_ref.dtype)


def gate_mix(xg, ssq, o_a, o_b, o_c, w_gate_all, b_gate_all, w_branch_all, layer):
    s, d = xg.shape
    assert N_BRANCH == 3
    tm = _tile(s, (1024, 512, 256))
    tn = _tile(d, (256, 128))
    nj = d // tn
    o_spec = _resident((tm, BRANCH_WIDTH), lambda i, j: (i, 0))
    wg_spec = lambda n: pl.BlockSpec((None, d, tn), lambda i, j: (layer, 0, n * nj + j))
    b_spec = lambda n: pl.BlockSpec((None, 1, tn), lambda i, j: (layer, 0, n * nj + j))
    b3d = b_gate_all.reshape(b_gate_all.shape[0], 1, N_BRANCH * d)
    return pl.pallas_call(
        _gate_mix_kernel,
        out_shape=jax.ShapeDtypeStruct((s, d), BF16),
        grid=(s // tm, nj),
        in_specs=[_resident((tm, d), lambda i, j: (i, 0)),
                  _resident((tm, LANES), lambda i, j: (i, 0)),
                  o_spec, o_spec, o_spec,
                  wg_spec(0), wg_spec(1), wg_spec(2), b_spec(0), b_spec(1), b_spec(2),
                  pl.BlockSpec((None, N_BRANCH, BRANCH_WIDTH, tn), lambda i, j: (layer, 0, 0, j))],
        out_specs=pl.BlockSpec((tm, tn), lambda i, j: (i, j)),
        compiler_params=_params(("parallel", "arbitrary")),
        name="gate_mix",
    )(xg, ssq, o_a, o_b, o_c, w_gate_all, w_gate_all, w_gate_all,
      b3d, b3d, b3d, w_branch_all)


RESIDUAL_ROW_PARTS = 2


def _residual_matmul_kernel(a_ref, w_ref, x_ref, g_ref, o_ref, xg_ref, ssq_ref, *, k_chunks):
    @pl.when(pl.program_id(1) == 0)
    def _():
        ssq_ref[...] = jnp.zeros(ssq_ref.shape, F32)

    kc = a_ref.shape[1] // k_chunks
    ws = [w_ref[c * kc:(c + 1) * kc, :].astype(BF16) for c in range(k_chunks)]
    part = a_ref.shape[0] // RESIDUAL_ROW_PARTS
    for r in range(RESIDUAL_ROW_PARTS):
        rows = slice(r * part, (r + 1) * part)
        acc = x_ref[rows, :]
        for c in range(k_chunks):
            acc = acc + jnp.dot(a_ref[rows, c * kc:(c + 1) * kc], ws[c], preferred_element_type=F32)
        o_ref[rows, :] = acc
        xg_ref[rows, :] = (acc * g_ref[...]).astype(BF16)
        ssq_ref[rows, :] += _lane_folded_sumsq(acc)


def residual_matmul(a, w_all, layer, x, g_next, tm_prefs, tn_prefs, k_chunks=1):
    s, k = a.shape
    d = w_all.shape[2]
    tm = _tile(s, tm_prefs)
    tn = _tile(d, tn_prefs)
    assert k % (k_chunks * LANES) == 0
    return pl.pallas_call(
        functools.partial(_residual_matmul_kernel, k_chunks=k_chunks),
        out_shape=(jax.ShapeDtypeStruct((s, d), F32), jax.ShapeDtypeStruct((s, d), BF16),
                   jax.ShapeDtypeStruct((s, LANES), F32)),
        grid=(s // tm, d // tn),
        in_specs=[_resident((tm, k), lambda i, j: (i, 0)),
                  pl.BlockSpec((None, k, tn), lambda i, j: (layer, 0, j)),
                  pl.BlockSpec((tm, tn), lambda i, j: (i, j)),
                  pl.BlockSpec((1, tn), lambda i, j: (0, j))],
        out_specs=(pl.BlockSpec((tm, tn), lambda i, j: (i, j)),
                   pl.BlockSpec((tm, tn), lambda i, j: (i, j)),
                   pl.BlockSpec((tm, LANES), lambda i, j: (i, 0))),
        input_output_aliases={2: 0},
        compiler_params=_params(("parallel", "arbitrary")),
        name="residual_matmul",
    )(a, w_all, x, g_next.reshape(1, d))


SWIGLU_ROW_PARTS = 4


def _swiglu_kernel(xg_ref, ssq_ref, wg_ref, wu_ref, o_ref):
    d = xg_ref.shape[1]
    wg = wg_ref[...].astype(BF16)
    wu = wu_ref[...].astype(BF16)
    part = xg_ref.shape[0] // SWIGLU_ROW_PARTS
    for r in range(SWIGLU_ROW_PARTS):
        rows = slice(r * part, (r + 1) * part)
        xg = xg_ref[rows, :]
        rstd = _rstd(ssq_ref[rows, :], d, NORM_EPS)
        g = jnp.dot(xg, wg, preferred_element_type=F32) * rstd
        u = jnp.dot(xg, wu, preferred_element_type=F32) * rstd
        o_ref[rows, :] = (g * jax.nn.sigmoid(g) * u).astype(o_ref.dtype)


def swiglu_up(xg, ssq, w_g_all, w_u_all, layer):
    s, d = xg.shape
    f = w_g_all.shape[2]
    tm = _tile(s, (2048, 1024, 512, 256))
    tn = _tile(f, (256, 128))
    w_spec = pl.BlockSpec((None, d, tn), lambda i, j: (layer, 0, j))
    return pl.pallas_call(
        _swiglu_kernel,
        out_shape=jax.ShapeDtypeStruct((s, f), BF16),
        grid=(s // tm, f // tn),
        in_specs=[_resident((tm, d), lambda i, j: (i, 0)),
                  _resident((tm, LANES), lambda i, j: (i, 0)),
                  w_spec, w_spec],
        out_specs=pl.BlockSpec((tm, tn), lambda i, j: (i, j)),
        compiler_params=_params(("parallel", "arbitrary")),
        name="swiglu_up",
    )(xg, ssq, w_g_all, w_u_all)


def kernel(x, g_mix, w_in, lam_q1, lam_k1, lam_q2, lam_k2, g_subln, w_gate, b_gate, w_branch,
           w_out, g_ffn, w_ffn_gate, w_ffn_up, w_ffn_down, g_final):
    b, s, d = x.shape
    depth = w_in.shape[0]
    tabs = rope_tables(s)
    down_chunks = 2 if w_ffn_down.shape[1] % (2 * LANES) == 0 else 1
    outs = []
    for bi in range(b):
        xr = x.reshape(s, d) if b == 1 else x[bi]
        xg, ssq = norm_prep(xr, g_mix[0])
        for l in range(depth):
            lam_init = 0.8 - 0.6 * math.exp(-0.3 * l)
            pad = lambda v: jnp.pad(v.astype(F32), (0, LANES - DIFF_QK_DIM))
            lam_params = jnp.stack(
                [pad(lam_q1[l]), pad(lam_k1[l]), pad(lam_q2[l]), pad(lam_k2[l]),
                 jnp.full((LANES,), lam_init, F32)]
                + [jnp.zeros((LANES,), F32)] * 3)
            proj = in_proj(xg, ssq, w_in, l, tabs)
            o_a = diff_attention(proj, lam_params, g_subln[l])
            o_b = moba_attention(proj)
            o_c = dilated_mixture(proj)
            z = gate_mix(xg, ssq, o_a, o_b, o_c, w_gate, b_gate, w_branch, l)
            xr, xg, ssq = residual_matmul(z, w_out, l, xr, g_ffn[l], (2048, 1024, 512, 256), (256, 128))
            u = swiglu_up(xg, ssq, w_ffn_gate, w_ffn_up, l)
            g_next = g_mix[l + 1] if l + 1 < depth else g_final
            xr, xg, ssq = residual_matmul(u, w_ffn_down, l, xr, g_next, (1024, 512, 256), (256, 128),
                                          down_chunks)
        outs.append(rmsnorm(xr, g_final, NORM_EPS, x.dtype))
    return outs[0].reshape(b, s, d) if b == 1 else jnp.stack(outs, axis=0)
```

```python
import functools
import math

import jax
import jax.numpy as jnp
from jax import lax
from jax.experimental import pallas as pl
from jax.experimental.pallas import tpu as pltpu

HEAD_DIM = 128
DIFF_HEADS = 4
DIFF_QK_DIM = HEAD_DIM // 2
MOBA_HEADS = 4
MOBA_BLOCK = 256
MOBA_TOP_K = 3
DIL_PATTERNS = ((128, 1), (512, 4), (2048, 16))
DIL_HEADS = 4
N_BRANCH = 3
BRANCH_WIDTH = 512
ROPE_THETA = 10000.0
NORM_EPS = 1e-6
SUBLN_EPS = 1e-5
NEG_INF = -1e30
LOG2_E = math.log2(math.e)

N_DIL = len(DIL_PATTERNS)
HEADS = 4
SLAB_AQ, SLAB_AK, SLAB_AV = 0, 4, 8
SLAB_BQ, SLAB_BK, SLAB_BV = 12, 16, 20
SLAB_CQ, SLAB_CK, SLAB_CV = 24, 36, 48
N_SLABS = 60
PROJ_TILE = 512

LANES = 128
BF16_SUBLANES = 16
VMEM_LIMIT_BYTES = 58 * 1024 * 1024

F32 = jnp.float32
BF16 = jnp.bfloat16


def _params(sem, vmem=VMEM_LIMIT_BYTES):
    return pltpu.CompilerParams(dimension_semantics=sem, vmem_limit_bytes=vmem)


def _tile(n, prefs):
    for p in prefs:
        if n % p == 0:
            return p
    return n


def _resident(block_shape, index_map):
    return pl.BlockSpec(block_shape, index_map, pipeline_mode=pl.Buffered(1))


def _lane_folded_sumsq(x):
    sq = x * x
    out = sq[:, 0:LANES]
    for c in range(1, x.shape[1] // LANES):
        out = out + sq[:, c * LANES:(c + 1) * LANES]
    return out


def _rstd(ssq, d, eps):
    return lax.rsqrt(jnp.sum(ssq, axis=1, keepdims=True) * (1.0 / d) + eps)


def _norm_prep_kernel(x_ref, g_ref, xg_ref, ssq_ref):
    x = x_ref[...]
    xg_ref[...] = (x * g_ref[...]).astype(BF16)
    ssq_ref[...] = _lane_folded_sumsq(x)


def norm_prep(x, g):
    s, d = x.shape
    tm = _tile(s, (256, 128, 8))
    return pl.pallas_call(
        _norm_prep_kernel,
        out_shape=(jax.ShapeDtypeStruct((s, d), BF16), jax.ShapeDtypeStruct((s, LANES), F32)),
        grid=(s // tm,),
        in_specs=[pl.BlockSpec((tm, d), lambda i: (i, 0)), pl.BlockSpec((1, d), lambda i: (0, 0))],
        out_specs=(pl.BlockSpec((tm, d), lambda i: (i, 0)), pl.BlockSpec((tm, LANES), lambda i: (i, 0))),
        compiler_params=_params(("parallel",)),
        name="norm_prep",
    )(x, g.reshape(1, d))


def _rmsnorm_kernel(x_ref, g_ref, o_ref, *, eps):
    x = x_ref[...]
    ms = jnp.mean(x * x, axis=-1, keepdims=True)
    o_ref[...] = (x * lax.rsqrt(ms + eps) * g_ref[...]).astype(o_ref.dtype)


def rmsnorm(x, g, eps, out_dtype):
    s, d = x.shape
    tm = _tile(s, (256, 128, 8))
    return pl.pallas_call(
        functools.partial(_rmsnorm_kernel, eps=eps),
        out_shape=jax.ShapeDtypeStruct((s, d), out_dtype),
        grid=(s // tm,),
        in_specs=[pl.BlockSpec((tm, d), lambda i: (i, 0)),
                  pl.BlockSpec((1, d), lambda i: (0, 0))],
        out_specs=pl.BlockSpec((tm, d), lambda i: (i, 0)),
        compiler_params=_params(("parallel",)),
        name="rmsnorm",
    )(x, g.reshape(1, d))


KIND_PLAIN, KIND_ROPE128_Q, KIND_ROPE128_K, KIND_ROPE64_Q, KIND_ROPE64_K = range(5)
_TILE_KINDS = (KIND_ROPE64_Q, KIND_ROPE64_K, KIND_PLAIN,
               KIND_ROPE128_Q, KIND_ROPE128_K, KIND_PLAIN,
               KIND_ROPE128_Q, KIND_ROPE128_Q, KIND_ROPE128_Q,
               KIND_ROPE128_K, KIND_ROPE128_K, KIND_ROPE128_K,
               KIND_PLAIN, KIND_PLAIN, KIND_PLAIN)
IN_PROJ_ROW_PARTS = 8


def _tile_kind(j):
    kind = jnp.int32(_TILE_KINDS[-1])
    for t in range(len(_TILE_KINDS) - 2, -1, -1):
        kind = jnp.where(j == t, _TILE_KINDS[t], kind)
    return kind


def _in_proj_kernel(xg_ref, ssq_ref, w_ref, ta_ref, tb_ref, tc_ref, td_ref, o_ref):
    d = xg_ref.shape[1]
    w = w_ref[...].astype(BF16)
    part = xg_ref.shape[0] // IN_PROJ_ROW_PARTS
    for r in range(IN_PROJ_ROW_PARTS):
        rows = slice(r * part, (r + 1) * part)
        acc = jnp.dot(xg_ref[rows, :], w, preferred_element_type=F32)
        acc = acc * _rstd(ssq_ref[rows, :], d, NORM_EPS)
        ta, tb, tc, td = ta_ref[rows, :], tb_ref[rows, :], tc_ref[rows, :], td_ref[rows, :]
        for c in range(PROJ_TILE // LANES):
            xs = acc[:, c * LANES:(c + 1) * LANES]
            y = (xs * ta + pltpu.roll(xs, 64, 1) * tb
                 + pltpu.roll(xs, 32, 1) * tc + pltpu.roll(xs, 96, 1) * td)
            o_ref[c, rows, :] = y.astype(o_ref.dtype)


def in_proj(xg, ssq, w_all, layer, tabs):
    s, d = xg.shape
    n = w_all.shape[2]
    assert n // PROJ_TILE == len(_TILE_KINDS)
    tm = _tile(s, (2048, 1024, 512, 256))
    tab_spec = pl.BlockSpec((None, tm, LANES), lambda i, j: (_tile_kind(j), i, 0))
    return pl.pallas_call(
        _in_proj_kernel,
        out_shape=jax.ShapeDtypeStruct((n // LANES, s, LANES), BF16),
        grid=(s // tm, n // PROJ_TILE),
        in_specs=[_resident((tm, d), lambda i, j: (i, 0)),
                  _resident((tm, LANES), lambda i, j: (i, 0)),
                  pl.BlockSpec((None, d, PROJ_TILE), lambda i, j: (layer, 0, j)),
                  tab_spec, tab_spec, tab_spec, tab_spec],
        out_specs=pl.BlockSpec((PROJ_TILE // LANES, tm, LANES), lambda i, j: (j, i, 0)),
        compiler_params=_params(("parallel", "arbitrary")),
        name="in_proj",
    )(xg, ssq, w_all, *tabs)


def rope_tables(seq):
    def angles(dim):
        inv = ROPE_THETA ** (-jnp.arange(0, dim, 2, dtype=F32) / dim)
        ang = jnp.arange(seq, dtype=F32)[:, None] * inv[None, :]
        reps = LANES // dim
        cos = jnp.tile(jnp.concatenate([jnp.cos(ang)] * 2, axis=1), (1, reps))
        sin = jnp.tile(jnp.concatenate([-jnp.sin(ang), jnp.sin(ang)], axis=1), (1, reps))
        return cos, sin
    c128, s128 = angles(HEAD_DIM)
    c64, s64 = angles(DIFF_QK_DIM)
    zero, one = jnp.zeros((seq, LANES), F32), jnp.ones((seq, LANES), F32)
    low = (jnp.arange(LANES) & (DIFF_QK_DIM // 2)) == 0
    s64_up, s64_down = jnp.where(low, s64, 0.0), jnp.where(low, 0.0, s64)
    q128, q64 = HEAD_DIM ** -0.5 * LOG2_E, DIFF_QK_DIM ** -0.5 * LOG2_E
    kinds = {KIND_PLAIN: (one, zero, zero, zero),
             KIND_ROPE128_Q: (c128 * q128, s128 * q128, zero, zero),
             KIND_ROPE128_K: (c128, s128, zero, zero),
             KIND_ROPE64_Q: (c64 * q64, zero, s64_down * q64, s64_up * q64),
             KIND_ROPE64_K: (c64, zero, s64_down, s64_up)}
    return tuple(jnp.stack([kinds[k][t] for k in range(len(kinds))]) for t in range(4))


VT_ROWS = HEAD_DIM + BF16_SUBLANES


def _transpose_bf16(x):
    return x.astype(F32).T.astype(BF16)


def _dot_nt(a, b):
    return lax.dot_general(a, b, (((1,), (1,)), ((), ())), preferred_element_type=F32)


def _fill_v_transposed(v_ref, vt_ref, chunk):
    seq = v_ref.shape[1]
    for c in range(seq // chunk):
        vt_ref[0:HEAD_DIM, c * chunk:(c + 1) * chunk] = _transpose_bf16(
            v_ref[0, c * chunk:(c + 1) * chunk, :])
    vt_ref[HEAD_DIM:VT_ROWS, :] = jnp.ones((BF16_SUBLANES, seq), BF16)


def _softmax_step(st, vt, carry):
    m_old, acc = carry
    m_new = jnp.maximum(m_old, jnp.max(st, axis=0, keepdims=True))
    alpha = jnp.exp2(m_old - m_new)
    p = jnp.exp2(st - m_new).astype(BF16)
    return m_new, alpha * acc + jnp.dot(vt, p, preferred_element_type=F32)


def _init_carry(nq):
    return jnp.full((1, nq), NEG_INF, F32), jnp.zeros((VT_ROWS, nq), F32)


def _normalized(carry):
    acc = carry[1]
    return acc[0:HEAD_DIM, :] / acc[HEAD_DIM:HEAD_DIM + 1, :]


def _softmax_pair(a, b, carry):
    return _softmax_step(*b, _softmax_step(*a, carry))


def _paired_loop(n, tile, carry):
    carry = lax.fori_loop(0, n // 2, lambda t, c: _softmax_pair(tile(2 * t), tile(2 * t + 1), c), carry)
    return lax.cond(n % 2 == 1, lambda c: _softmax_step(*tile(n - 1), c), lambda c: c, carry)


def _diff_attn_kernel(q_ref, k_ref, v_ref, lam_ref, gsub_ref, o_ref, vt_ref, qs_ref, *, tq):
    i = pl.program_id(1)

    @pl.when(i == 0)
    def _():
        _fill_v_transposed(v_ref, vt_ref, tq)

    q = q_ref[0]
    lane = lax.broadcasted_iota(jnp.int32, q.shape, 1)
    zero = jnp.zeros_like(q)
    qs_ref[0:tq, :] = jnp.where(lane < DIFF_QK_DIM, q, zero)
    qs_ref[tq:2 * tq, :] = jnp.where(lane >= DIFF_QK_DIM, q, zero)

    def tile(j):
        start = pl.multiple_of(j * tq, tq)
        return _dot_nt(k_ref[0, pl.ds(start, tq), :], qs_ref[...]), vt_ref[:, pl.ds(start, tq)]

    def diagonal():
        st, vt = tile(i)
        krow = lax.broadcasted_iota(jnp.int32, st.shape, 0)
        qcol = lax.broadcasted_iota(jnp.int32, st.shape, 1)
        qcol = jnp.where(qcol >= tq, qcol - tq, qcol)
        return jnp.where(krow <= qcol, st, NEG_INF), vt

    carry = lax.fori_loop(
        0, i // 2, lambda t, c: _softmax_pair(tile(2 * t), tile(2 * t + 1), c), _init_carry(2 * tq))
    carry = lax.cond(i % 2 == 1,
                     lambda c: _softmax_pair(tile(i - 1), diagonal(), c),
                     lambda c: _softmax_step(*diagonal(), c), carry)

    lp = lam_ref[...]
    lam_init = lp[4:5, 0:1]
    lam = (jnp.exp(jnp.sum(lp[0:1] * lp[1:2], axis=1, keepdims=True))
           - jnp.exp(jnp.sum(lp[2:3] * lp[3:4], axis=1, keepdims=True)) + lam_init)
    o = _normalized(carry)
    od = o[:, 0:tq] - lam * o[:, tq:2 * tq]
    ms = jnp.mean(od * od, axis=0, keepdims=True)
    y = od * lax.rsqrt(ms + SUBLN_EPS) * gsub_ref[...] * (1.0 - lam_init)
    o_ref[...] = y.T.astype(o_ref.dtype)


def diff_attention(proj, lam_params, g_sub):
    _, s, _ = proj.shape
    tq = _tile(s, (512, 256, 128))
    return pl.pallas_call(
        functools.partial(_diff_attn_kernel, tq=tq),
        out_shape=jax.ShapeDtypeStruct((s, BRANCH_WIDTH), BF16),
        grid=(HEADS, s // tq),
        in_specs=[pl.BlockSpec((1, tq, LANES), lambda h, i: (SLAB_AQ + h, i, 0)),
                  pl.BlockSpec((1, s, LANES), lambda h, i: (SLAB_AK + h, 0, 0)),
                  pl.BlockSpec((1, s, LANES), lambda h, i: (SLAB_AV + h, 0, 0)),
                  pl.BlockSpec((8, LANES), lambda h, i: (0, 0)),
                  pl.BlockSpec((HEAD_DIM, 1), lambda h, i: (0, 0))],
        out_specs=pl.BlockSpec((tq, LANES), lambda h, i: (i, h)),
        scratch_shapes=[pltpu.VMEM((VT_ROWS, s), BF16),
                        pltpu.VMEM((2 * tq, LANES), BF16)],
        compiler_params=_params(("arbitrary", "arbitrary")),
        name="diff_attn",
    )(proj, proj, proj, lam_params, g_sub.reshape(HEAD_DIM, 1))


def _moba_attn_kernel(q_ref, k_ref, v_ref, o_ref, vt_ref, kmean_ref, bias_ref, *, tq, tk, nb):
    i = pl.program_id(1)
    blk = MOBA_BLOCK
    blk_shift = blk.bit_length() - 1
    blk_per_tile = tk // blk
    own_tiles = tq // tk

    @pl.when(i == 0)
    def _():
        _fill_v_transposed(v_ref, vt_ref, tk)
        for n in range(nb):
            kmean_ref[n:n + 1, :] = jnp.mean(
                k_ref[0, n * blk:(n + 1) * blk, :].astype(F32), axis=0, keepdims=True)

    q = q_ref[0]
    km = kmean_ref[...]
    km_hi = km.astype(BF16)
    km_lo = (km - km_hi.astype(F32)).astype(BF16)
    gate = _dot_nt(km_hi, q) + _dot_nt(km_lo, q)
    bidx = lax.broadcasted_iota(jnp.int32, gate.shape, 0)
    qpos = i * tq + lax.broadcasted_iota(jnp.int32, gate.shape, 1)
    own = jnp.right_shift(qpos, blk_shift)
    past_blk = bidx < own
    gate = jnp.where(past_blk, gate, NEG_INF)
    rank = jnp.zeros(gate.shape, jnp.int32)
    for mth in range(nb):
        gm = gate[mth:mth + 1, :]
        beats = jnp.logical_or(gm > gate, jnp.logical_and(gm == gate, bidx > mth))
        rank = rank + beats.astype(jnp.int32)
    selected = jnp.logical_and(rank < MOBA_TOP_K, past_blk)
    bias_ref[...] = jnp.where(selected, 0.0, NEG_INF).astype(F32)

    def scores(j):
        start = pl.multiple_of(j * tk, tk)
        st = _dot_nt(k_ref[0, pl.ds(start, tk), :], q)
        return st, vt_ref[:, pl.ds(start, tk)]

    def past(j):
        st, vt = scores(j)
        st = jnp.concatenate(
            [st[b * blk:(b + 1) * blk, :] + bias_ref[pl.ds(j * blk_per_tile + b, 1), :]
             for b in range(blk_per_tile)], axis=0)
        return st, vt

    first_own_tile = i * own_tiles
    carry = _paired_loop(first_own_tile, past, _init_carry(tq))

    own_row = jnp.right_shift(i * tq + lax.broadcasted_iota(jnp.int32, (1, tq), 1), blk_shift)
    own = [scores(first_own_tile + t) for t in range(own_tiles)]
    for t in range(own_tiles):
        j = first_own_tile + t
        st, vt = own[t]
        kpos = j * tk + lax.broadcasted_iota(jnp.int32, st.shape, 0)
        qp = i * tq + lax.broadcasted_iota(jnp.int32, st.shape, 1)
        st = jnp.where(kpos <= qp, st, NEG_INF)
        parts = []
        for b in range(blk_per_tile):
            n = j * blk_per_tile + b
            row_bias = jnp.where(own_row > n, bias_ref[pl.ds(n, 1), :],
                                 jnp.where(own_row == n, 0.0, NEG_INF))
            parts.append(st[b * blk:(b + 1) * blk, :] + row_bias)
        carry = _softmax_step(jnp.concatenate(parts, axis=0), vt, carry)

    o_ref[...] = _normalized(carry).T.astype(o_ref.dtype)


def moba_attention(proj):
    _, s, _ = proj.shape
    assert s % MOBA_BLOCK == 0 and MOBA_BLOCK & (MOBA_BLOCK - 1) == 0
    nb = s // MOBA_BLOCK
    tq = _tile(s, (1024, 512, 256))
    tk = _tile(tq, (512, 256))
    return pl.pallas_call(
        functools.partial(_moba_attn_kernel, tq=tq, tk=tk, nb=nb),
        out_shape=jax.ShapeDtypeStruct((s, BRANCH_WIDTH), BF16),
        grid=(HEADS, s // tq),
        in_specs=[pl.BlockSpec((1, tq, LANES), lambda h, i: (SLAB_BQ + h, i, 0)),
                  pl.BlockSpec((1, s, LANES), lambda h, i: (SLAB_BK + h, 0, 0)),
                  pl.BlockSpec((1, s, LANES), lambda h, i: (SLAB_BV + h, 0, 0))],
        out_specs=pl.BlockSpec((tq, LANES), lambda h, i: (i, h)),
        scratch_shapes=[pltpu.VMEM((VT_ROWS, s), BF16),
                        pltpu.VMEM((nb, LANES), F32),
                        pltpu.VMEM((nb, tq), F32)],
        compiler_params=_params(("arbitrary", "arbitrary")),
        name="moba_attn",
    )(proj, proj, proj)


CLASS_UNROLL = 4


def _dilated_kernel(q_ref, k_ref, v_ref, kp_ref, vp_ref, o_ref, lse_ref,
                    qf_ref, kf_ref, vf_ref, kpf_ref, vpf_ref, *, tq, band, dilation):
    i = pl.program_id(1)
    qf_ref[...] = q_ref[0].astype(F32)
    kf_ref[...] = k_ref[0].astype(F32)
    vf_ref[...] = v_ref[0].astype(F32)
    kpf_ref[...] = kp_ref[0].astype(F32)
    vpf_ref[...] = vp_ref[0].astype(F32)

    def one_class(r, carry):
        cls = lambda ref, n: ref[pl.ds(r, n, stride=dilation), :].astype(BF16)
        q = cls(qf_ref, tq)
        k = jnp.concatenate([cls(kpf_ref, band), cls(kf_ref, tq)], axis=0)
        v = jnp.concatenate([cls(vpf_ref, band), cls(vf_ref, tq)], axis=0)
        outs, lses = [], []
        for u in range(tq // band):
            ku = k[u * band:(u + 2) * band, :]
            vu = v[u * band:(u + 2) * band, :]
            sc = _dot_nt(q[u * band:(u + 1) * band, :], ku)
            qi = lax.broadcasted_iota(jnp.int32, sc.shape, 0)
            kj = lax.broadcasted_iota(jnp.int32, sc.shape, 1)
            ok = jnp.logical_and(kj >= qi, kj <= qi + band)
            if u == 0:
                ok = jnp.logical_and(ok, jnp.logical_or(kj >= band, i > 0))
            sc = jnp.where(ok, sc, NEG_INF)
            m = jnp.max(sc, axis=1, keepdims=True)
            e = jnp.exp2(sc - m)
            den = jnp.sum(e, axis=1, keepdims=True)
            outs.append(jnp.dot(e.astype(BF16), vu, preferred_element_type=F32) / den)
            lses.append(jnp.broadcast_to(m + jnp.log2(den), (band, LANES)))
        o_ref[0, pl.ds(r, tq, stride=dilation), :] = jnp.concatenate(outs, axis=0)
        lse_ref[0, pl.ds(r, tq, stride=dilation), :] = jnp.concatenate(lses, axis=0)
        return carry

    unroll = min(dilation, CLASS_UNROLL)

    def classes(t, carry):
        for u in range(unroll):
            one_class(unroll * t + u, carry)
        return carry

    lax.fori_loop(0, dilation // unroll, classes, 0)


def dilated_group(proj, g, window, dilation):
    _, s, _ = proj.shape
    band = window // dilation
    assert band == LANES and s % (band * dilation) == 0
    length = s // dilation
    tq = _tile(length, (max(512, 2048 // dilation), 512, 256, 128))
    per = tq // band
    rows, prev_rows = tq * dilation, band * dilation
    sq, sk, sv = SLAB_CQ + g * HEADS, SLAB_CK + g * HEADS, SLAB_CV + g * HEADS
    cur = lambda base: pl.BlockSpec((1, rows, LANES), lambda h, i: (base + h, i, 0))
    prev = lambda base: pl.BlockSpec(
        (1, prev_rows, LANES), lambda h, i: (base + h, jnp.maximum(i * per - 1, 0), 0))
    out_spec = pl.BlockSpec((1, rows, LANES), lambda h, i: (h, i, 0))
    return pl.pallas_call(
        functools.partial(_dilated_kernel, tq=tq, band=band, dilation=dilation),
        out_shape=(jax.ShapeDtypeStruct((HEADS, s, LANES), F32),
                   jax.ShapeDtypeStruct((HEADS, s, LANES), F32)),
        grid=(HEADS, length // tq),
        in_specs=[cur(sq), cur(sk), cur(sv), prev(sk), prev(sv)],
        out_specs=(out_spec, out_spec),
        scratch_shapes=[pltpu.VMEM((rows, LANES), F32)] * 3 + [pltpu.VMEM((prev_rows, LANES), F32)] * 2,
        compiler_params=_params(("parallel", "parallel")),
        name=f"dilated_d{dilation}",
    )(proj, proj, proj, proj, proj)


def _dil_merge_kernel(*refs):
    o_refs, lse_refs, out_ref = refs[:N_DIL], refs[N_DIL:2 * N_DIL], refs[2 * N_DIL]
    lses = [r[0] for r in lse_refs]
    m = functools.reduce(jnp.maximum, lses)
    es = [jnp.exp2(x - m) for x in lses]
    inv = 1.0 / functools.reduce(lambda a, b: a + b, es)
    acc = None
    for e, o_ref in zip(es, o_refs):
        term = (e * inv) * o_ref[0]
        acc = term if acc is None else acc + term
    out_ref[...] = acc.astype(out_ref.dtype)


def dilated_mixture(proj):
    _, s, _ = proj.shape
    outs, lses = [], []
    for g, (window, dilation) in enumerate(DIL_PATTERNS):
        o, lse = dilated_group(proj, g, window, dilation)
        outs.append(o)
        lses.append(lse)
    tm = _tile(s, (1024, 512, 256))
    spec = pl.BlockSpec((1, tm, LANES), lambda h, i: (h, i, 0))
    return pl.pallas_call(
        _dil_merge_kernel,
        out_shape=jax.ShapeDtypeStruct((s, BRANCH_WIDTH), BF16),
        grid=(HEADS, s // tm),
        in_specs=[spec] * (2 * N_DIL),
        out_specs=pl.BlockSpec((tm, LANES), lambda h, i: (i, h)),
        compiler_params=_params(("parallel", "parallel")),
        name="dilated_merge",
    )(*outs, *lses)


def _gate_mix_kernel(xg_ref, ssq_ref, oa_ref, ob_ref, oc_ref, wg0_ref, wg1_ref, wg2_ref,
                     b0_ref, b1_ref, b2_ref, wb_ref, z_ref):
    xg = xg_ref[...]
    rstd = _rstd(ssq_ref[...], xg.shape[1], NORM_EPS)
    z = None
    for n, (o_ref, wg_ref, b_ref) in enumerate(((oa_ref, wg0_ref, b0_ref),
                                                (ob_ref, wg1_ref, b1_ref),
                                                (oc_ref, wg2_ref, b2_ref))):
        pre = jnp.dot(xg, wg_ref[...].astype(BF16), preferred_element_type=F32)
        gate = jax.nn.sigmoid(pre * rstd + b_ref[...])
        y = jnp.dot(o_ref[...], wb_ref[n].astype(BF16), preferred_element_type=F32)
        z = gate * y if z is None else z + gate * y
    z_ref[...] = z.astype(z_ref.dtype)


def gate_mix(xg, ssq, o_a, o_b, o_c, w_gate_all, b_gate_all, w_branch_all, layer):
    s, d = xg.shape
    assert N_BRANCH == 3
    tm = _tile(s, (1024, 512, 256))
    tn = _tile(d, (256, 128))
    nj = d // tn
    o_spec = _resident((tm, BRANCH_WIDTH), lambda i, j: (i, 0))
    wg_spec = lambda n: pl.BlockSpec((None, d, tn), lambda i, j: (layer, 0, n * nj + j))
    b_spec = lambda n: pl.BlockSpec((None, 1, tn), lambda i, j: (layer, 0, n * nj + j))
    b3d = b_gate_all.reshape(b_gate_all.shape[0], 1, N_BRANCH * d)
    return pl.pallas_call(
        _gate_mix_kernel,
        out_shape=jax.ShapeDtypeStruct((s, d), BF16),
        grid=(s // tm, nj),
        in_specs=[_resident((tm, d), lambda i, j: (i, 0)),
                  _resident((tm, LANES), lambda i, j: (i, 0)),
                  o_spec, o_spec, o_spec,
                  wg_spec(0), wg_spec(1), wg_spec(2), b_spec(0), b_spec(1), b_spec(2),
                  pl.BlockSpec((None, N_BRANCH, BRANCH_WIDTH, tn), lambda i, j: (layer, 0, 0, j))],
        out_specs=pl.BlockSpec((tm, tn), lambda i, j: (i, j)),
        compiler_params=_params(("parallel", "arbitrary")),
        name="gate_mix",
    )(xg, ssq, o_a, o_b, o_c, w_gate_all, w_gate_all, w_gate_all,
      b3d, b3d, b3d, w_branch_all)


RESIDUAL_ROW_PARTS = 2


def _residual_matmul_kernel(a_ref, w_ref, x_ref, g_ref, o_ref, xg_ref, ssq_ref, *, k_chunks):
    @pl.when(pl.program_id(1) == 0)
    def _():
        ssq_ref[...] = jnp.zeros(ssq_ref.shape, F32)

    kc = a_ref.shape[1] // k_chunks
    ws = [w_ref[c * kc:(c + 1) * kc, :].astype(BF16) for c in range(k_chunks)]
    part = a_ref.shape[0] // RESIDUAL_ROW_PARTS
    for r in range(RESIDUAL_ROW_PARTS):
        rows = slice(r * part, (r + 1) * part)
        acc = x_ref[rows, :]
        for c in range(k_chunks):
            acc = acc + jnp.dot(a_ref[rows, c * kc:(c + 1) * kc], ws[c], preferred_element_type=F32)
        o_ref[rows, :] = acc
        xg_ref[rows, :] = (acc * g_ref[...]).astype(BF16)
        ssq_ref[rows, :] += _lane_folded_sumsq(acc)


def residual_matmul(a, w_all, layer, x, g_next, tm_prefs, tn_prefs, k_chunks=1):
    s, k = a.shape
    d = w_all.shape[2]
    tm = _tile(s, tm_prefs)
    tn = _tile(d, tn_prefs)
    assert k % (k_chunks * LANES) == 0
    return pl.pallas_call(
        functools.partial(_residual_matmul_kernel, k_chunks=k_chunks),
        out_shape=(jax.ShapeDtypeStruct((s, d), F32), jax.ShapeDtypeStruct((s, d), BF16),
                   jax.ShapeDtypeStruct((s, LANES), F32)),
        grid=(s // tm, d // tn),
        in_specs=[_resident((tm, k), lambda i, j: (i, 0)),
                  pl.BlockSpec((None, k, tn), lambda i, j: (layer, 0, j)),
                  pl.BlockSpec((tm, tn), lambda i, j: (i, j)),
                  pl.BlockSpec((1, tn), lambda i, j: (0, j))],
        out_specs=(pl.BlockSpec((tm, tn), lambda i, j: (i, j)),
                   pl.BlockSpec((tm, tn), lambda i, j: (i, j)),
                   pl.BlockSpec((tm, LANES), lambda i, j: (i, 0))),
        input_output_aliases={2: 0},
        compiler_params=_params(("parallel", "arbitrary")),
        name="residual_matmul",
    )(a, w_all, x, g_next.reshape(1, d))


SWIGLU_ROW_PARTS = 4


def _swiglu_kernel(xg_ref, ssq_ref, wg_ref, wu_ref, o_ref):
    d = xg_ref.shape[1]
    wg = wg_ref[...].astype(BF16)
    wu = wu_ref[...].astype(BF16)
    part = xg_ref.shape[0] // SWIGLU_ROW_PARTS
    for r in range(SWIGLU_ROW_PARTS):
        rows = slice(r * part, (r + 1) * part)
        xg = xg_ref[rows, :]
        rstd = _rstd(ssq_ref[rows, :], d, NORM_EPS)
        g = jnp.dot(xg, wg, preferred_element_type=F32) * rstd
        u = jnp.dot(xg, wu, preferred_element_type=F32) * rstd
        o_ref[rows, :] = (g * jax.nn.sigmoid(g) * u).astype(o_ref.dtype)


def swiglu_up(xg, ssq, w_g_all, w_u_all, layer):
    s, d = xg.shape
    f = w_g_all.shape[2]
    tm = _tile(s, (2048, 1024, 512, 256))
    tn = _tile(f, (256, 128))
    w_spec = pl.BlockSpec((None, d, tn), lambda i, j: (layer, 0, j))
    return pl.pallas_call(
        _swiglu_kernel,
        out_shape=jax.ShapeDtypeStruct((s, f), BF16),
        grid=(s // tm, f // tn),
        in_specs=[_resident((tm, d), lambda i, j: (i, 0)),
                  _resident((tm, LANES), lambda i, j: (i, 0)),
                  w_spec, w_spec],
        out_specs=pl.BlockSpec((tm, tn), lambda i, j: (i, j)),
        compiler_params=_params(("parallel", "arbitrary")),
        name="swiglu_up",
    )(xg, ssq, w_g_all, w_u_all)


def kernel(x, g_mix, w_in, lam_q1, lam_k1, lam_q2, lam_k2, g_subln, w_gate, b_gate, w_branch,
           w_out, g_ffn, w_ffn_gate, w_ffn_up, w_ffn_down, g_final):
    b, s, d = x.shape
    depth = w_in.shape[0]
    tabs = rope_tables(s)
    down_chunks = 2 if w_ffn_down.shape[1] % (2 * LANES) == 0 else 1
    outs = []
    for bi in range(b):
        xr = x.reshape(s, d) if b == 1 else x[bi]
        xg, ssq = norm_prep(xr, g_mix[0])
        for l in range(depth):
            lam_init = 0.8 - 0.6 * math.exp(-0.3 * l)
            pad = lambda v: jnp.pad(v.astype(F32), (0, LANES - DIFF_QK_DIM))
            lam_params = jnp.stack(
                [pad(lam_q1[l]), pad(lam_k1[l]), pad(lam_q2[l]), pad(lam_k2[l]),
                 jnp.full((LANES,), lam_init, F32)]
                + [jnp.zeros((LANES,), F32)] * 3)
            proj = in_proj(xg, ssq, w_in, l, tabs)
            o_a = diff_attention(proj, lam_params, g_subln[l])
            o_b = moba_attention(proj)
            o_c = dilated_mixture(proj)
            z = gate_mix(xg, ssq, o_a, o_b, o_c, w_gate, b_gate, w_branch, l)
            xr, xg, ssq = residual_matmul(z, w_out, l, xr, g_ffn[l], (2048, 1024, 512, 256), (256, 128))
            u = swiglu_up(xg, ssq, w_ffn_gate, w_ffn_up, l)
            g_next = g_mix[l + 1] if l + 1 < depth else g_final
            xr, xg, ssq = residual_matmul(u, w_ffn_down, l, xr, g_next, (1024, 512, 256), (256, 128),
                                          down_chunks)
        outs.append(rmsnorm(xr, g_final, NORM_EPS, x.dtype))
    return outs[0].reshape(b, s, d) if b == 1 else jnp.stack(outs, axis=0)
```

```python
import functools
import math

import jax
import jax.numpy as jnp
from jax import lax
from jax.experimental import pallas as pl
from jax.experimental.pallas import tpu as pltpu

HEAD_DIM = 128
DIFF_HEADS = 4
DIFF_QK_DIM = HEAD_DIM // 2
MOBA_HEADS = 4
MOBA_BLOCK = 256
MOBA_TOP_K = 3
DIL_PATTERNS = ((128, 1), (512, 4), (2048, 16))
DIL_HEADS = 4
N_BRANCH = 3
BRANCH_WIDTH = 512
ROPE_THETA = 10000.0
NORM_EPS = 1e-6
SUBLN_EPS = 1e-5
NEG_INF = -1e30
LOG2_E = math.log2(math.e)

N_DIL = len(DIL_PATTERNS)
HEADS = 4
SLAB_AQ, SLAB_AK, SLAB_AV = 0, 4, 8
SLAB_BQ, SLAB_BK, SLAB_BV = 12, 16, 20
SLAB_CQ, SLAB_CK, SLAB_CV = 24, 36, 48
N_SLABS = 60
PROJ_TILE = 512

LANES = 128
BF16_SUBLANES = 16
VMEM_LIMIT_BYTES = 58 * 1024 * 1024

F32 = jnp.float32
BF16 = jnp.bfloat16


def _params(sem, vmem=VMEM_LIMIT_BYTES):
    return pltpu.CompilerParams(dimension_semantics=sem, vmem_limit_bytes=vmem)


def _tile(n, prefs):
    for p in prefs:
        if n % p == 0:
            return p
    return n


def _resident(block_shape, index_map):
    return pl.BlockSpec(block_shape, index_map, pipeline_mode=pl.Buffered(1))


def _lane_folded_sumsq(x):
    sq = x * x
    out = sq[:, 0:LANES]
    for c in range(1, x.shape[1] // LANES):
        out = out + sq[:, c * LANES:(c + 1) * LANES]
    return out


def _rstd(ssq, d, eps):
    return lax.rsqrt(jnp.sum(ssq, axis=1, keepdims=True) * (1.0 / d) + eps)


def _norm_prep_kernel(x_ref, g_ref, xg_ref, ssq_ref):
    x = x_ref[...]
    xg_ref[...] = (x * g_ref[...]).astype(BF16)
    ssq_ref[...] = _lane_folded_sumsq(x)


def norm_prep(x, g):
    s, d = x.shape
    tm = _tile(s, (256, 128, 8))
    return pl.pallas_call(
        _norm_prep_kernel,
        out_shape=(jax.ShapeDtypeStruct((s, d), BF16), jax.ShapeDtypeStruct((s, LANES), F32)),
        grid=(s // tm,),
        in_specs=[pl.BlockSpec((tm, d), lambda i: (i, 0)), pl.BlockSpec((1, d), lambda i: (0, 0))],
        out_specs=(pl.BlockSpec((tm, d), lambda i: (i, 0)), pl.BlockSpec((tm, LANES), lambda i: (i, 0))),
        compiler_params=_params(("parallel",)),
        name="norm_prep",
    )(x, g.reshape(1, d))


def _rmsnorm_kernel(x_ref, g_ref, o_ref, *, eps):
    x = x_ref[...]
    ms = jnp.mean(x * x, axis=-1, keepdims=True)
    o_ref[...] = (x * lax.rsqrt(ms + eps) * g_ref[...]).astype(o_ref.dtype)


def rmsnorm(x, g, eps, out_dtype):
    s, d = x.shape
    tm = _tile(s, (256, 128, 8))
    return pl.pallas_call(
        functools.partial(_rmsnorm_kernel, eps=eps),
        out_shape=jax.ShapeDtypeStruct((s, d), out_dtype),
        grid=(s // tm,),
        in_specs=[pl.BlockSpec((tm, d), lambda i: (i, 0)),
                  pl.BlockSpec((1, d), lambda i: (0, 0))],
        out_specs=pl.BlockSpec((tm, d), lambda i: (i, 0)),
        compiler_params=_params(("parallel",)),
        name="rmsnorm",
    )(x, g.reshape(1, d))


KIND_PLAIN, KIND_ROPE128_Q, KIND_ROPE128_K, KIND_ROPE64_Q, KIND_ROPE64_K = range(5)
_TILE_KINDS = (KIND_ROPE64_Q, KIND_ROPE64_K, KIND_PLAIN,
               KIND_ROPE128_Q, KIND_ROPE128_K, KIND_PLAIN,
               KIND_ROPE128_Q, KIND_ROPE128_Q, KIND_ROPE128_Q,
               KIND_ROPE128_K, KIND_ROPE128_K, KIND_ROPE128_K,
               KIND_PLAIN, KIND_PLAIN, KIND_PLAIN)
IN_PROJ_ROW_PARTS = 8


def _tile_kind(j):
    kind = jnp.int32(_TILE_KINDS[-1])
    for t in range(len(_TILE_KINDS) - 2, -1, -1):
        kind = jnp.where(j == t, _TILE_KINDS[t], kind)
    return kind


def _in_proj_kernel(xg_ref, ssq_ref, w_ref, ta_ref, tb_ref, tc_ref, td_ref, o_ref):
    d = xg_ref.shape[1]
    w = w_ref[...].astype(BF16)
    part = xg_ref.shape[0] // IN_PROJ_ROW_PARTS
    for r in range(IN_PROJ_ROW_PARTS):
        rows = slice(r * part, (r + 1) * part)
        acc = jnp.dot(xg_ref[rows, :], w, preferred_element_type=F32)
        acc = acc * _rstd(ssq_ref[rows, :], d, NORM_EPS)
        ta, tb, tc, td = ta_ref[rows, :], tb_ref[rows, :], tc_ref[rows, :], td_ref[rows, :]
        for c in range(PROJ_TILE // LANES):
            xs = acc[:, c * LANES:(c + 1) * LANES]
            y = (xs * ta + pltpu.roll(xs, 64, 1) * tb
                 + pltpu.roll(xs, 32, 1) * tc + pltpu.roll(xs, 96, 1) * td)
            o_ref[c, rows, :] = y.astype(o_ref.dtype)


def in_proj(xg, ssq, w_all, layer, tabs):
    s, d = xg.shape
    n = w_all.shape[2]
    assert n // PROJ_TILE == len(_TILE_KINDS)
    tm = _tile(s, (2048, 1024, 512, 256))
    tab_spec = pl.BlockSpec((None, tm, LANES), lambda i, j: (_tile_kind(j), i, 0))
    return pl.pallas_call(
        _in_proj_kernel,
        out_shape=jax.ShapeDtypeStruct((n // LANES, s, LANES), BF16),
        grid=(s // tm, n // PROJ_TILE),
        in_specs=[_resident((tm, d), lambda i, j: (i, 0)),
                  _resident((tm, LANES), lambda i, j: (i, 0)),
                  pl.BlockSpec((None, d, PROJ_TILE), lambda i, j: (layer, 0, j)),
                  tab_spec, tab_spec, tab_spec, tab_spec],
        out_specs=pl.BlockSpec((PROJ_TILE // LANES, tm, LANES), lambda i, j: (j, i, 0)),
        compiler_params=_params(("parallel", "arbitrary")),
        name="in_proj",
    )(xg, ssq, w_all, *tabs)


def rope_tables(seq):
    def angles(dim):
        inv = ROPE_THETA ** (-jnp.arange(0, dim, 2, dtype=F32) / dim)
        ang = jnp.arange(seq, dtype=F32)[:, None] * inv[None, :]
        reps = LANES // dim
        cos = jnp.tile(jnp.concatenate([jnp.cos(ang)] * 2, axis=1), (1, reps))
        sin = jnp.tile(jnp.concatenate([-jnp.sin(ang), jnp.sin(ang)], axis=1), (1, reps))
        return cos, sin
    c128, s128 = angles(HEAD_DIM)
    c64, s64 = angles(DIFF_QK_DIM)
    zero, one = jnp.zeros((seq, LANES), F32), jnp.ones((seq, LANES), F32)
    low = (jnp.arange(LANES) & (DIFF_QK_DIM // 2)) == 0
    s64_up, s64_down = jnp.where(low, s64, 0.0), jnp.where(low, 0.0, s64)
    q128, q64 = HEAD_DIM ** -0.5 * LOG2_E, DIFF_QK_DIM ** -0.5 * LOG2_E
    kinds = {KIND_PLAIN: (one, zero, zero, zero),
             KIND_ROPE128_Q: (c128 * q128, s128 * q128, zero, zero),
             KIND_ROPE128_K: (c128, s128, zero, zero),
             KIND_ROPE64_Q: (c64 * q64, zero, s64_down * q64, s64_up * q64),
             KIND_ROPE64_K: (c64, zero, s64_down, s64_up)}
    return tuple(jnp.stack([kinds[k][t] for k in range(len(kinds))]) for t in range(4))


VT_ROWS = HEAD_DIM + BF16_SUBLANES


def _transpose_bf16(x):
    return x.astype(F32).T.astype(BF16)


def _dot_nt(a, b):
    return lax.dot_general(a, b, (((1,), (1,)), ((), ())), preferred_element_type=F32)


def _fill_v_transposed(v_ref, vt_ref, chunk):
    seq = v_ref.shape[1]
    for c in range(seq // chunk):
        vt_ref[0:HEAD_DIM, c * chunk:(c + 1) * chunk] = _transpose_bf16(
            v_ref[0, c * chunk:(c + 1) * chunk, :])
    vt_ref[HEAD_DIM:VT_ROWS, :] = jnp.ones((BF16_SUBLANES, seq), BF16)


def _softmax_step(st, vt, carry):
    m_old, acc = carry
    m_new = jnp.maximum(m_old, jnp.max(st, axis=0, keepdims=True))
    alpha = jnp.exp2(m_old - m_new)
    p = jnp.exp2(st - m_new).astype(BF16)
    return m_new, alpha * acc + jnp.dot(vt, p, preferred_element_type=F32)


def _init_carry(nq):
    return jnp.full((1, nq), NEG_INF, F32), jnp.zeros((VT_ROWS, nq), F32)


def _normalized(carry):
    acc = carry[1]
    return acc[0:HEAD_DIM, :] / acc[HEAD_DIM:HEAD_DIM + 1, :]


TILE_UNROLL = 4


def _pipelined_attention(n_past, granule, past_tile, tail_tiles, scores_into, st_ref, carry):
    def sequence(first, tiles, c, issue_after_last):
        for u, tile in enumerate(tiles):
            if u + 1 < len(tiles) or issue_after_last:
                scores_into((u + 1) % 2, first + u + 1)
            c = _softmax_step(*tile(first + u, st_ref[u % 2]), c)
        return c

    assert TILE_UNROLL % 2 == 0 and TILE_UNROLL % granule == 0
    scores_into(0, 0)
    carry = lax.fori_loop(
        0, n_past // TILE_UNROLL,
        lambda t, c: sequence(TILE_UNROLL * t, [past_tile] * TILE_UNROLL, c, True), carry)
    first = (n_past // TILE_UNROLL) * TILE_UNROLL
    leftovers = range(0, TILE_UNROLL, granule)
    return lax.switch((n_past % TILE_UNROLL) // granule,
                      [lambda c, r=r: sequence(first, [past_tile] * r + list(tail_tiles), c, False)
                       for r in leftovers], carry)


def _diff_attn_kernel(q_ref, k_ref, v_ref, lam_ref, gsub_ref, o_ref, vt_ref, qs_ref, st_ref, *, tq):
    i = pl.program_id(1)

    @pl.when(i == 0)
    def _():
        _fill_v_transposed(v_ref, vt_ref, tq)

    q = q_ref[0]
    lane = lax.broadcasted_iota(jnp.int32, q.shape, 1)
    zero = jnp.zeros_like(q)
    qs_ref[0:tq, :] = jnp.where(lane < DIFF_QK_DIM, q, zero)
    qs_ref[tq:2 * tq, :] = jnp.where(lane >= DIFF_QK_DIM, q, zero)

    def scores_into(slot, j):
        start = pl.multiple_of(j * tq, tq)
        st_ref[slot] = _dot_nt(k_ref[0, pl.ds(start, tq), :], qs_ref[...])

    def values(j):
        return vt_ref[:, pl.ds(pl.multiple_of(j * tq, tq), tq)]

    def past(j, st):
        return st, values(j)

    def diagonal(j, st):
        krow = lax.broadcasted_iota(jnp.int32, st.shape, 0)
        qcol = lax.broadcasted_iota(jnp.int32, st.shape, 1)
        qcol = jnp.where(qcol >= tq, qcol - tq, qcol)
        return jnp.where(krow <= qcol, st, NEG_INF), values(j)

    carry = _pipelined_attention(i, 1, past, [diagonal], scores_into, st_ref, _init_carry(2 * tq))

    lp = lam_ref[...]
    lam_init = lp[4:5, 0:1]
    lam = (jnp.exp(jnp.sum(lp[0:1] * lp[1:2], axis=1, keepdims=True))
           - jnp.exp(jnp.sum(lp[2:3] * lp[3:4], axis=1, keepdims=True)) + lam_init)
    o = _normalized(carry)
    od = o[:, 0:tq] - lam * o[:, tq:2 * tq]
    ms = jnp.mean(od * od, axis=0, keepdims=True)
    y = od * lax.rsqrt(ms + SUBLN_EPS) * gsub_ref[...] * (1.0 - lam_init)
    o_ref[...] = y.T.astype(o_ref.dtype)


def diff_attention(proj, lam_params, g_sub):
    _, s, _ = proj.shape
    tq = _tile(s, (512, 256, 128))
    return pl.pallas_call(
        functools.partial(_diff_attn_kernel, tq=tq),
        out_shape=jax.ShapeDtypeStruct((s, BRANCH_WIDTH), BF16),
        grid=(HEADS, s // tq),
        in_specs=[pl.BlockSpec((1, tq, LANES), lambda h, i: (SLAB_AQ + h, i, 0)),
                  pl.BlockSpec((1, s, LANES), lambda h, i: (SLAB_AK + h, 0, 0)),
                  pl.BlockSpec((1, s, LANES), lambda h, i: (SLAB_AV + h, 0, 0)),
                  pl.BlockSpec((8, LANES), lambda h, i: (0, 0)),
                  pl.BlockSpec((HEAD_DIM, 1), lambda h, i: (0, 0))],
        out_specs=pl.BlockSpec((tq, LANES), lambda h, i: (i, h)),
        scratch_shapes=[pltpu.VMEM((VT_ROWS, s), BF16),
                        pltpu.VMEM((2 * tq, LANES), BF16),
                        pltpu.VMEM((2, tq, 2 * tq), F32)],
        compiler_params=_params(("arbitrary", "arbitrary")),
        name="diff_attn",
    )(proj, proj, proj, lam_params, g_sub.reshape(HEAD_DIM, 1))


def _moba_attn_kernel(q_ref, k_ref, v_ref, o_ref, vt_ref, kmean_ref, bias_ref, st_ref, *, tq, tk, nb):
    i = pl.program_id(1)
    blk = MOBA_BLOCK
    blk_shift = blk.bit_length() - 1
    blk_per_tile = tk // blk
    own_tiles = tq // tk

    @pl.when(i == 0)
    def _():
        _fill_v_transposed(v_ref, vt_ref, tk)
        for n in range(nb):
            kmean_ref[n:n + 1, :] = jnp.mean(
                k_ref[0, n * blk:(n + 1) * blk, :].astype(F32), axis=0, keepdims=True)

    q = q_ref[0]
    km = kmean_ref[...]
    km_hi = km.astype(BF16)
    km_lo = (km - km_hi.astype(F32)).astype(BF16)
    gate = _dot_nt(km_hi, q) + _dot_nt(km_lo, q)
    bidx = lax.broadcasted_iota(jnp.int32, gate.shape, 0)
    qpos = i * tq + lax.broadcasted_iota(jnp.int32, gate.shape, 1)
    own = jnp.right_shift(qpos, blk_shift)
    past_blk = bidx < own
    gate = jnp.where(past_blk, gate, NEG_INF)
    rank = jnp.zeros(gate.shape, jnp.int32)
    for mth in range(nb):
        gm = gate[mth:mth + 1, :]
        beats = jnp.logical_or(gm > gate, jnp.logical_and(gm == gate, bidx > mth))
        rank = rank + beats.astype(jnp.int32)
    selected = jnp.logical_and(rank < MOBA_TOP_K, past_blk)
    bias_ref[...] = jnp.where(selected, 0.0, NEG_INF).astype(F32)

    def scores_into(slot, j):
        start = pl.multiple_of(j * tk, tk)
        st_ref[slot] = _dot_nt(k_ref[0, pl.ds(start, tk), :], q)

    def values(j):
        return vt_ref[:, pl.ds(pl.multiple_of(j * tk, tk), tk)]

    def past(j, st):
        st = jnp.concatenate(
            [st[b * blk:(b + 1) * blk, :] + bias_ref[pl.ds(j * blk_per_tile + b, 1), :]
             for b in range(blk_per_tile)], axis=0)
        return st, values(j)

    own_row = jnp.right_shift(i * tq + lax.broadcasted_iota(jnp.int32, (1, tq), 1), blk_shift)

    def own(j, st):
        kpos = j * tk + lax.broadcasted_iota(jnp.int32, st.shape, 0)
        qp = i * tq + lax.broadcasted_iota(jnp.int32, st.shape, 1)
        st = jnp.where(kpos <= qp, st, NEG_INF)
        parts = []
        for b in range(blk_per_tile):
            n = j * blk_per_tile + b
            row_bias = jnp.where(own_row > n, bias_ref[pl.ds(n, 1), :],
                                 jnp.where(own_row == n, 0.0, NEG_INF))
            parts.append(st[b * blk:(b + 1) * blk, :] + row_bias)
        return jnp.concatenate(parts, axis=0), values(j)

    carry = _pipelined_attention(i * own_tiles, own_tiles, past, [own] * own_tiles, scores_into, st_ref,
                                 _init_carry(tq))

    o_ref[...] = _normalized(carry).T.astype(o_ref.dtype)


def moba_attention(proj):
    _, s, _ = proj.shape
    assert s % MOBA_BLOCK == 0 and MOBA_BLOCK & (MOBA_BLOCK - 1) == 0
    nb = s // MOBA_BLOCK
    tq = _tile(s, (1024, 512, 256))
    tk = _tile(tq, (512, 256))
    return pl.pallas_call(
        functools.partial(_moba_attn_kernel, tq=tq, tk=tk, nb=nb),
        out_shape=jax.ShapeDtypeStruct((s, BRANCH_WIDTH), BF16),
        grid=(HEADS, s // tq),
        in_specs=[pl.BlockSpec((1, tq, LANES), lambda h, i: (SLAB_BQ + h, i, 0)),
                  pl.BlockSpec((1, s, LANES), lambda h, i: (SLAB_BK + h, 0, 0)),
                  pl.BlockSpec((1, s, LANES), lambda h, i: (SLAB_BV + h, 0, 0))],
        out_specs=pl.BlockSpec((tq, LANES), lambda h, i: (i, h)),
        scratch_shapes=[pltpu.VMEM((VT_ROWS, s), BF16),
                        pltpu.VMEM((nb, LANES), F32),
                        pltpu.VMEM((nb, tq), F32),
                        pltpu.VMEM((2, tk, tq), F32)],
        compiler_params=_params(("arbitrary", "arbitrary")),
        name="moba_attn",
    )(proj, proj, proj)


CLASS_UNROLL = 4


def _dilated_kernel(q_ref, k_ref, v_ref, kp_ref, vp_ref, o_ref, lse_ref,
                    qf_ref, kf_ref, vf_ref, kpf_ref, vpf_ref, *, tq, band, dilation):
    i = pl.program_id(1)
    qf_ref[...] = q_ref[0].astype(F32)
    kf_ref[...] = k_ref[0].astype(F32)
    vf_ref[...] = v_ref[0].astype(F32)
    kpf_ref[...] = kp_ref[0].astype(F32)
    vpf_ref[...] = vp_ref[0].astype(F32)

    def one_class(r, carry):
        cls = lambda ref, n: ref[pl.ds(r, n, stride=dilation), :].astype(BF16)
        q = cls(qf_ref, tq)
        k = jnp.concatenate([cls(kpf_ref, band), cls(kf_ref, tq)], axis=0)
        v = jnp.concatenate([cls(vpf_ref, band), cls(vf_ref, tq)], axis=0)
        outs, lses = [], []
        for u in range(tq // band):
            ku = k[u * band:(u + 2) * band, :]
            vu = v[u * band:(u + 2) * band, :]
            sc = _dot_nt(q[u * band:(u + 1) * band, :], ku)
            qi = lax.broadcasted_iota(jnp.int32, sc.shape, 0)
            kj = lax.broadcasted_iota(jnp.int32, sc.shape, 1)
            ok = jnp.logical_and(kj >= qi, kj <= qi + band)
            if u == 0:
                ok = jnp.logical_and(ok, jnp.logical_or(kj >= band, i > 0))
            sc = jnp.where(ok, sc, NEG_INF)
            m = jnp.max(sc, axis=1, keepdims=True)
            e = jnp.exp2(sc - m)
            den = jnp.sum(e, axis=1, keepdims=True)
            outs.append(jnp.dot(e.astype(BF16), vu, preferred_element_type=F32) / den)
            lses.append(jnp.broadcast_to(m + jnp.log2(den), (band, LANES)))
        o_ref[0, pl.ds(r, tq, stride=dilation), :] = jnp.concatenate(outs, axis=0)
        lse_ref[0, pl.ds(r, tq, stride=dilation), :] = jnp.concatenate(lses, axis=0)
        return carry

    unroll = min(dilation, CLASS_UNROLL)

    def classes(t, carry):
        for u in range(unroll):
            one_class(unroll * t + u, carry)
        return carry

    lax.fori_loop(0, dilation // unroll, classes, 0)


def dilated_group(proj, g, window, dilation):
    _, s, _ = proj.shape
    band = window // dilation
    assert band == LANES and s % (band * dilation) == 0
    length = s // dilation
    tq = _tile(length, (max(512, 2048 // dilation), 512, 256, 128))
    per = tq // band
    rows, prev_rows = tq * dilation, band * dilation
    sq, sk, sv = SLAB_CQ + g * HEADS, SLAB_CK + g * HEADS, SLAB_CV + g * HEADS
    cur = lambda base: pl.BlockSpec((1, rows, LANES), lambda h, i: (base + h, i, 0))
    prev = lambda base: pl.BlockSpec(
        (1, prev_rows, LANES), lambda h, i: (base + h, jnp.maximum(i * per - 1, 0), 0))
    out_spec = pl.BlockSpec((1, rows, LANES), lambda h, i: (h, i, 0))
    return pl.pallas_call(
        functools.partial(_dilated_kernel, tq=tq, band=band, dilation=dilation),
        out_shape=(jax.ShapeDtypeStruct((HEADS, s, LANES), F32),
                   jax.ShapeDtypeStruct((HEADS, s, LANES), F32)),
        grid=(HEADS, length // tq),
        in_specs=[cur(sq), cur(sk), cur(sv), prev(sk), prev(sv)],
        out_specs=(out_spec, out_spec),
        scratch_shapes=[pltpu.VMEM((rows, LANES), F32)] * 3 + [pltpu.VMEM((prev_rows, LANES), F32)] * 2,
        compiler_params=_params(("parallel", "parallel")),
        name=f"dilated_d{dilation}",
    )(proj, proj, proj, proj, proj)


def _dil_merge_kernel(*refs):
    o_refs, lse_refs, out_ref = refs[:N_DIL], refs[N_DIL:2 * N_DIL], refs[2 * N_DIL]
    lses = [r[0] for r in lse_refs]
    m = functools.reduce(jnp.maximum, lses)
    es = [jnp.exp2(x - m) for x in lses]
    inv = 1.0 / functools.reduce(lambda a, b: a + b, es)
    acc = None
    for e, o_ref in zip(es, o_refs):
        term = (e * inv) * o_ref[0]
        acc = term if acc is None else acc + term
    out_ref[...] = acc.astype(out_ref.dtype)


def dilated_mixture(proj):
    _, s, _ = proj.shape
    outs, lses = [], []
    for g, (window, dilation) in enumerate(DIL_PATTERNS):
        o, lse = dilated_group(proj, g, window, dilation)
        outs.append(o)
        lses.append(lse)
    tm = _tile(s, (1024, 512, 256))
    spec = pl.BlockSpec((1, tm, LANES), lambda h, i: (h, i, 0))
    return pl.pallas_call(
        _dil_merge_kernel,
        out_shape=jax.ShapeDtypeStruct((s, BRANCH_WIDTH), BF16),
        grid=(HEADS, s // tm),
        in_specs=[spec] * (2 * N_DIL),
        out_specs=pl.BlockSpec((tm, LANES), lambda h, i: (i, h)),
        compiler_params=_params(("parallel", "parallel")),
        name="dilated_merge",
    )(*outs, *lses)


def _gate_mix_kernel(xg_ref, ssq_ref, oa_ref, ob_ref, oc_ref, wg0_ref, wg1_ref, wg2_ref,
                     b0_ref, b1_ref, b2_ref, wb_ref, z_ref):
    xg = xg_ref[...]
    rstd = _rstd(ssq_ref[...], xg.shape[1], NORM_EPS)
    z = None
    for n, (o_ref, wg_ref, b_ref) in enumerate(((oa_ref, wg0_ref, b0_ref),
                                                (ob_ref, wg1_ref, b1_ref),
                                                (oc_ref, wg2_ref, b2_ref))):
        pre = jnp.dot(xg, wg_ref[...].astype(BF16), preferred_element_type=F32)
        gate = jax.nn.sigmoid(pre * rstd + b_ref[...])
        y = jnp.dot(o_ref[...], wb_ref[n].astype(BF16), preferred_element_type=F32)
        z = gate * y if z is None else z + gate * y
    z_ref[...] = z.astype(z_ref.dtype)


def gate_mix(xg, ssq, o_a, o_b, o_c, w_gate_all, b_gate_all, w_branch_all, layer):
    s, d = xg.shape
    assert N_BRANCH == 3
    tm = _tile(s, (1024, 512, 256))
    tn = _tile(d, (256, 128))
    nj = d // tn
    o_spec = _resident((tm, BRANCH_WIDTH), lambda i, j: (i, 0))
    wg_spec = lambda n: pl.BlockSpec((None, d, tn), lambda i, j: (layer, 0, n * nj + j))
    b_spec = lambda n: pl.BlockSpec((None, 1, tn), lambda i, j: (layer, 0, n * nj + j))
    b3d = b_gate_all.reshape(b_gate_all.shape[0], 1, N_BRANCH * d)
    return pl.pallas_call(
        _gate_mix_kernel,
        out_shape=jax.ShapeDtypeStruct((s, d), BF16),
        grid=(s // tm, nj),
        in_specs=[_resident((tm, d), lambda i, j: (i, 0)),
                  _resident((tm, LANES), lambda i, j: (i, 0)),
                  o_spec, o_spec, o_spec,
                  wg_spec(0), wg_spec(1), wg_spec(2), b_spec(0), b_spec(1), b_spec(2),
                  pl.BlockSpec((None, N_BRANCH, BRANCH_WIDTH, tn), lambda i, j: (layer, 0, 0, j))],
        out_specs=pl.BlockSpec((tm, tn), lambda i, j: (i, j)),
        compiler_params=_params(("parallel", "arbitrary")),
        name="gate_mix",
    )(xg, ssq, o_a, o_b, o_c, w_gate_all, w_gate_all, w_gate_all,
      b3d, b3d, b3d, w_branch_all)


RESIDUAL_ROW_PARTS = 2


def _residual_matmul_kernel(a_ref, w_ref, x_ref, g_ref, o_ref, xg_ref, ssq_ref, *, k_chunks):
    @pl.when(pl.program_id(1) == 0)
    def _():
        ssq_ref[...] = jnp.zeros(ssq_ref.shape, F32)

    kc = a_ref.shape[1] // k_chunks
    ws = [w_ref[c * kc:(c + 1) * kc, :].astype(BF16) for c in range(k_chunks)]
    part = a_ref.shape[0] // RESIDUAL_ROW_PARTS
    for r in range(RESIDUAL_ROW_PARTS):
        rows = slice(r * part, (r + 1) * part)
        acc = x_ref[rows, :]
        for c in range(k_chunks):
            acc = acc + jnp.dot(a_ref[rows, c * kc:(c + 1) * kc], ws[c], preferred_element_type=F32)
        o_ref[rows, :] = acc
        xg_ref[rows, :] = (acc * g_ref[...]).astype(BF16)
        ssq_ref[rows, :] += _lane_folded_sumsq(acc)


def residual_matmul(a, w_all, layer, x, g_next, tm_prefs, tn_prefs, k_chunks=1):
    s, k = a.shape
    d = w_all.shape[2]
    tm = _tile(s, tm_prefs)
    tn = _tile(d, tn_prefs)
    assert k % (k_chunks * LANES) == 0
    return pl.pallas_call(
        functools.partial(_residual_matmul_kernel, k_chunks=k_chunks),
        out_shape=(jax.ShapeDtypeStruct((s, d), F32), jax.ShapeDtypeStruct((s, d), BF16),
                   jax.ShapeDtypeStruct((s, LANES), F32)),
        grid=(s // tm, d // tn),
        in_specs=[_resident((tm, k), lambda i, j: (i, 0)),
                  pl.BlockSpec((None, k, tn), lambda i, j: (layer, 0, j)),
                  pl.BlockSpec((tm, tn), lambda i, j: (i, j)),
                  pl.BlockSpec((1, tn), lambda i, j: (0, j))],
        out_specs=(pl.BlockSpec((tm, tn), lambda i, j: (i, j)),
                   pl.BlockSpec((tm, tn), lambda i, j: (i, j)),
                   pl.BlockSpec((tm, LANES), lambda i, j: (i, 0))),
        input_output_aliases={2: 0},
        compiler_params=_params(("parallel", "arbitrary")),
        name="residual_matmul",
    )(a, w_all, x, g_next.reshape(1, d))


SWIGLU_ROW_PARTS = 4


def _swiglu_kernel(xg_ref, ssq_ref, wg_ref, wu_ref, o_ref):
    d = xg_ref.shape[1]
    wg = wg_ref[...].astype(BF16)
    wu = wu_ref[...].astype(BF16)
    part = xg_ref.shape[0] // SWIGLU_ROW_PARTS
    for r in range(SWIGLU_ROW_PARTS):
        rows = slice(r * part, (r + 1) * part)
        xg = xg_ref[rows, :]
        rstd = _rstd(ssq_ref[rows, :], d, NORM_EPS)
        g = jnp.dot(xg, wg, preferred_element_type=F32) * rstd
        u = jnp.dot(xg, wu, preferred_element_type=F32) * rstd
        o_ref[rows, :] = (g * jax.nn.sigmoid(g) * u).astype(o_ref.dtype)


def swiglu_up(xg, ssq, w_g_all, w_u_all, layer):
    s, d = xg.shape
    f = w_g_all.shape[2]
    tm = _tile(s, (2048, 1024, 512, 256))
    tn = _tile(f, (256, 128))
    w_spec = pl.BlockSpec((None, d, tn), lambda i, j: (layer, 0, j))
    return pl.pallas_call(
        _swiglu_kernel,
        out_shape=jax.ShapeDtypeStruct((s, f), BF16),
        grid=(s // tm, f // tn),
        in_specs=[_resident((tm, d), lambda i, j: (i, 0)),
                  _resident((tm, LANES), lambda i, j: (i, 0)),
                  w_spec, w_spec],
        out_specs=pl.BlockSpec((tm, tn), lambda i, j: (i, j)),
        compiler_params=_params(("parallel", "arbitrary")),
        name="swiglu_up",
    )(xg, ssq, w_g_all, w_u_all)


def kernel(x, g_mix, w_in, lam_q1, lam_k1, lam_q2, lam_k2, g_subln, w_gate, b_gate, w_branch,
           w_out, g_ffn, w_ffn_gate, w_ffn_up, w_ffn_down, g_final):
    b, s, d = x.shape
    depth = w_in.shape[0]
    tabs = rope_tables(s)
    down_chunks = 2 if w_ffn_down.shape[1] % (2 * LANES) == 0 else 1
    outs = []
    for bi in range(b):
        xr = x.reshape(s, d) if b == 1 else x[bi]
        xg, ssq = norm_prep(xr, g_mix[0])
        for l in range(depth):
            lam_init = 0.8 - 0.6 * math.exp(-0.3 * l)
            pad = lambda v: jnp.pad(v.astype(F32), (0, LANES - DIFF_QK_DIM))
            lam_params = jnp.stack(
                [pad(lam_q1[l]), pad(lam_k1[l]), pad(lam_q2[l]), pad(lam_k2[l]),
                 jnp.full((LANES,), lam_init, F32)]
                + [jnp.zeros((LANES,), F32)] * 3)
            proj = in_proj(xg, ssq, w_in, l, tabs)
            o_a = diff_attention(proj, lam_params, g_subln[l])
            o_b = moba_attention(proj)
            o_c = dilated_mixture(proj)
            z = gate_mix(xg, ssq, o_a, o_b, o_c, w_gate, b_gate, w_branch, l)
            xr, xg, ssq = residual_matmul(z, w_out, l, xr, g_ffn[l], (2048, 1024, 512, 256), (256, 128))
            u = swiglu_up(xg, ssq, w_ffn_gate, w_ffn_up, l)
            g_next = g_mix[l + 1] if l + 1 < depth else g_final
            xr, xg, ssq = residual_matmul(u, w_ffn_down, l, xr, g_next, (1024, 512, 256), (256, 128),
                                          down_chunks)
        outs.append(rmsnorm(xr, g_final, NORM_EPS, x.dtype))
    return outs[0].reshape(b, s, d) if b == 1 else jnp.stack(outs, axis=0)
```

```python
import functools
import math

import jax
import jax.numpy as jnp
from jax import lax
from jax.experimental import pallas as pl
from jax.experimental.pallas import tpu as pltpu

HEAD_DIM = 128
DIFF_HEADS = 4
DIFF_QK_DIM = HEAD_DIM // 2
MOBA_HEADS = 4
MOBA_BLOCK = 256
MOBA_TOP_K = 3
DIL_PATTERNS = ((128, 1), (512, 4), (2048, 16))
DIL_HEADS = 4
N_BRANCH = 3
BRANCH_WIDTH = 512
ROPE_THETA = 10000.0
NORM_EPS = 1e-6
SUBLN_EPS = 1e-5
NEG_INF = -1e30
LOG2_E = math.log2(math.e)

HEADS = 4
SLAB_AQ, SLAB_AK, SLAB_AV = 0, 4, 8
SLAB_BQ, SLAB_BK, SLAB_BV = 12, 16, 20
SLAB_CQ, SLAB_CK, SLAB_CV = 24, 36, 48
N_SLABS = 60
PROJ_TILE = 512

LANES = 128
BF16_SUBLANES = 16
VMEM_LIMIT_BYTES = 58 * 1024 * 1024

F32 = jnp.float32
BF16 = jnp.bfloat16


def _params(sem, vmem=VMEM_LIMIT_BYTES):
    return pltpu.CompilerParams(dimension_semantics=sem, vmem_limit_bytes=vmem)


def _tile(n, prefs):
    for p in prefs:
        if n % p == 0:
            return p
    return n


def _resident(block_shape, index_map):
    return pl.BlockSpec(block_shape, index_map, pipeline_mode=pl.Buffered(1))


def _lane_folded_sumsq(x):
    sq = x * x
    out = sq[:, 0:LANES]
    for c in range(1, x.shape[1] // LANES):
        out = out + sq[:, c * LANES:(c + 1) * LANES]
    return out


def _rstd(ssq, d, eps):
    return lax.rsqrt(jnp.sum(ssq, axis=1, keepdims=True) * (1.0 / d) + eps)


def _norm_prep_kernel(x_ref, g_ref, xg_ref, ssq_ref):
    x = x_ref[...]
    xg_ref[...] = (x * g_ref[...]).astype(BF16)
    ssq_ref[...] = _lane_folded_sumsq(x)


def norm_prep(x, g):
    s, d = x.shape
    tm = _tile(s, (256, 128, 8))
    return pl.pallas_call(
        _norm_prep_kernel,
        out_shape=(jax.ShapeDtypeStruct((s, d), BF16), jax.ShapeDtypeStruct((s, LANES), F32)),
        grid=(s // tm,),
        in_specs=[pl.BlockSpec((tm, d), lambda i: (i, 0)), pl.BlockSpec((1, d), lambda i: (0, 0))],
        out_specs=(pl.BlockSpec((tm, d), lambda i: (i, 0)), pl.BlockSpec((tm, LANES), lambda i: (i, 0))),
        compiler_params=_params(("parallel",)),
        name="norm_prep",
    )(x, g.reshape(1, d))


def _rmsnorm_kernel(x_ref, g_ref, o_ref, *, eps):
    x = x_ref[...]
    ms = jnp.mean(x * x, axis=-1, keepdims=True)
    o_ref[...] = (x * lax.rsqrt(ms + eps) * g_ref[...]).astype(o_ref.dtype)


def rmsnorm(x, g, eps, out_dtype):
    s, d = x.shape
    tm = _tile(s, (256, 128, 8))
    return pl.pallas_call(
        functools.partial(_rmsnorm_kernel, eps=eps),
        out_shape=jax.ShapeDtypeStruct((s, d), out_dtype),
        grid=(s // tm,),
        in_specs=[pl.BlockSpec((tm, d), lambda i: (i, 0)),
                  pl.BlockSpec((1, d), lambda i: (0, 0))],
        out_specs=pl.BlockSpec((tm, d), lambda i: (i, 0)),
        compiler_params=_params(("parallel",)),
        name="rmsnorm",
    )(x, g.reshape(1, d))


KIND_PLAIN, KIND_ROPE128_Q, KIND_ROPE128_K, KIND_ROPE64_Q, KIND_ROPE64_K = range(5)
_TILE_KINDS = (KIND_ROPE64_Q, KIND_ROPE64_K, KIND_PLAIN,
               KIND_ROPE128_Q, KIND_ROPE128_K, KIND_PLAIN,
               KIND_ROPE128_Q, KIND_ROPE128_Q, KIND_ROPE128_Q,
               KIND_ROPE128_K, KIND_ROPE128_K, KIND_ROPE128_K,
               KIND_PLAIN, KIND_PLAIN, KIND_PLAIN)
IN_PROJ_ROW_PARTS = 8


def _tile_kind(j):
    kind = jnp.int32(_TILE_KINDS[-1])
    for t in range(len(_TILE_KINDS) - 2, -1, -1):
        kind = jnp.where(j == t, _TILE_KINDS[t], kind)
    return kind


def _in_proj_kernel(xg_ref, ssq_ref, w_ref, ta_ref, tb_ref, tc_ref, td_ref, o_ref):
    d = xg_ref.shape[1]
    w = w_ref[...].astype(BF16)
    part = xg_ref.shape[0] // IN_PROJ_ROW_PARTS
    for r in range(IN_PROJ_ROW_PARTS):
        rows = slice(r * part, (r + 1) * part)
        acc = jnp.dot(xg_ref[rows, :], w, preferred_element_type=F32)
        acc = acc * _rstd(ssq_ref[rows, :], d, NORM_EPS)
        ta, tb, tc, td = ta_ref[rows, :], tb_ref[rows, :], tc_ref[rows, :], td_ref[rows, :]
        for c in range(PROJ_TILE // LANES):
            xs = acc[:, c * LANES:(c + 1) * LANES]
            y = (xs * ta + pltpu.roll(xs, 64, 1) * tb
                 + pltpu.roll(xs, 32, 1) * tc + pltpu.roll(xs, 96, 1) * td)
            o_ref[c, rows, :] = y.astype(o_ref.dtype)


def in_proj(xg, ssq, w_all, layer, tabs):
    s, d = xg.shape
    n = w_all.shape[2]
    assert n // PROJ_TILE == len(_TILE_KINDS)
    tm = _tile(s, (2048, 1024, 512, 256))
    tab_spec = pl.BlockSpec((None, tm, LANES), lambda i, j: (_tile_kind(j), i, 0))
    return pl.pallas_call(
        _in_proj_kernel,
        out_shape=jax.ShapeDtypeStruct((n // LANES, s, LANES), BF16),
        grid=(s // tm, n // PROJ_TILE),
        in_specs=[_resident((tm, d), lambda i, j: (i, 0)),
                  _resident((tm, LANES), lambda i, j: (i, 0)),
                  pl.BlockSpec((None, d, PROJ_TILE), lambda i, j: (layer, 0, j)),
                  tab_spec, tab_spec, tab_spec, tab_spec],
        out_specs=pl.BlockSpec((PROJ_TILE // LANES, tm, LANES), lambda i, j: (j, i, 0)),
        compiler_params=_params(("parallel", "arbitrary")),
        name="in_proj",
    )(xg, ssq, w_all, *tabs)


def rope_tables(seq):
    def angles(dim):
        inv = ROPE_THETA ** (-jnp.arange(0, dim, 2, dtype=F32) / dim)
        ang = jnp.arange(seq, dtype=F32)[:, None] * inv[None, :]
        reps = LANES // dim
        cos = jnp.tile(jnp.concatenate([jnp.cos(ang)] * 2, axis=1), (1, reps))
        sin = jnp.tile(jnp.concatenate([-jnp.sin(ang), jnp.sin(ang)], axis=1), (1, reps))
        return cos, sin
    c128, s128 = angles(HEAD_DIM)
    c64, s64 = angles(DIFF_QK_DIM)
    zero, one = jnp.zeros((seq, LANES), F32), jnp.ones((seq, LANES), F32)
    low = (jnp.arange(LANES) & (DIFF_QK_DIM // 2)) == 0
    s64_up, s64_down = jnp.where(low, s64, 0.0), jnp.where(low, 0.0, s64)
    q128, q64 = HEAD_DIM ** -0.5 * LOG2_E, DIFF_QK_DIM ** -0.5 * LOG2_E
    kinds = {KIND_PLAIN: (one, zero, zero, zero),
             KIND_ROPE128_Q: (c128 * q128, s128 * q128, zero, zero),
             KIND_ROPE128_K: (c128, s128, zero, zero),
             KIND_ROPE64_Q: (c64 * q64, zero, s64_down * q64, s64_up * q64),
             KIND_ROPE64_K: (c64, zero, s64_down, s64_up)}
    return tuple(jnp.stack([kinds[k][t] for k in range(len(kinds))]) for t in range(4))


VT_ROWS = HEAD_DIM + BF16_SUBLANES


def _transpose_bf16(x):
    return x.astype(F32).T.astype(BF16)


def _dot_nt(a, b):
    return lax.dot_general(a, b, (((1,), (1,)), ((), ())), preferred_element_type=F32)


def _fill_v_transposed(v_ref, vt_ref, chunk):
    seq = v_ref.shape[1]
    for c in range(seq // chunk):
        vt_ref[0:HEAD_DIM, c * chunk:(c + 1) * chunk] = _transpose_bf16(
            v_ref[0, c * chunk:(c + 1) * chunk, :])
    vt_ref[HEAD_DIM:VT_ROWS, :] = jnp.ones((BF16_SUBLANES, seq), BF16)


def _softmax_step(st, vt, carry):
    m_old, acc = carry
    m_new = jnp.maximum(m_old, jnp.max(st, axis=0, keepdims=True))
    alpha = jnp.exp2(m_old - m_new)
    p = jnp.exp2(st - m_new).astype(BF16)
    return m_new, alpha * acc + jnp.dot(vt, p, preferred_element_type=F32)


def _init_carry(nq):
    return jnp.full((1, nq), NEG_INF, F32), jnp.zeros((VT_ROWS, nq), F32)


def _normalized(carry):
    acc = carry[1]
    return acc[0:HEAD_DIM, :] / acc[HEAD_DIM:HEAD_DIM + 1, :]


TILE_UNROLL = 4


def _pipelined_attention(n_past, granule, past_tile, tail_tiles, scores_into, st_ref, carry):
    def sequence(first, tiles, c, issue_after_last):
        for u, tile in enumerate(tiles):
            if u + 1 < len(tiles) or issue_after_last:
                scores_into((u + 1) % 2, first + u + 1)
            c = _softmax_step(*tile(first + u, st_ref[u % 2]), c)
        return c

    assert TILE_UNROLL % 2 == 0 and TILE_UNROLL % granule == 0
    scores_into(0, 0)
    carry = lax.fori_loop(
        0, n_past // TILE_UNROLL,
        lambda t, c: sequence(TILE_UNROLL * t, [past_tile] * TILE_UNROLL, c, True), carry)
    first = (n_past // TILE_UNROLL) * TILE_UNROLL
    leftovers = range(0, TILE_UNROLL, granule)
    return lax.switch((n_past % TILE_UNROLL) // granule,
                      [lambda c, r=r: sequence(first, [past_tile] * r + list(tail_tiles), c, False)
                       for r in leftovers], carry)


def _diff_attn_kernel(q_ref, k_ref, v_ref, lam_ref, gsub_ref, o_ref, vt_ref, qs_ref, st_ref, *, tq):
    i = pl.program_id(1)

    @pl.when(i == 0)
    def _():
        _fill_v_transposed(v_ref, vt_ref, tq)

    q = q_ref[0]
    lane = lax.broadcasted_iota(jnp.int32, q.shape, 1)
    zero = jnp.zeros_like(q)
    qs_ref[0:tq, :] = jnp.where(lane < DIFF_QK_DIM, q, zero)
    qs_ref[tq:2 * tq, :] = jnp.where(lane >= DIFF_QK_DIM, q, zero)

    def scores_into(slot, j):
        start = pl.multiple_of(j * tq, tq)
        st_ref[slot] = _dot_nt(k_ref[0, pl.ds(start, tq), :], qs_ref[...])

    def values(j):
        return vt_ref[:, pl.ds(pl.multiple_of(j * tq, tq), tq)]

    def past(j, st):
        return st, values(j)

    def diagonal(j, st):
        krow = lax.broadcasted_iota(jnp.int32, st.shape, 0)
        qcol = lax.broadcasted_iota(jnp.int32, st.shape, 1)
        qcol = jnp.where(qcol >= tq, qcol - tq, qcol)
        return jnp.where(krow <= qcol, st, NEG_INF), values(j)

    carry = _pipelined_attention(i, 1, past, [diagonal], scores_into, st_ref, _init_carry(2 * tq))

    lp = lam_ref[...]
    lam_init = lp[4:5, 0:1]
    lam = (jnp.exp(jnp.sum(lp[0:1] * lp[1:2], axis=1, keepdims=True))
           - jnp.exp(jnp.sum(lp[2:3] * lp[3:4], axis=1, keepdims=True)) + lam_init)
    o = _normalized(carry)
    od = o[:, 0:tq] - lam * o[:, tq:2 * tq]
    ms = jnp.mean(od * od, axis=0, keepdims=True)
    y = od * lax.rsqrt(ms + SUBLN_EPS) * gsub_ref[...] * (1.0 - lam_init)
    o_ref[...] = y.T.astype(o_ref.dtype)


def diff_attention(proj, lam_params, g_sub):
    _, s, _ = proj.shape
    tq = _tile(s, (512, 256, 128))
    return pl.pallas_call(
        functools.partial(_diff_attn_kernel, tq=tq),
        out_shape=jax.ShapeDtypeStruct((s, BRANCH_WIDTH), BF16),
        grid=(HEADS, s // tq),
        in_specs=[pl.BlockSpec((1, tq, LANES), lambda h, i: (SLAB_AQ + h, i, 0)),
                  pl.BlockSpec((1, s, LANES), lambda h, i: (SLAB_AK + h, 0, 0)),
                  pl.BlockSpec((1, s, LANES), lambda h, i: (SLAB_AV + h, 0, 0)),
                  pl.BlockSpec((8, LANES), lambda h, i: (0, 0)),
                  pl.BlockSpec((HEAD_DIM, 1), lambda h, i: (0, 0))],
        out_specs=pl.BlockSpec((tq, LANES), lambda h, i: (i, h)),
        scratch_shapes=[pltpu.VMEM((VT_ROWS, s), BF16),
                        pltpu.VMEM((2 * tq, LANES), BF16),
                        pltpu.VMEM((2, tq, 2 * tq), F32)],
        compiler_params=_params(("arbitrary", "arbitrary")),
        name="diff_attn",
    )(proj, proj, proj, lam_params, g_sub.reshape(HEAD_DIM, 1))


def _moba_attn_kernel(q_ref, k_ref, v_ref, o_ref, vt_ref, kmean_ref, bias_ref, st_ref, *, tq, tk, nb):
    i = pl.program_id(1)
    blk = MOBA_BLOCK
    blk_shift = blk.bit_length() - 1
    blk_per_tile = tk // blk
    own_tiles = tq // tk

    @pl.when(i == 0)
    def _():
        _fill_v_transposed(v_ref, vt_ref, tk)
        for n in range(nb):
            kmean_ref[n:n + 1, :] = jnp.mean(
                k_ref[0, n * blk:(n + 1) * blk, :].astype(F32), axis=0, keepdims=True)

    q = q_ref[0]
    km = kmean_ref[...]
    km_hi = km.astype(BF16)
    km_lo = (km - km_hi.astype(F32)).astype(BF16)
    gate = _dot_nt(km_hi, q) + _dot_nt(km_lo, q)
    bidx = lax.broadcasted_iota(jnp.int32, gate.shape, 0)
    qpos = i * tq + lax.broadcasted_iota(jnp.int32, gate.shape, 1)
    own = jnp.right_shift(qpos, blk_shift)
    past_blk = bidx < own
    gate = jnp.where(past_blk, gate, NEG_INF)
    cand = gate
    selected = jnp.zeros(gate.shape, jnp.bool_)
    for _ in range(MOBA_TOP_K):
        top = jnp.max(cand, axis=0, keepdims=True)
        first = jnp.min(jnp.where(cand == top, bidx, nb), axis=0, keepdims=True)
        pick = bidx == first
        selected = jnp.logical_or(selected, pick)
        cand = jnp.where(pick, NEG_INF, cand)
    selected = jnp.logical_and(selected, past_blk)
    bias_ref[...] = jnp.where(selected, 0.0, NEG_INF).astype(F32)

    def scores_into(slot, j):
        start = pl.multiple_of(j * tk, tk)
        st_ref[slot] = _dot_nt(k_ref[0, pl.ds(start, tk), :], q)

    def values(j):
        return vt_ref[:, pl.ds(pl.multiple_of(j * tk, tk), tk)]

    def past(j, st):
        st = jnp.concatenate(
            [st[b * blk:(b + 1) * blk, :] + bias_ref[pl.ds(j * blk_per_tile + b, 1), :]
             for b in range(blk_per_tile)], axis=0)
        return st, values(j)

    own_row = jnp.right_shift(i * tq + lax.broadcasted_iota(jnp.int32, (1, tq), 1), blk_shift)

    def own(j, st):
        kpos = j * tk + lax.broadcasted_iota(jnp.int32, st.shape, 0)
        qp = i * tq + lax.broadcasted_iota(jnp.int32, st.shape, 1)
        st = jnp.where(kpos <= qp, st, NEG_INF)
        parts = []
        for b in range(blk_per_tile):
            n = j * blk_per_tile + b
            row_bias = jnp.where(own_row > n, bias_ref[pl.ds(n, 1), :],
                                 jnp.where(own_row == n, 0.0, NEG_INF))
            parts.append(st[b * blk:(b + 1) * blk, :] + row_bias)
        return jnp.concatenate(parts, axis=0), values(j)

    carry = _pipelined_attention(i * own_tiles, own_tiles, past, [own] * own_tiles, scores_into, st_ref,
                                 _init_carry(tq))

    o_ref[...] = _normalized(carry).T.astype(o_ref.dtype)


def moba_attention(proj):
    _, s, _ = proj.shape
    assert s % MOBA_BLOCK == 0 and MOBA_BLOCK & (MOBA_BLOCK - 1) == 0
    nb = s // MOBA_BLOCK
    tq = _tile(s, (1024, 512, 256))
    tk = _tile(tq, (512, 256))
    return pl.pallas_call(
        functools.partial(_moba_attn_kernel, tq=tq, tk=tk, nb=nb),
        out_shape=jax.ShapeDtypeStruct((s, BRANCH_WIDTH), BF16),
        grid=(HEADS, s // tq),
        in_specs=[pl.BlockSpec((1, tq, LANES), lambda h, i: (SLAB_BQ + h, i, 0)),
                  pl.BlockSpec((1, s, LANES), lambda h, i: (SLAB_BK + h, 0, 0)),
                  pl.BlockSpec((1, s, LANES), lambda h, i: (SLAB_BV + h, 0, 0))],
        out_specs=pl.BlockSpec((tq, LANES), lambda h, i: (i, h)),
        scratch_shapes=[pltpu.VMEM((VT_ROWS, s), BF16),
                        pltpu.VMEM((nb, LANES), F32),
                        pltpu.VMEM((nb, tq), F32),
                        pltpu.VMEM((2, tk, tq), F32)],
        compiler_params=_params(("arbitrary", "arbitrary")),
        name="moba_attn",
    )(proj, proj, proj)


CLASS_UNROLL = 4


def _dilated_kernel(q_ref, k_ref, v_ref, kp_ref, vp_ref, *refs, tq, band, dilation, n_merge):
    other_o, other_lse = refs[:n_merge], refs[n_merge:2 * n_merge]
    out_refs = refs[2 * n_merge:len(refs) - 5]
    qf_ref, kf_ref, vf_ref, kpf_ref, vpf_ref = refs[len(refs) - 5:]
    i = pl.program_id(1)
    qf_ref[...] = q_ref[0].astype(F32)
    kf_ref[...] = k_ref[0].astype(F32)
    vf_ref[...] = v_ref[0].astype(F32)
    kpf_ref[...] = kp_ref[0].astype(F32)
    vpf_ref[...] = vp_ref[0].astype(F32)

    def one_class(r, carry):
        cls = lambda ref, n: ref[pl.ds(r, n, stride=dilation), :].astype(BF16)
        q = cls(qf_ref, tq)
        k = jnp.concatenate([cls(kpf_ref, band), cls(kf_ref, tq)], axis=0)
        v = jnp.concatenate([cls(vpf_ref, band), cls(vf_ref, tq)], axis=0)
        outs, lses = [], []
        for u in range(tq // band):
            ku = k[u * band:(u + 2) * band, :]
            vu = v[u * band:(u + 2) * band, :]
            sc = _dot_nt(q[u * band:(u + 1) * band, :], ku)
            qi = lax.broadcasted_iota(jnp.int32, sc.shape, 0)
            kj = lax.broadcasted_iota(jnp.int32, sc.shape, 1)
            ok = jnp.logical_and(kj >= qi, kj <= qi + band)
            if u == 0:
                ok = jnp.logical_and(ok, jnp.logical_or(kj >= band, i > 0))
            sc = jnp.where(ok, sc, NEG_INF)
            m = jnp.max(sc, axis=1, keepdims=True)
            e = jnp.exp2(sc - m)
            den = jnp.sum(e, axis=1, keepdims=True)
            outs.append(jnp.dot(e.astype(BF16), vu, preferred_element_type=F32) / den)
            lses.append(jnp.broadcast_to(m + jnp.log2(den), (band, LANES)))
        o, lse = jnp.concatenate(outs, axis=0), jnp.concatenate(lses, axis=0)
        if n_merge:
            all_o = [o] + [ref[0] for ref in other_o]
            all_lse = [lse] + [ref[0] for ref in other_lse]
            top = functools.reduce(jnp.maximum, all_lse)
            es = [jnp.exp2(x - top) for x in all_lse]
            inv = 1.0 / functools.reduce(lambda a, b: a + b, es)
            mixed = functools.reduce(lambda a, b: a + b, [(e * inv) * x for e, x in zip(es, all_o)])
            out_refs[0][...] = mixed.astype(out_refs[0].dtype)
        else:
            o_ref, lse_ref = out_refs
            o_ref[0, pl.ds(r, tq, stride=dilation), :] = o
            lse_ref[0, pl.ds(r, tq, stride=dilation), :] = lse
        return carry

    unroll = min(dilation, CLASS_UNROLL)

    def classes(t, carry):
        for u in range(unroll):
            one_class(unroll * t + u, carry)
        return carry

    lax.fori_loop(0, dilation // unroll, classes, 0)


def dilated_group(proj, g, window, dilation, merge_with=()):
    _, s, _ = proj.shape
    band = window // dilation
    assert band == LANES and s % (band * dilation) == 0
    assert not merge_with or dilation == 1
    length = s // dilation
    tq = _tile(length, (max(512, 2048 // dilation), 512, 256, 128))
    per = tq // band
    rows, prev_rows = tq * dilation, band * dilation
    sq, sk, sv = SLAB_CQ + g * HEADS, SLAB_CK + g * HEADS, SLAB_CV + g * HEADS
    cur = lambda base: pl.BlockSpec((1, rows, LANES), lambda h, i: (base + h, i, 0))
    prev = lambda base: pl.BlockSpec(
        (1, prev_rows, LANES), lambda h, i: (base + h, jnp.maximum(i * per - 1, 0), 0))
    group_spec = pl.BlockSpec((1, rows, LANES), lambda h, i: (h, i, 0))
    if merge_with:
        out_shape = jax.ShapeDtypeStruct((s, BRANCH_WIDTH), BF16)
        out_specs = pl.BlockSpec((rows, LANES), lambda h, i: (i, h))
    else:
        out_shape = (jax.ShapeDtypeStruct((HEADS, s, LANES), F32),) * 2
        out_specs = (group_spec, group_spec)
    others = [o for o, _ in merge_with] + [lse for _, lse in merge_with]
    return pl.pallas_call(
        functools.partial(_dilated_kernel, tq=tq, band=band, dilation=dilation, n_merge=len(merge_with)),
        out_shape=out_shape,
        grid=(HEADS, length // tq),
        in_specs=[cur(sq), cur(sk), cur(sv), prev(sk), prev(sv)] + [group_spec] * len(others),
        out_specs=out_specs,
        scratch_shapes=[pltpu.VMEM((rows, LANES), F32)] * 3 + [pltpu.VMEM((prev_rows, LANES), F32)] * 2,
        compiler_params=_params(("parallel", "parallel")),
        name=f"dilated_d{dilation}",
    )(proj, proj, proj, proj, proj, *others)


def dilated_mixture(proj):
    dense = [g for g, (_, dilation) in enumerate(DIL_PATTERNS) if dilation == 1]
    assert len(dense) == 1
    others = tuple(dilated_group(proj, g, window, dilation)
                   for g, (window, dilation) in enumerate(DIL_PATTERNS) if dilation != 1)
    return dilated_group(proj, dense[0], *DIL_PATTERNS[dense[0]], merge_with=others)


def _gate_mix_kernel(xg_ref, ssq_ref, oa_ref, ob_ref, oc_ref, wg0_ref, wg1_ref, wg2_ref,
                     b0_ref, b1_ref, b2_ref, wb_ref, z_ref):
    xg = xg_ref[...]
    rstd = _rstd(ssq_ref[...], xg.shape[1], NORM_EPS)
    z = None
    for n, (o_ref, wg_ref, b_ref) in enumerate(((oa_ref, wg0_ref, b0_ref),
                                                (ob_ref, wg1_ref, b1_ref),
                                                (oc_ref, wg2_ref, b2_ref))):
        pre = jnp.dot(xg, wg_ref[...].astype(BF16), preferred_element_type=F32)
        gate = jax.nn.sigmoid(pre * rstd + b_ref[...])
        y = jnp.dot(o_ref[...], wb_ref[n].astype(BF16), preferred_element_type=F32)
        z = gate * y if z is None else z + gate * y
    z_ref[...] = z.astype(z_ref.dtype)


def gate_mix(xg, ssq, o_a, o_b, o_c, w_gate_all, b_gate_all, w_branch_all, layer):
    s, d = xg.shape
    assert N_BRANCH == 3
    tm = _tile(s, (1024, 512, 256))
    tn = _tile(d, (256, 128))
    nj = d // tn
    o_spec = _resident((tm, BRANCH_WIDTH), lambda i, j: (i, 0))
    wg_spec = lambda n: pl.BlockSpec((None, d, tn), lambda i, j: (layer, 0, n * nj + j))
    b_spec = lambda n: pl.BlockSpec((None, 1, tn), lambda i, j: (layer, 0, n * nj + j))
    b3d = b_gate_all.reshape(b_gate_all.shape[0], 1, N_BRANCH * d)
    return pl.pallas_call(
        _gate_mix_kernel,
        out_shape=jax.ShapeDtypeStruct((s, d), BF16),
        grid=(s // tm, nj),
        in_specs=[_resident((tm, d), lambda i, j: (i, 0)),
                  _resident((tm, LANES), lambda i, j: (i, 0)),
                  o_spec, o_spec, o_spec,
                  wg_spec(0), wg_spec(1), wg_spec(2), b_spec(0), b_spec(1), b_spec(2),
                  pl.BlockSpec((None, N_BRANCH, BRANCH_WIDTH, tn), lambda i, j: (layer, 0, 0, j))],
        out_specs=pl.BlockSpec((tm, tn), lambda i, j: (i, j)),
        compiler_params=_params(("parallel", "arbitrary")),
        name="gate_mix",
    )(xg, ssq, o_a, o_b, o_c, w_gate_all, w_gate_all, w_gate_all,
      b3d, b3d, b3d, w_branch_all)


RESIDUAL_ROW_PARTS = 2


def _residual_matmul_kernel(a_ref, w_ref, x_ref, g_ref, o_ref, xg_ref, ssq_ref, *, k_chunks):
    @pl.when(pl.program_id(1) == 0)
    def _():
        ssq_ref[...] = jnp.zeros(ssq_ref.shape, F32)

    kc = a_ref.shape[1] // k_chunks
    ws = [w_ref[c * kc:(c + 1) * kc, :].astype(BF16) for c in range(k_chunks)]
    part = a_ref.shape[0] // RESIDUAL_ROW_PARTS
    for r in range(RESIDUAL_ROW_PARTS):
        rows = slice(r * part, (r + 1) * part)
        acc = x_ref[rows, :]
        for c in range(k_chunks):
            acc = acc + jnp.dot(a_ref[rows, c * kc:(c + 1) * kc], ws[c], preferred_element_type=F32)
        o_ref[rows, :] = acc
        xg_ref[rows, :] = (acc * g_ref[...]).astype(BF16)
        ssq_ref[rows, :] += _lane_folded_sumsq(acc)


def residual_matmul(a, w_all, layer, x, g_next, tm_prefs, tn_prefs, k_chunks=1):
    s, k = a.shape
    d = w_all.shape[2]
    tm = _tile(s, tm_prefs)
    tn = _tile(d, tn_prefs)
    assert k % (k_chunks * LANES) == 0
    return pl.pallas_call(
        functools.partial(_residual_matmul_kernel, k_chunks=k_chunks),
        out_shape=(jax.ShapeDtypeStruct((s, d), F32), jax.ShapeDtypeStruct((s, d), BF16),
                   jax.ShapeDtypeStruct((s, LANES), F32)),
        grid=(s // tm, d // tn),
        in_specs=[_resident((tm, k), lambda i, j: (i, 0)),
                  pl.BlockSpec((None, k, tn), lambda i, j: (layer, 0, j)),
                  pl.BlockSpec((tm, tn), lambda i, j: (i, j)),
                  pl.BlockSpec((1, tn), lambda i, j: (0, j))],
        out_specs=(pl.BlockSpec((tm, tn), lambda i, j: (i, j)),
                   pl.BlockSpec((tm, tn), lambda i, j: (i, j)),
                   pl.BlockSpec((tm, LANES), lambda i, j: (i, 0))),
        input_output_aliases={2: 0},
        compiler_params=_params(("parallel", "arbitrary")),
        name="residual_matmul",
    )(a, w_all, x, g_next.reshape(1, d))


SWIGLU_ROW_PARTS = 4


def _swiglu_kernel(xg_ref, ssq_ref, wg_ref, wu_ref, o_ref):
    d = xg_ref.shape[1]
    wg = wg_ref[...].astype(BF16)
    wu = wu_ref[...].astype(BF16)
    part = xg_ref.shape[0] // SWIGLU_ROW_PARTS
    for r in range(SWIGLU_ROW_PARTS):
        rows = slice(r * part, (r + 1) * part)
        xg = xg_ref[rows, :]
        rstd = _rstd(ssq_ref[rows, :], d, NORM_EPS)
        g = jnp.dot(xg, wg, preferred_element_type=F32) * rstd
        u = jnp.dot(xg, wu, preferred_element_type=F32) * rstd
        o_ref[rows, :] = (g * jax.nn.sigmoid(g) * u).astype(o_ref.dtype)


def swiglu_up(xg, ssq, w_g_all, w_u_all, layer):
    s, d = xg.shape
    f = w_g_all.shape[2]
    tm = _tile(s, (2048, 1024, 512, 256))
    tn = _tile(f, (256, 128))
    w_spec = pl.BlockSpec((None, d, tn), lambda i, j: (layer, 0, j))
    return pl.pallas_call(
        _swiglu_kernel,
        out_shape=jax.ShapeDtypeStruct((s, f), BF16),
        grid=(s // tm, f // tn),
        in_specs=[_resident((tm, d), lambda i, j: (i, 0)),
                  _resident((tm, LANES), lambda i, j: (i, 0)),
                  w_spec, w_spec],
        out_specs=pl.BlockSpec((tm, tn), lambda i, j: (i, j)),
        compiler_params=_params(("parallel", "arbitrary")),
        name="swiglu_up",
    )(xg, ssq, w_g_all, w_u_all)


def kernel(x, g_mix, w_in, lam_q1, lam_k1, lam_q2, lam_k2, g_subln, w_gate, b_gate, w_branch,
           w_out, g_ffn, w_ffn_gate, w_ffn_up, w_ffn_down, g_final):
    b, s, d = x.shape
    depth = w_in.shape[0]
    tabs = rope_tables(s)
    down_chunks = 2 if w_ffn_down.shape[1] % (2 * LANES) == 0 else 1
    outs = []
    for bi in range(b):
        xr = x.reshape(s, d) if b == 1 else x[bi]
        xg, ssq = norm_prep(xr, g_mix[0])
        for l in range(depth):
            lam_init = 0.8 - 0.6 * math.exp(-0.3 * l)
            pad = lambda v: jnp.pad(v.astype(F32), (0, LANES - DIFF_QK_DIM))
            lam_params = jnp.stack(
                [pad(lam_q1[l]), pad(lam_k1[l]), pad(lam_q2[l]), pad(lam_k2[l]),
                 jnp.full((LANES,), lam_init, F32)]
                + [jnp.zeros((LANES,), F32)] * 3)
            proj = in_proj(xg, ssq, w_in, l, tabs)
            o_a = diff_attention(proj, lam_params, g_subln[l])
            o_b = moba_attention(proj)
            o_c = dilated_mixture(proj)
            z = gate_mix(xg, ssq, o_a, o_b, o_c, w_gate, b_gate, w_branch, l)
            xr, xg, ssq = residual_matmul(z, w_out, l, xr, g_ffn[l], (2048, 1024, 512, 256), (256, 128))
            u = swiglu_up(xg, ssq, w_ffn_gate, w_ffn_up, l)
            g_next = g_mix[l + 1] if l + 1 < depth else g_final
            xr, xg, ssq = residual_matmul(u, w_ffn_down, l, xr, g_next, (1024, 512, 256), (256, 128),
                                          down_chunks)
        outs.append(rmsnorm(xr, g_final, NORM_EPS, x.dtype))
    return outs[0].reshape(b, s, d) if b == 1 else jnp.stack(outs, axis=0)
```

```python
import functools
import math

import jax
import jax.numpy as jnp
from jax import lax
from jax.experimental import pallas as pl
from jax.experimental.pallas import tpu as pltpu

HEAD_DIM = 128
DIFF_QK_DIM = HEAD_DIM // 2
MOBA_BLOCK = 256
MOBA_TOP_K = 3
DIL_PATTERNS = ((128, 1), (512, 4), (2048, 16))
N_BRANCH = 3
BRANCH_WIDTH = 512
ROPE_THETA = 10000.0
NORM_EPS = 1e-6
SUBLN_EPS = 1e-5
NEG_INF = -1e30
LOG2_E = math.log2(math.e)

HEADS = 4
SLAB_AQ, SLAB_AK, SLAB_AV = 0, 4, 8
SLAB_BQ, SLAB_BK, SLAB_BV = 12, 16, 20
SLAB_CQ, SLAB_CK, SLAB_CV = 24, 36, 48
PROJ_TILE = 512

LANES = 128
BF16_SUBLANES = 16
VMEM_LIMIT_BYTES = 58 * 1024 * 1024

F32 = jnp.float32
BF16 = jnp.bfloat16


def _params(sem):
    return pltpu.CompilerParams(dimension_semantics=sem, vmem_limit_bytes=VMEM_LIMIT_BYTES)


def _tile(n, prefs):
    for p in prefs:
        if n % p == 0:
            return p
    return n


def _resident(block_shape, index_map):
    return pl.BlockSpec(block_shape, index_map, pipeline_mode=pl.Buffered(1))


PREFETCHED_ACT_TILE_BYTES = 16 * 1024 * 1024


def _activation_spec(block_shape, index_map, itemsize):
    if math.prod(block_shape) * itemsize <= PREFETCHED_ACT_TILE_BYTES:
        return pl.BlockSpec(block_shape, index_map)
    return _resident(block_shape, index_map)


def _lane_folded_sumsq(x):
    sq = x * x
    out = sq[:, 0:LANES]
    for c in range(1, x.shape[1] // LANES):
        out = out + sq[:, c * LANES:(c + 1) * LANES]
    return out


def _rstd(ssq, d, eps):
    return lax.rsqrt(jnp.sum(ssq, axis=1, keepdims=True) * (1.0 / d) + eps)


def _norm_prep_kernel(x_ref, g_ref, xg_ref, ssq_ref):
    x = x_ref[...]
    xg_ref[...] = (x * g_ref[...]).astype(BF16)
    ssq_ref[...] = _lane_folded_sumsq(x)


def norm_prep(x, g):
    s, d = x.shape
    tm = _tile(s, (256, 128, 8))
    return pl.pallas_call(
        _norm_prep_kernel,
        out_shape=(jax.ShapeDtypeStruct((s, d), BF16), jax.ShapeDtypeStruct((s, LANES), F32)),
        grid=(s // tm,),
        in_specs=[pl.BlockSpec((tm, d), lambda i: (i, 0)), pl.BlockSpec((1, d), lambda i: (0, 0))],
        out_specs=(pl.BlockSpec((tm, d), lambda i: (i, 0)), pl.BlockSpec((tm, LANES), lambda i: (i, 0))),
        compiler_params=_params(("parallel",)),
        name="norm_prep",
    )(x, g.reshape(1, d))


def _rmsnorm_kernel(x_ref, g_ref, o_ref, *, eps):
    x = x_ref[...]
    ms = jnp.mean(x * x, axis=-1, keepdims=True)
    o_ref[...] = (x * lax.rsqrt(ms + eps) * g_ref[...]).astype(o_ref.dtype)


def rmsnorm(x, g, eps, out_dtype):
    s, d = x.shape
    tm = _tile(s, (256, 128, 8))
    return pl.pallas_call(
        functools.partial(_rmsnorm_kernel, eps=eps),
        out_shape=jax.ShapeDtypeStruct((s, d), out_dtype),
        grid=(s // tm,),
        in_specs=[pl.BlockSpec((tm, d), lambda i: (i, 0)),
                  pl.BlockSpec((1, d), lambda i: (0, 0))],
        out_specs=pl.BlockSpec((tm, d), lambda i: (i, 0)),
        compiler_params=_params(("parallel",)),
        name="rmsnorm",
    )(x, g.reshape(1, d))


KIND_PLAIN, KIND_ROPE128_Q, KIND_ROPE128_K, KIND_ROPE64_Q, KIND_ROPE64_K = range(5)
_TILE_KINDS = (KIND_ROPE64_Q, KIND_ROPE64_K, KIND_PLAIN,
               KIND_ROPE128_Q, KIND_ROPE128_K, KIND_PLAIN,
               KIND_ROPE128_Q, KIND_ROPE128_Q, KIND_ROPE128_Q,
               KIND_ROPE128_K, KIND_ROPE128_K, KIND_ROPE128_K,
               KIND_PLAIN, KIND_PLAIN, KIND_PLAIN)
IN_PROJ_ROW_PARTS = 8


def _tile_kind(j):
    kind = jnp.int32(_TILE_KINDS[-1])
    for t in range(len(_TILE_KINDS) - 2, -1, -1):
        kind = jnp.where(j == t, _TILE_KINDS[t], kind)
    return kind


def _in_proj_kernel(xg_ref, ssq_ref, w_ref, ta_ref, tb_ref, tc_ref, td_ref, o_ref):
    d = xg_ref.shape[1]
    w = w_ref[...].astype(BF16)
    part = xg_ref.shape[0] // IN_PROJ_ROW_PARTS
    for r in range(IN_PROJ_ROW_PARTS):
        rows = slice(r * part, (r + 1) * part)
        acc = jnp.dot(xg_ref[rows, :], w, preferred_element_type=F32)
        acc = acc * _rstd(ssq_ref[rows, :], d, NORM_EPS)
        ta, tb, tc, td = ta_ref[rows, :], tb_ref[rows, :], tc_ref[rows, :], td_ref[rows, :]
        for c in range(PROJ_TILE // LANES):
            xs = acc[:, c * LANES:(c + 1) * LANES]
            y = (xs * ta + pltpu.roll(xs, 64, 1) * tb
                 + pltpu.roll(xs, 32, 1) * tc + pltpu.roll(xs, 96, 1) * td)
            o_ref[c, rows, :] = y.astype(o_ref.dtype)


def in_proj(xg, ssq, w_all, layer, tabs):
    s, d = xg.shape
    n = w_all.shape[2]
    assert n // PROJ_TILE == len(_TILE_KINDS)
    tm = _tile(s, (2048, 1024, 512, 256))
    tab_spec = pl.BlockSpec((None, tm, LANES), lambda i, j: (_tile_kind(j), i, 0))
    return pl.pallas_call(
        _in_proj_kernel,
        out_shape=jax.ShapeDtypeStruct((n // LANES, s, LANES), BF16),
        grid=(s // tm, n // PROJ_TILE),
        in_specs=[_resident((tm, d), lambda i, j: (i, 0)),
                  _resident((tm, LANES), lambda i, j: (i, 0)),
                  pl.BlockSpec((None, d, PROJ_TILE), lambda i, j: (layer, 0, j)),
                  tab_spec, tab_spec, tab_spec, tab_spec],
        out_specs=pl.BlockSpec((PROJ_TILE // LANES, tm, LANES), lambda i, j: (j, i, 0)),
        compiler_params=_params(("parallel", "arbitrary")),
        name="in_proj",
    )(xg, ssq, w_all, *tabs)


def rope_tables(seq):
    def angles(dim):
        inv = ROPE_THETA ** (-jnp.arange(0, dim, 2, dtype=F32) / dim)
        ang = jnp.arange(seq, dtype=F32)[:, None] * inv[None, :]
        reps = LANES // dim
        cos = jnp.tile(jnp.concatenate([jnp.cos(ang)] * 2, axis=1), (1, reps))
        sin = jnp.tile(jnp.concatenate([-jnp.sin(ang), jnp.sin(ang)], axis=1), (1, reps))
        return cos, sin
    c128, s128 = angles(HEAD_DIM)
    c64, s64 = angles(DIFF_QK_DIM)
    zero, one = jnp.zeros((seq, LANES), F32), jnp.ones((seq, LANES), F32)
    low = (jnp.arange(LANES) & (DIFF_QK_DIM // 2)) == 0
    s64_up, s64_down = jnp.where(low, s64, 0.0), jnp.where(low, 0.0, s64)
    q128, q64 = HEAD_DIM ** -0.5 * LOG2_E, DIFF_QK_DIM ** -0.5 * LOG2_E
    kinds = {KIND_PLAIN: (one, zero, zero, zero),
             KIND_ROPE128_Q: (c128 * q128, s128 * q128, zero, zero),
             KIND_ROPE128_K: (c128, s128, zero, zero),
             KIND_ROPE64_Q: (c64 * q64, zero, s64_down * q64, s64_up * q64),
             KIND_ROPE64_K: (c64, zero, s64_down, s64_up)}
    return tuple(jnp.stack([kinds[k][t] for k in range(len(kinds))]) for t in range(4))


VT_ROWS = HEAD_DIM + BF16_SUBLANES


def _transpose_bf16(x):
    return x.astype(F32).T.astype(BF16)


def _dot_nt(a, b):
    return lax.dot_general(a, b, (((1,), (1,)), ((), ())), preferred_element_type=F32)


def _fill_v_transposed(v_ref, vt_ref, chunk):
    seq = v_ref.shape[1]
    for c in range(seq // chunk):
        vt_ref[0:HEAD_DIM, c * chunk:(c + 1) * chunk] = _transpose_bf16(
            v_ref[0, c * chunk:(c + 1) * chunk, :])
    vt_ref[HEAD_DIM:VT_ROWS, :] = jnp.ones((BF16_SUBLANES, seq), BF16)


def _softmax_step(st, vt, carry):
    m_old, acc = carry
    m_new = jnp.maximum(m_old, jnp.max(st, axis=0, keepdims=True))
    alpha = jnp.exp2(m_old - m_new)
    p = jnp.exp2(st - m_new).astype(BF16)
    return m_new, alpha * acc + jnp.dot(vt, p, preferred_element_type=F32)


def _init_carry(nq):
    return jnp.full((1, nq), NEG_INF, F32), jnp.zeros((VT_ROWS, nq), F32)


def _normalized(carry):
    acc = carry[1]
    return acc[0:HEAD_DIM, :] / acc[HEAD_DIM:HEAD_DIM + 1, :]


TILE_UNROLL = 4


def _pipelined_attention(n_past, granule, past_tile, tail_tiles, scores_into, st_ref, carry):
    def sequence(first, tiles, c, issue_after_last):
        for u, tile in enumerate(tiles):
            if u + 1 < len(tiles) or issue_after_last:
                scores_into((u + 1) % 2, first + u + 1)
            c = _softmax_step(*tile(first + u, st_ref[u % 2]), c)
        return c

    assert TILE_UNROLL % 2 == 0 and TILE_UNROLL % granule == 0
    scores_into(0, 0)
    carry = lax.fori_loop(
        0, n_past // TILE_UNROLL,
        lambda t, c: sequence(TILE_UNROLL * t, [past_tile] * TILE_UNROLL, c, True), carry)
    first = (n_past // TILE_UNROLL) * TILE_UNROLL
    leftovers = range(0, TILE_UNROLL, granule)
    return lax.switch((n_past % TILE_UNROLL) // granule,
                      [lambda c, r=r: sequence(first, [past_tile] * r + list(tail_tiles), c, False)
                       for r in leftovers], carry)


def _diff_attn_kernel(q_ref, k_ref, v_ref, lam_ref, gsub_ref, o_ref, vt_ref, qs_ref, st_ref, *, tq):
    i = pl.program_id(1)

    @pl.when(i == 0)
    def _():
        _fill_v_transposed(v_ref, vt_ref, tq)

    q = q_ref[0]
    lane = lax.broadcasted_iota(jnp.int32, q.shape, 1)
    zero = jnp.zeros_like(q)
    qs_ref[0:tq, :] = jnp.where(lane < DIFF_QK_DIM, q, zero)
    qs_ref[tq:2 * tq, :] = jnp.where(lane >= DIFF_QK_DIM, q, zero)

    def scores_into(slot, j):
        start = pl.multiple_of(j * tq, tq)
        st_ref[slot] = _dot_nt(k_ref[0, pl.ds(start, tq), :], qs_ref[...])

    def values(j):
        return vt_ref[:, pl.ds(pl.multiple_of(j * tq, tq), tq)]

    def past(j, st):
        return st, values(j)

    def diagonal(j, st):
        krow = lax.broadcasted_iota(jnp.int32, st.shape, 0)
        qcol = lax.broadcasted_iota(jnp.int32, st.shape, 1)
        qcol = jnp.where(qcol >= tq, qcol - tq, qcol)
        return jnp.where(krow <= qcol, st, NEG_INF), values(j)

    carry = _pipelined_attention(i, 1, past, [diagonal], scores_into, st_ref, _init_carry(2 * tq))

    lp = lam_ref[...]
    lam_init = lp[4:5, 0:1]
    lam = (jnp.exp(jnp.sum(lp[0:1] * lp[1:2], axis=1, keepdims=True))
           - jnp.exp(jnp.sum(lp[2:3] * lp[3:4], axis=1, keepdims=True)) + lam_init)
    o = _normalized(carry)
    od = o[:, 0:tq] - lam * o[:, tq:2 * tq]
    ms = jnp.mean(od * od, axis=0, keepdims=True)
    y = od * lax.rsqrt(ms + SUBLN_EPS) * gsub_ref[...] * (1.0 - lam_init)
    o_ref[...] = y.T.astype(o_ref.dtype)


def diff_attention(proj, lam_params, g_sub):
    _, s, _ = proj.shape
    tq = _tile(s, (512, 256, 128))
    return pl.pallas_call(
        functools.partial(_diff_attn_kernel, tq=tq),
        out_shape=jax.ShapeDtypeStruct((s, BRANCH_WIDTH), BF16),
        grid=(HEADS, s // tq),
        in_specs=[pl.BlockSpec((1, tq, LANES), lambda h, i: (SLAB_AQ + h, i, 0)),
                  pl.BlockSpec((1, s, LANES), lambda h, i: (SLAB_AK + h, 0, 0)),
                  pl.BlockSpec((1, s, LANES), lambda h, i: (SLAB_AV + h, 0, 0)),
                  pl.BlockSpec((8, LANES), lambda h, i: (0, 0)),
                  pl.BlockSpec((HEAD_DIM, 1), lambda h, i: (0, 0))],
        out_specs=pl.BlockSpec((tq, LANES), lambda h, i: (i, h)),
        scratch_shapes=[pltpu.VMEM((VT_ROWS, s), BF16),
                        pltpu.VMEM((2 * tq, LANES), BF16),
                        pltpu.VMEM((2, tq, 2 * tq), F32)],
        compiler_params=_params(("arbitrary", "arbitrary")),
        name="diff_attn",
    )(proj, proj, proj, lam_params, g_sub.reshape(HEAD_DIM, 1))


def _moba_attn_kernel(q_ref, k_ref, v_ref, o_ref, vt_ref, kmean_ref, bias_ref, st_ref, *, tq, tk, nb):
    i = pl.program_id(1)
    blk = MOBA_BLOCK
    blk_shift = blk.bit_length() - 1
    blk_per_tile = tk // blk
    own_tiles = tq // tk

    @pl.when(i == 0)
    def _():
        _fill_v_transposed(v_ref, vt_ref, tk)
        for n in range(nb):
            kmean_ref[n:n + 1, :] = jnp.mean(
                k_ref[0, n * blk:(n + 1) * blk, :].astype(F32), axis=0, keepdims=True)

    q = q_ref[0]
    km = kmean_ref[...]
    km_hi = km.astype(BF16)
    km_lo = (km - km_hi.astype(F32)).astype(BF16)
    gate = _dot_nt(km_hi, q) + _dot_nt(km_lo, q)
    bidx = lax.broadcasted_iota(jnp.int32, gate.shape, 0)
    qpos = i * tq + lax.broadcasted_iota(jnp.int32, gate.shape, 1)
    own = jnp.right_shift(qpos, blk_shift)
    past_blk = bidx < own
    gate = jnp.where(past_blk, gate, NEG_INF)
    cand = gate
    selected = jnp.zeros(gate.shape, jnp.bool_)
    for _ in range(MOBA_TOP_K):
        top = jnp.max(cand, axis=0, keepdims=True)
        first = jnp.min(jnp.where(cand == top, bidx, nb), axis=0, keepdims=True)
        pick = bidx == first
        selected = jnp.logical_or(selected, pick)
        cand = jnp.where(pick, NEG_INF, cand)
    selected = jnp.logical_and(selected, past_blk)
    bias_ref[...] = jnp.where(selected, 0.0, NEG_INF).astype(F32)

    def scores_into(slot, j):
        start = pl.multiple_of(j * tk, tk)
        st_ref[slot] = _dot_nt(k_ref[0, pl.ds(start, tk), :], q)

    def values(j):
        return vt_ref[:, pl.ds(pl.multiple_of(j * tk, tk), tk)]

    def past(j, st):
        st = jnp.concatenate(
            [st[b * blk:(b + 1) * blk, :] + bias_ref[pl.ds(j * blk_per_tile + b, 1), :]
             for b in range(blk_per_tile)], axis=0)
        return st, values(j)

    own_row = jnp.right_shift(i * tq + lax.broadcasted_iota(jnp.int32, (1, tq), 1), blk_shift)

    def own(j, st):
        kpos = j * tk + lax.broadcasted_iota(jnp.int32, st.shape, 0)
        qp = i * tq + lax.broadcasted_iota(jnp.int32, st.shape, 1)
        st = jnp.where(kpos <= qp, st, NEG_INF)
        parts = []
        for b in range(blk_per_tile):
            n = j * blk_per_tile + b
            row_bias = jnp.where(own_row > n, bias_ref[pl.ds(n, 1), :],
                                 jnp.where(own_row == n, 0.0, NEG_INF))
            parts.append(st[b * blk:(b + 1) * blk, :] + row_bias)
        return jnp.concatenate(parts, axis=0), values(j)

    carry = _pipelined_attention(i * own_tiles, own_tiles, past, [own] * own_tiles, scores_into, st_ref,
                                 _init_carry(tq))

    o_ref[...] = _normalized(carry).T.astype(o_ref.dtype)


def moba_attention(proj):
    _, s, _ = proj.shape
    assert s % MOBA_BLOCK == 0 and MOBA_BLOCK & (MOBA_BLOCK - 1) == 0
    nb = s // MOBA_BLOCK
    tq = _tile(s, (1024, 512, 256))
    tk = _tile(tq, (512, 256))
    return pl.pallas_call(
        functools.partial(_moba_attn_kernel, tq=tq, tk=tk, nb=nb),
        out_shape=jax.ShapeDtypeStruct((s, BRANCH_WIDTH), BF16),
        grid=(HEADS, s // tq),
        in_specs=[pl.BlockSpec((1, tq, LANES), lambda h, i: (SLAB_BQ + h, i, 0)),
                  pl.BlockSpec((1, s, LANES), lambda h, i: (SLAB_BK + h, 0, 0)),
                  pl.BlockSpec((1, s, LANES), lambda h, i: (SLAB_BV + h, 0, 0))],
        out_specs=pl.BlockSpec((tq, LANES), lambda h, i: (i, h)),
        scratch_shapes=[pltpu.VMEM((VT_ROWS, s), BF16),
                        pltpu.VMEM((nb, LANES), F32),
                        pltpu.VMEM((nb, tq), F32),
                        pltpu.VMEM((2, tk, tq), F32)],
        compiler_params=_params(("arbitrary", "arbitrary")),
        name="moba_attn",
    )(proj, proj, proj)


CLASS_UNROLL = 4


def _dilated_kernel(q_ref, k_ref, v_ref, kp_ref, vp_ref, *refs, tq, band, dilation, n_merge):
    other_o, other_lse = refs[:n_merge], refs[n_merge:2 * n_merge]
    out_refs = refs[2 * n_merge:len(refs) - 5]
    qf_ref, kf_ref, vf_ref, kpf_ref, vpf_ref = refs[len(refs) - 5:]
    i = pl.program_id(1)
    qf_ref[...] = q_ref[0].astype(F32)
    kf_ref[...] = k_ref[0].astype(F32)
    vf_ref[...] = v_ref[0].astype(F32)
    kpf_ref[...] = kp_ref[0].astype(F32)
    vpf_ref[...] = vp_ref[0].astype(F32)

    def one_class(r, carry):
        cls = lambda ref, n: ref[pl.ds(r, n, stride=dilation), :].astype(BF16)
        q = cls(qf_ref, tq)
        k = jnp.concatenate([cls(kpf_ref, band), cls(kf_ref, tq)], axis=0)
        v = jnp.concatenate([cls(vpf_ref, band), cls(vf_ref, tq)], axis=0)
        outs, lses = [], []
        for u in range(tq // band):
            ku = k[u * band:(u + 2) * band, :]
            vu = v[u * band:(u + 2) * band, :]
            sc = _dot_nt(q[u * band:(u + 1) * band, :], ku)
            qi = lax.broadcasted_iota(jnp.int32, sc.shape, 0)
            kj = lax.broadcasted_iota(jnp.int32, sc.shape, 1)
            ok = jnp.logical_and(kj >= qi, kj <= qi + band)
            if u == 0:
                ok = jnp.logical_and(ok, jnp.logical_or(kj >= band, i > 0))
            sc = jnp.where(ok, sc, NEG_INF)
            m = jnp.max(sc, axis=1, keepdims=True)
            e = jnp.exp2(sc - m)
            den = jnp.sum(e, axis=1, keepdims=True)
            outs.append(jnp.dot(e.astype(BF16), vu, preferred_element_type=F32) / den)
            lses.append(jnp.broadcast_to(m + jnp.log2(den), (band, LANES)))
        o, lse = jnp.concatenate(outs, axis=0), jnp.concatenate(lses, axis=0)
        if n_merge:
            all_o = [o] + [ref[0] for ref in other_o]
            all_lse = [lse] + [ref[0] for ref in other_lse]
            top = functools.reduce(jnp.maximum, all_lse)
            es = [jnp.exp2(x - top) for x in all_lse]
            inv = 1.0 / functools.reduce(lambda a, b: a + b, es)
            mixed = functools.reduce(lambda a, b: a + b, [(e * inv) * x for e, x in zip(es, all_o)])
            out_refs[0][...] = mixed.astype(out_refs[0].dtype)
        else:
            o_ref, lse_ref = out_refs
            o_ref[0, pl.ds(r, tq, stride=dilation), :] = o
            lse_ref[0, pl.ds(r, tq, stride=dilation), :] = lse
        return carry

    unroll = min(dilation, CLASS_UNROLL)

    def classes(t, carry):
        for u in range(unroll):
            one_class(unroll * t + u, carry)
        return carry

    lax.fori_loop(0, dilation // unroll, classes, 0)


def dilated_group(proj, g, window, dilation, merge_with=()):
    _, s, _ = proj.shape
    band = window // dilation
    assert band == LANES and s % (band * dilation) == 0
    assert not merge_with or dilation == 1
    length = s // dilation
    tq = _tile(length, (max(512, 2048 // dilation), 512, 256, 128))
    per = tq // band
    rows, prev_rows = tq * dilation, band * dilation
    sq, sk, sv = SLAB_CQ + g * HEADS, SLAB_CK + g * HEADS, SLAB_CV + g * HEADS
    cur = lambda base: pl.BlockSpec((1, rows, LANES), lambda h, i: (base + h, i, 0))
    prev = lambda base: pl.BlockSpec(
        (1, prev_rows, LANES), lambda h, i: (base + h, jnp.maximum(i * per - 1, 0), 0))
    group_spec = pl.BlockSpec((1, rows, LANES), lambda h, i: (h, i, 0))
    if merge_with:
        out_shape = jax.ShapeDtypeStruct((s, BRANCH_WIDTH), BF16)
        out_specs = pl.BlockSpec((rows, LANES), lambda h, i: (i, h))
    else:
        out_shape = (jax.ShapeDtypeStruct((HEADS, s, LANES), F32),) * 2
        out_specs = (group_spec, group_spec)
    others = [o for o, _ in merge_with] + [lse for _, lse in merge_with]
    return pl.pallas_call(
        functools.partial(_dilated_kernel, tq=tq, band=band, dilation=dilation, n_merge=len(merge_with)),
        out_shape=out_shape,
        grid=(HEADS, length // tq),
        in_specs=[cur(sq), cur(sk), cur(sv), prev(sk), prev(sv)] + [group_spec] * len(others),
        out_specs=out_specs,
        scratch_shapes=[pltpu.VMEM((rows, LANES), F32)] * 3 + [pltpu.VMEM((prev_rows, LANES), F32)] * 2,
        compiler_params=_params(("parallel", "parallel")),
        name=f"dilated_d{dilation}",
    )(proj, proj, proj, proj, proj, *others)


def dilated_mixture(proj):
    dense = [g for g, (_, dilation) in enumerate(DIL_PATTERNS) if dilation == 1]
    assert len(dense) == 1
    others = tuple(dilated_group(proj, g, window, dilation)
                   for g, (window, dilation) in enumerate(DIL_PATTERNS) if dilation != 1)
    return dilated_group(proj, dense[0], *DIL_PATTERNS[dense[0]], merge_with=others)


def _gate_mix_kernel(xg_ref, ssq_ref, oa_ref, ob_ref, oc_ref, wg0_ref, wg1_ref, wg2_ref,
                     b0_ref, b1_ref, b2_ref, wb_ref, z_ref):
    xg = xg_ref[...]
    rstd = _rstd(ssq_ref[...], xg.shape[1], NORM_EPS)
    z = None
    for n, (o_ref, wg_ref, b_ref) in enumerate(((oa_ref, wg0_ref, b0_ref),
                                                (ob_ref, wg1_ref, b1_ref),
                                                (oc_ref, wg2_ref, b2_ref))):
        pre = jnp.dot(xg, wg_ref[...].astype(BF16), preferred_element_type=F32)
        gate = jax.nn.sigmoid(pre * rstd + b_ref[...])
        y = jnp.dot(o_ref[...], wb_ref[n].astype(BF16), preferred_element_type=F32)
        z = gate * y if z is None else z + gate * y
    z_ref[...] = z.astype(z_ref.dtype)


def gate_mix(xg, ssq, o_a, o_b, o_c, w_gate_all, b_gate_all, w_branch_all, layer):
    s, d = xg.shape
    assert N_BRANCH == 3
    tm = _tile(s, (1024, 512, 256))
    tn = _tile(d, (256, 128))
    nj = d // tn
    o_spec = pl.BlockSpec((tm, BRANCH_WIDTH), lambda i, j: (i, 0))
    wg_spec = lambda n: pl.BlockSpec((None, d, tn), lambda i, j: (layer, 0, n * nj + j))
    b_spec = lambda n: pl.BlockSpec((None, 1, tn), lambda i, j: (layer, 0, n * nj + j))
    b3d = b_gate_all.reshape(b_gate_all.shape[0], 1, N_BRANCH * d)
    return pl.pallas_call(
        _gate_mix_kernel,
        out_shape=jax.ShapeDtypeStruct((s, d), BF16),
        grid=(s // tm, nj),
        in_specs=[_resident((tm, d), lambda i, j: (i, 0)),
                  _resident((tm, LANES), lambda i, j: (i, 0)),
                  o_spec, o_spec, o_spec,
                  wg_spec(0), wg_spec(1), wg_spec(2), b_spec(0), b_spec(1), b_spec(2),
                  pl.BlockSpec((None, N_BRANCH, BRANCH_WIDTH, tn), lambda i, j: (layer, 0, 0, j))],
        out_specs=pl.BlockSpec((tm, tn), lambda i, j: (i, j)),
        compiler_params=_params(("parallel", "arbitrary")),
        name="gate_mix",
    )(xg, ssq, o_a, o_b, o_c, w_gate_all, w_gate_all, w_gate_all,
      b3d, b3d, b3d, w_branch_all)


RESIDUAL_ROW_PARTS = 2


def _residual_matmul_kernel(a_ref, w_ref, x_ref, g_ref, o_ref, xg_ref, ssq_ref, *, k_chunks):
    @pl.when(pl.program_id(1) == 0)
    def _():
        ssq_ref[...] = jnp.zeros(ssq_ref.shape, F32)

    kc = a_ref.shape[1] // k_chunks
    ws = [w_ref[c * kc:(c + 1) * kc, :].astype(BF16) for c in range(k_chunks)]
    part = a_ref.shape[0] // RESIDUAL_ROW_PARTS
    for r in range(RESIDUAL_ROW_PARTS):
        rows = slice(r * part, (r + 1) * part)
        acc = x_ref[rows, :]
        for c in range(k_chunks):
            acc = acc + jnp.dot(a_ref[rows, c * kc:(c + 1) * kc], ws[c], preferred_element_type=F32)
        o_ref[rows, :] = acc
        xg_ref[rows, :] = (acc * g_ref[...]).astype(BF16)
        ssq_ref[rows, :] += _lane_folded_sumsq(acc)


def residual_matmul(a, w_all, layer, x, g_next, tm_prefs, tn_prefs, k_chunks=1):
    s, k = a.shape
    d = w_all.shape[2]
    tm = _tile(s, tm_prefs)
    tn = _tile(d, tn_prefs)
    assert k % (k_chunks * LANES) == 0
    return pl.pallas_call(
        functools.partial(_residual_matmul_kernel, k_chunks=k_chunks),
        out_shape=(jax.ShapeDtypeStruct((s, d), F32), jax.ShapeDtypeStruct((s, d), BF16),
                   jax.ShapeDtypeStruct((s, LANES), F32)),
        grid=(s // tm, d // tn),
        in_specs=[_activation_spec((tm, k), lambda i, j: (i, 0), a.dtype.itemsize),
                  pl.BlockSpec((None, k, tn), lambda i, j: (layer, 0, j)),
                  pl.BlockSpec((tm, tn), lambda i, j: (i, j)),
                  pl.BlockSpec((1, tn), lambda i, j: (0, j))],
        out_specs=(pl.BlockSpec((tm, tn), lambda i, j: (i, j)),
                   pl.BlockSpec((tm, tn), lambda i, j: (i, j)),
                   pl.BlockSpec((tm, LANES), lambda i, j: (i, 0))),
        input_output_aliases={2: 0},
        compiler_params=_params(("parallel", "arbitrary")),
        name="residual_matmul",
    )(a, w_all, x, g_next.reshape(1, d))


SWIGLU_ROW_PARTS = 4


def _swiglu_kernel(xg_ref, ssq_ref, wg_ref, wu_ref, o_ref):
    d = xg_ref.shape[1]
    wg = wg_ref[...].astype(BF16)
    wu = wu_ref[...].astype(BF16)
    part = xg_ref.shape[0] // SWIGLU_ROW_PARTS
    for r in range(SWIGLU_ROW_PARTS):
        rows = slice(r * part, (r + 1) * part)
        xg = xg_ref[rows, :]
        rstd = _rstd(ssq_ref[rows, :], d, NORM_EPS)
        g = jnp.dot(xg, wg, preferred_element_type=F32) * rstd
        u = jnp.dot(xg, wu, preferred_element_type=F32) * rstd
        o_ref[rows, :] = (g * jax.nn.sigmoid(g) * u).astype(o_ref.dtype)


def swiglu_up(xg, ssq, w_g_all, w_u_all, layer):
    s, d = xg.shape
    f = w_g_all.shape[2]
    tm = _tile(s, (2048, 1024, 512, 256))
    tn = _tile(f, (256, 128))
    w_spec = pl.BlockSpec((None, d, tn), lambda i, j: (layer, 0, j))
    return pl.pallas_call(
        _swiglu_kernel,
        out_shape=jax.ShapeDtypeStruct((s, f), BF16),
        grid=(s // tm, f // tn),
        in_specs=[_activation_spec((tm, d), lambda i, j: (i, 0), xg.dtype.itemsize),
                  _resident((tm, LANES), lambda i, j: (i, 0)),
                  w_spec, w_spec],
        out_specs=pl.BlockSpec((tm, tn), lambda i, j: (i, j)),
        compiler_params=_params(("parallel", "arbitrary")),
        name="swiglu_up",
    )(xg, ssq, w_g_all, w_u_all)


def kernel(x, g_mix, w_in, lam_q1, lam_k1, lam_q2, lam_k2, g_subln, w_gate, b_gate, w_branch,
           w_out, g_ffn, w_ffn_gate, w_ffn_up, w_ffn_down, g_final):
    b, s, d = x.shape
    depth = w_in.shape[0]
    tabs = rope_tables(s)
    down_chunks = 2 if w_ffn_down.shape[1] % (2 * LANES) == 0 else 1
    outs = []
    for bi in range(b):
        xr = x.reshape(s, d) if b == 1 else x[bi]
        xg, ssq = norm_prep(xr, g_mix[0])
        for l in range(depth):
            lam_init = 0.8 - 0.6 * math.exp(-0.3 * l)
            pad = lambda v: jnp.pad(v.astype(F32), (0, LANES - DIFF_QK_DIM))
            lam_params = jnp.stack(
                [pad(lam_q1[l]), pad(lam_k1[l]), pad(lam_q2[l]), pad(lam_k2[l]),
                 jnp.full((LANES,), lam_init, F32)]
                + [jnp.zeros((LANES,), F32)] * 3)
            proj = in_proj(xg, ssq, w_in, l, tabs)
            o_a = diff_attention(proj, lam_params, g_subln[l])
            o_b = moba_attention(proj)
            o_c = dilated_mixture(proj)
            z = gate_mix(xg, ssq, o_a, o_b, o_c, w_gate, b_gate, w_branch, l)
            xr, xg, ssq = residual_matmul(z, w_out, l, xr, g_ffn[l], (2048, 1024, 512, 256), (256, 128))
            u = swiglu_up(xg, ssq, w_ffn_gate, w_ffn_up, l)
            g_next = g_mix[l + 1] if l + 1 < depth else g_final
            xr, xg, ssq = residual_matmul(u, w_ffn_down, l, xr, g_next, (1024, 512, 256), (256, 128),
                                          down_chunks)
        outs.append(rmsnorm(xr, g_final, NORM_EPS, x.dtype))
    return outs[0].reshape(b, s, d) if b == 1 else jnp.stack(outs, axis=0)
```

```python
import functools
import math

import jax
import jax.numpy as jnp
from jax import lax
from jax.experimental import pallas as pl
from jax.experimental.pallas import tpu as pltpu

HEAD_DIM = 128
DIFF_QK_DIM = HEAD_DIM // 2
MOBA_BLOCK = 256
MOBA_TOP_K = 3
DIL_PATTERNS = ((128, 1), (512, 4), (2048, 16))
N_BRANCH = 3
BRANCH_WIDTH = 512
ROPE_THETA = 10000.0
NORM_EPS = 1e-6
SUBLN_EPS = 1e-5
NEG_INF = -1e30
LOG2_E = math.log2(math.e)

HEADS = 4
SLAB_AQ, SLAB_AK, SLAB_AV = 0, 4, 8
SLAB_BQ, SLAB_BK, SLAB_BV = 12, 16, 20
SLAB_CQ, SLAB_CK, SLAB_CV = 24, 36, 48
PROJ_TILE = 512

LANES = 128
BF16_SUBLANES = 16
VMEM_LIMIT_BYTES = 58 * 1024 * 1024

F32 = jnp.float32
BF16 = jnp.bfloat16


def _params(sem):
    return pltpu.CompilerParams(dimension_semantics=sem, vmem_limit_bytes=VMEM_LIMIT_BYTES)


def _tile(n, prefs):
    for p in prefs:
        if n % p == 0:
            return p
    return n


def _resident(block_shape, index_map):
    return pl.BlockSpec(block_shape, index_map, pipeline_mode=pl.Buffered(1))


PREFETCHED_ACT_TILE_BYTES = 16 * 1024 * 1024


def _activation_spec(block_shape, index_map, itemsize):
    if math.prod(block_shape) * itemsize <= PREFETCHED_ACT_TILE_BYTES:
        return pl.BlockSpec(block_shape, index_map)
    return _resident(block_shape, index_map)


def _lane_folded_sumsq(x):
    sq = x * x
    out = sq[:, 0:LANES]
    for c in range(1, x.shape[1] // LANES):
        out = out + sq[:, c * LANES:(c + 1) * LANES]
    return out


def _rstd(ssq, d, eps):
    return lax.rsqrt(jnp.sum(ssq, axis=1, keepdims=True) * (1.0 / d) + eps)


def _norm_prep_kernel(x_ref, g_ref, xg_ref, ssq_ref):
    x = x_ref[...]
    xg_ref[...] = (x * g_ref[...]).astype(BF16)
    ssq_ref[...] = _lane_folded_sumsq(x)


def norm_prep(x, g):
    s, d = x.shape
    tm = _tile(s, (256, 128, 8))
    return pl.pallas_call(
        _norm_prep_kernel,
        out_shape=(jax.ShapeDtypeStruct((s, d), BF16), jax.ShapeDtypeStruct((s, LANES), F32)),
        grid=(s // tm,),
        in_specs=[pl.BlockSpec((tm, d), lambda i: (i, 0)), pl.BlockSpec((1, d), lambda i: (0, 0))],
        out_specs=(pl.BlockSpec((tm, d), lambda i: (i, 0)), pl.BlockSpec((tm, LANES), lambda i: (i, 0))),
        compiler_params=_params(("parallel",)),
        name="norm_prep",
    )(x, g.reshape(1, d))


def _rmsnorm_kernel(x_ref, g_ref, o_ref, *, eps):
    x = x_ref[...]
    ms = jnp.mean(x * x, axis=-1, keepdims=True)
    o_ref[...] = (x * lax.rsqrt(ms + eps) * g_ref[...]).astype(o_ref.dtype)


def rmsnorm(x, g, eps, out_dtype):
    s, d = x.shape
    tm = _tile(s, (256, 128, 8))
    return pl.pallas_call(
        functools.partial(_rmsnorm_kernel, eps=eps),
        out_shape=jax.ShapeDtypeStruct((s, d), out_dtype),
        grid=(s // tm,),
        in_specs=[pl.BlockSpec((tm, d), lambda i: (i, 0)),
                  pl.BlockSpec((1, d), lambda i: (0, 0))],
        out_specs=pl.BlockSpec((tm, d), lambda i: (i, 0)),
        compiler_params=_params(("parallel",)),
        name="rmsnorm",
    )(x, g.reshape(1, d))


KIND_PLAIN, KIND_ROPE128_Q, KIND_ROPE128_K, KIND_ROPE64_Q, KIND_ROPE64_K = range(5)
_TILE_KINDS = (KIND_ROPE64_Q, KIND_ROPE64_K, KIND_PLAIN,
               KIND_ROPE128_Q, KIND_ROPE128_K, KIND_PLAIN,
               KIND_ROPE128_Q, KIND_ROPE128_Q, KIND_ROPE128_Q,
               KIND_ROPE128_K, KIND_ROPE128_K, KIND_ROPE128_K,
               KIND_PLAIN, KIND_PLAIN, KIND_PLAIN)
IN_PROJ_ROW_PARTS = 16


def _tile_kind(j):
    kind = jnp.int32(_TILE_KINDS[-1])
    for t in range(len(_TILE_KINDS) - 2, -1, -1):
        kind = jnp.where(j == t, _TILE_KINDS[t], kind)
    return kind


def _in_proj_kernel(xg_ref, ssq_ref, w_ref, ta_ref, tb_ref, tc_ref, td_ref, o_ref):
    d = xg_ref.shape[1]
    w = w_ref[...].astype(BF16)
    part = xg_ref.shape[0] // IN_PROJ_ROW_PARTS
    for r in range(IN_PROJ_ROW_PARTS):
        rows = slice(r * part, (r + 1) * part)
        acc = jnp.dot(xg_ref[rows, :], w, preferred_element_type=F32)
        acc = acc * _rstd(ssq_ref[rows, :], d, NORM_EPS)
        ta, tb, tc, td = ta_ref[rows, :], tb_ref[rows, :], tc_ref[rows, :], td_ref[rows, :]
        for c in range(PROJ_TILE // LANES):
            xs = acc[:, c * LANES:(c + 1) * LANES]
            y = (xs * ta + pltpu.roll(xs, 64, 1) * tb
                 + pltpu.roll(xs, 32, 1) * tc + pltpu.roll(xs, 96, 1) * td)
            o_ref[c, rows, :] = y.astype(o_ref.dtype)


def in_proj(xg, ssq, w_all, layer, tabs):
    s, d = xg.shape
    n = w_all.shape[2]
    assert n // PROJ_TILE == len(_TILE_KINDS)
    tm = _tile(s, (2048, 1024, 512, 256))
    tab_spec = pl.BlockSpec((None, tm, LANES), lambda i, j: (_tile_kind(j), i, 0))
    return pl.pallas_call(
        _in_proj_kernel,
        out_shape=jax.ShapeDtypeStruct((n // LANES, s, LANES), BF16),
        grid=(s // tm, n // PROJ_TILE),
        in_specs=[_resident((tm, d), lambda i, j: (i, 0)),
                  _resident((tm, LANES), lambda i, j: (i, 0)),
                  pl.BlockSpec((None, d, PROJ_TILE), lambda i, j: (layer, 0, j)),
                  tab_spec, tab_spec, tab_spec, tab_spec],
        out_specs=pl.BlockSpec((PROJ_TILE // LANES, tm, LANES), lambda i, j: (j, i, 0)),
        compiler_params=_params(("parallel", "arbitrary")),
        name="in_proj",
    )(xg, ssq, w_all, *tabs)


def rope_tables(seq):
    def angles(dim):
        inv = ROPE_THETA ** (-jnp.arange(0, dim, 2, dtype=F32) / dim)
        ang = jnp.arange(seq, dtype=F32)[:, None] * inv[None, :]
        reps = LANES // dim
        cos = jnp.tile(jnp.concatenate([jnp.cos(ang)] * 2, axis=1), (1, reps))
        sin = jnp.tile(jnp.concatenate([-jnp.sin(ang), jnp.sin(ang)], axis=1), (1, reps))
        return cos, sin
    c128, s128 = angles(HEAD_DIM)
    c64, s64 = angles(DIFF_QK_DIM)
    zero, one = jnp.zeros((seq, LANES), F32), jnp.ones((seq, LANES), F32)
    low = (jnp.arange(LANES) & (DIFF_QK_DIM // 2)) == 0
    s64_up, s64_down = jnp.where(low, s64, 0.0), jnp.where(low, 0.0, s64)
    q128, q64 = HEAD_DIM ** -0.5 * LOG2_E, DIFF_QK_DIM ** -0.5 * LOG2_E
    kinds = {KIND_PLAIN: (one, zero, zero, zero),
             KIND_ROPE128_Q: (c128 * q128, s128 * q128, zero, zero),
             KIND_ROPE128_K: (c128, s128, zero, zero),
             KIND_ROPE64_Q: (c64 * q64, zero, s64_down * q64, s64_up * q64),
             KIND_ROPE64_K: (c64, zero, s64_down, s64_up)}
    return tuple(jnp.stack([kinds[k][t] for k in range(len(kinds))]) for t in range(4))


VT_ROWS = HEAD_DIM + BF16_SUBLANES


def _transpose_bf16(x):
    return x.astype(F32).T.astype(BF16)


def _dot_nt(a, b):
    return lax.dot_general(a, b, (((1,), (1,)), ((), ())), preferred_element_type=F32)


def _fill_v_transposed(v_ref, vt_ref, chunk):
    seq = v_ref.shape[1]
    for c in range(seq // chunk):
        vt_ref[0:HEAD_DIM, c * chunk:(c + 1) * chunk] = _transpose_bf16(
            v_ref[0, c * chunk:(c + 1) * chunk, :])
    vt_ref[HEAD_DIM:VT_ROWS, :] = jnp.ones((BF16_SUBLANES, seq), BF16)


def _softmax_step(st, vt, carry):
    m_old, acc = carry
    m_new = jnp.maximum(m_old, jnp.max(st, axis=0, keepdims=True))
    alpha = jnp.exp2(m_old - m_new)
    p = jnp.exp2(st - m_new).astype(BF16)
    return m_new, alpha * acc + jnp.dot(vt, p, preferred_element_type=F32)


def _init_carry(nq):
    return jnp.full((1, nq), NEG_INF, F32), jnp.zeros((VT_ROWS, nq), F32)


def _normalized(carry):
    acc = carry[1]
    return acc[0:HEAD_DIM, :] / acc[HEAD_DIM:HEAD_DIM + 1, :]


TILE_UNROLL = 4


def _pipelined_attention(n_past, granule, past_tile, tail_tiles, scores_into, st_ref, carry):
    def sequence(first, tiles, c, issue_after_last):
        for u, tile in enumerate(tiles):
            if u + 1 < len(tiles) or issue_after_last:
                scores_into((u + 1) % 2, first + u + 1)
            c = _softmax_step(*tile(first + u, st_ref[u % 2]), c)
        return c

    assert TILE_UNROLL % 2 == 0 and TILE_UNROLL % granule == 0
    scores_into(0, 0)
    carry = lax.fori_loop(
        0, n_past // TILE_UNROLL,
        lambda t, c: sequence(TILE_UNROLL * t, [past_tile] * TILE_UNROLL, c, True), carry)
    first = (n_past // TILE_UNROLL) * TILE_UNROLL
    leftovers = range(0, TILE_UNROLL, granule)
    return lax.switch((n_past % TILE_UNROLL) // granule,
                      [lambda c, r=r: sequence(first, [past_tile] * r + list(tail_tiles), c, False)
                       for r in leftovers], carry)


def _diff_attn_kernel(q_ref, k_ref, v_ref, lam_ref, gsub_ref, o_ref, vt_ref, qs_ref, st_ref, *, tq):
    i = pl.program_id(1)

    @pl.when(i == 0)
    def _():
        _fill_v_transposed(v_ref, vt_ref, tq)

    q = q_ref[0]
    lane = lax.broadcasted_iota(jnp.int32, q.shape, 1)
    zero = jnp.zeros_like(q)
    qs_ref[0:tq, :] = jnp.where(lane < DIFF_QK_DIM, q, zero)
    qs_ref[tq:2 * tq, :] = jnp.where(lane >= DIFF_QK_DIM, q, zero)

    def scores_into(slot, j):
        start = pl.multiple_of(j * tq, tq)
        st_ref[slot] = _dot_nt(k_ref[0, pl.ds(start, tq), :], qs_ref[...])

    def values(j):
        return vt_ref[:, pl.ds(pl.multiple_of(j * tq, tq), tq)]

    def past(j, st):
        return st, values(j)

    def diagonal(j, st):
        krow = lax.broadcasted_iota(jnp.int32, st.shape, 0)
        qcol = lax.broadcasted_iota(jnp.int32, st.shape, 1)
        qcol = jnp.where(qcol >= tq, qcol - tq, qcol)
        return jnp.where(krow <= qcol, st, NEG_INF), values(j)

    carry = _pipelined_attention(i, 1, past, [diagonal], scores_into, st_ref, _init_carry(2 * tq))

    lp = lam_ref[...]
    lam_init = lp[4:5, 0:1]
    lam = (jnp.exp(jnp.sum(lp[0:1] * lp[1:2], axis=1, keepdims=True))
           - jnp.exp(jnp.sum(lp[2:3] * lp[3:4], axis=1, keepdims=True)) + lam_init)
    o = _normalized(carry)
    od = o[:, 0:tq] - lam * o[:, tq:2 * tq]
    ms = jnp.mean(od * od, axis=0, keepdims=True)
    y = od * lax.rsqrt(ms + SUBLN_EPS) * gsub_ref[...] * (1.0 - lam_init)
    o_ref[...] = y.T.astype(o_ref.dtype)


def diff_attention(proj, lam_params, g_sub):
    _, s, _ = proj.shape
    tq = _tile(s, (512, 256, 128))
    return pl.pallas_call(
        functools.partial(_diff_attn_kernel, tq=tq),
        out_shape=jax.ShapeDtypeStruct((s, BRANCH_WIDTH), BF16),
        grid=(HEADS, s // tq),
        in_specs=[pl.BlockSpec((1, tq, LANES), lambda h, i: (SLAB_AQ + h, i, 0)),
                  pl.BlockSpec((1, s, LANES), lambda h, i: (SLAB_AK + h, 0, 0)),
                  pl.BlockSpec((1, s, LANES), lambda h, i: (SLAB_AV + h, 0, 0)),
                  pl.BlockSpec((8, LANES), lambda h, i: (0, 0)),
                  pl.BlockSpec((HEAD_DIM, 1), lambda h, i: (0, 0))],
        out_specs=pl.BlockSpec((tq, LANES), lambda h, i: (i, h)),
        scratch_shapes=[pltpu.VMEM((VT_ROWS, s), BF16),
                        pltpu.VMEM((2 * tq, LANES), BF16),
                        pltpu.VMEM((2, tq, 2 * tq), F32)],
        compiler_params=_params(("arbitrary", "arbitrary")),
        name="diff_attn",
    )(proj, proj, proj, lam_params, g_sub.reshape(HEAD_DIM, 1))


def _moba_attn_kernel(q_ref, k_ref, v_ref, o_ref, vt_ref, kmean_ref, bias_ref, st_ref, *, tq, tk, nb):
    i = pl.program_id(1)
    blk = MOBA_BLOCK
    blk_shift = blk.bit_length() - 1
    blk_per_tile = tk // blk
    own_tiles = tq // tk

    @pl.when(i == 0)
    def _():
        _fill_v_transposed(v_ref, vt_ref, tk)
        for n in range(nb):
            kmean_ref[n:n + 1, :] = jnp.mean(
                k_ref[0, n * blk:(n + 1) * blk, :].astype(F32), axis=0, keepdims=True)

    q = q_ref[0]
    km = kmean_ref[...]
    km_hi = km.astype(BF16)
    km_lo = (km - km_hi.astype(F32)).astype(BF16)
    gate = _dot_nt(km_hi, q) + _dot_nt(km_lo, q)
    bidx = lax.broadcasted_iota(jnp.int32, gate.shape, 0)
    qpos = i * tq + lax.broadcasted_iota(jnp.int32, gate.shape, 1)
    own = jnp.right_shift(qpos, blk_shift)
    past_blk = bidx < own
    gate = jnp.where(past_blk, gate, NEG_INF)
    cand = gate
    selected = jnp.zeros(gate.shape, jnp.bool_)
    for _ in range(MOBA_TOP_K):
        top = jnp.max(cand, axis=0, keepdims=True)
        first = jnp.min(jnp.where(cand == top, bidx, nb), axis=0, keepdims=True)
        pick = bidx == first
        selected = jnp.logical_or(selected, pick)
        cand = jnp.where(pick, NEG_INF, cand)
    selected = jnp.logical_and(selected, past_blk)
    bias_ref[...] = jnp.where(selected, 0.0, NEG_INF).astype(F32)

    def scores_into(slot, j):
        start = pl.multiple_of(j * tk, tk)
        st_ref[slot] = _dot_nt(k_ref[0, pl.ds(start, tk), :], q)

    def values(j):
        return vt_ref[:, pl.ds(pl.multiple_of(j * tk, tk), tk)]

    def past(j, st):
        st = jnp.concatenate(
            [st[b * blk:(b + 1) * blk, :] + bias_ref[pl.ds(j * blk_per_tile + b, 1), :]
             for b in range(blk_per_tile)], axis=0)
        return st, values(j)

    own_row = jnp.right_shift(i * tq + lax.broadcasted_iota(jnp.int32, (1, tq), 1), blk_shift)

    def own(j, st):
        kpos = j * tk + lax.broadcasted_iota(jnp.int32, st.shape, 0)
        qp = i * tq + lax.broadcasted_iota(jnp.int32, st.shape, 1)
        st = jnp.where(kpos <= qp, st, NEG_INF)
        parts = []
        for b in range(blk_per_tile):
            n = j * blk_per_tile + b
            row_bias = jnp.where(own_row > n, bias_ref[pl.ds(n, 1), :],
                                 jnp.where(own_row == n, 0.0, NEG_INF))
            parts.append(st[b * blk:(b + 1) * blk, :] + row_bias)
        return jnp.concatenate(parts, axis=0), values(j)

    carry = _pipelined_attention(i * own_tiles, own_tiles, past, [own] * own_tiles, scores_into, st_ref,
                                 _init_carry(tq))

    o_ref[...] = _normalized(carry).T.astype(o_ref.dtype)


def moba_attention(proj):
    _, s, _ = proj.shape
    assert s % MOBA_BLOCK == 0 and MOBA_BLOCK & (MOBA_BLOCK - 1) == 0
    nb = s // MOBA_BLOCK
    tq = _tile(s, (1024, 512, 256))
    tk = _tile(tq, (512, 256))
    return pl.pallas_call(
        functools.partial(_moba_attn_kernel, tq=tq, tk=tk, nb=nb),
        out_shape=jax.ShapeDtypeStruct((s, BRANCH_WIDTH), BF16),
        grid=(HEADS, s // tq),
        in_specs=[pl.BlockSpec((1, tq, LANES), lambda h, i: (SLAB_BQ + h, i, 0)),
                  pl.BlockSpec((1, s, LANES), lambda h, i: (SLAB_BK + h, 0, 0)),
                  pl.BlockSpec((1, s, LANES), lambda h, i: (SLAB_BV + h, 0, 0))],
        out_specs=pl.BlockSpec((tq, LANES), lambda h, i: (i, h)),
        scratch_shapes=[pltpu.VMEM((VT_ROWS, s), BF16),
                        pltpu.VMEM((nb, LANES), F32),
                        pltpu.VMEM((nb, tq), F32),
                        pltpu.VMEM((2, tk, tq), F32)],
        compiler_params=_params(("arbitrary", "arbitrary")),
        name="moba_attn",
    )(proj, proj, proj)


CLASS_UNROLL = 4


def _dilated_kernel(q_ref, k_ref, v_ref, kp_ref, vp_ref, *refs, tq, band, dilation, n_merge):
    other_o, other_lse = refs[:n_merge], refs[n_merge:2 * n_merge]
    out_refs = refs[2 * n_merge:len(refs) - 5]
    qf_ref, kf_ref, vf_ref, kpf_ref, vpf_ref = refs[len(refs) - 5:]
    i = pl.program_id(1)
    qf_ref[...] = q_ref[0].astype(F32)
    kf_ref[...] = k_ref[0].astype(F32)
    vf_ref[...] = v_ref[0].astype(F32)
    kpf_ref[...] = kp_ref[0].astype(F32)
    vpf_ref[...] = vp_ref[0].astype(F32)

    def one_class(r, carry):
        cls = lambda ref, n: ref[pl.ds(r, n, stride=dilation), :].astype(BF16)
        q = cls(qf_ref, tq)
        k = jnp.concatenate([cls(kpf_ref, band), cls(kf_ref, tq)], axis=0)
        v = jnp.concatenate([cls(vpf_ref, band), cls(vf_ref, tq)], axis=0)
        outs, lses = [], []
        for u in range(tq // band):
            ku = k[u * band:(u + 2) * band, :]
            vu = v[u * band:(u + 2) * band, :]
            sc = _dot_nt(q[u * band:(u + 1) * band, :], ku)
            qi = lax.broadcasted_iota(jnp.int32, sc.shape, 0)
            kj = lax.broadcasted_iota(jnp.int32, sc.shape, 1)
            ok = jnp.logical_and(kj >= qi, kj <= qi + band)
            if u == 0:
                ok = jnp.logical_and(ok, jnp.logical_or(kj >= band, i > 0))
            sc = jnp.where(ok, sc, NEG_INF)
            m = jnp.max(sc, axis=1, keepdims=True)
            e = jnp.exp2(sc - m)
            den = jnp.sum(e, axis=1, keepdims=True)
            outs.append(jnp.dot(e.astype(BF16), vu, preferred_element_type=F32) / den)
            lses.append(jnp.broadcast_to(m + jnp.log2(den), (band, LANES)))
        o, lse = jnp.concatenate(outs, axis=0), jnp.concatenate(lses, axis=0)
        if n_merge:
            all_o = [o] + [ref[0] for ref in other_o]
            all_lse = [lse] + [ref[0] for ref in other_lse]
            top = functools.reduce(jnp.maximum, all_lse)
            es = [jnp.exp2(x - top) for x in all_lse]
            inv = 1.0 / functools.reduce(lambda a, b: a + b, es)
            mixed = functools.reduce(lambda a, b: a + b, [(e * inv) * x for e, x in zip(es, all_o)])
            out_refs[0][...] = mixed.astype(out_refs[0].dtype)
        else:
            o_ref, lse_ref = out_refs
            o_ref[0, pl.ds(r, tq, stride=dilation), :] = o
            lse_ref[0, pl.ds(r, tq, stride=dilation), :] = lse
        return carry

    unroll = min(dilation, CLASS_UNROLL)

    def classes(t, carry):
        for u in range(unroll):
            one_class(unroll * t + u, carry)
        return carry

    lax.fori_loop(0, dilation // unroll, classes, 0)


def dilated_group(proj, g, window, dilation, merge_with=()):
    _, s, _ = proj.shape
    band = window // dilation
    assert band == LANES and s % (band * dilation) == 0
    assert not merge_with or dilation == 1
    length = s // dilation
    tq = _tile(length, (max(512, 2048 // dilation), 512, 256, 128))
    per = tq // band
    rows, prev_rows = tq * dilation, band * dilation
    sq, sk, sv = SLAB_CQ + g * HEADS, SLAB_CK + g * HEADS, SLAB_CV + g * HEADS
    cur = lambda base: pl.BlockSpec((1, rows, LANES), lambda h, i: (base + h, i, 0))
    prev = lambda base: pl.BlockSpec(
        (1, prev_rows, LANES), lambda h, i: (base + h, jnp.maximum(i * per - 1, 0), 0))
    group_spec = pl.BlockSpec((1, rows, LANES), lambda h, i: (h, i, 0))
    if merge_with:
        out_shape = jax.ShapeDtypeStruct((s, BRANCH_WIDTH), BF16)
        out_specs = pl.BlockSpec((rows, LANES), lambda h, i: (i, h))
    else:
        out_shape = (jax.ShapeDtypeStruct((HEADS, s, LANES), F32),) * 2
        out_specs = (group_spec, group_spec)
    others = [o for o, _ in merge_with] + [lse for _, lse in merge_with]
    return pl.pallas_call(
        functools.partial(_dilated_kernel, tq=tq, band=band, dilation=dilation, n_merge=len(merge_with)),
        out_shape=out_shape,
        grid=(HEADS, length // tq),
        in_specs=[cur(sq), cur(sk), cur(sv), prev(sk), prev(sv)] + [group_spec] * len(others),
        out_specs=out_specs,
        scratch_shapes=[pltpu.VMEM((rows, LANES), F32)] * 3 + [pltpu.VMEM((prev_rows, LANES), F32)] * 2,
        compiler_params=_params(("parallel", "parallel")),
        name=f"dilated_d{dilation}",
    )(proj, proj, proj, proj, proj, *others)


def dilated_mixture(proj):
    dense = [g for g, (_, dilation) in enumerate(DIL_PATTERNS) if dilation == 1]
    assert len(dense) == 1
    others = tuple(dilated_group(proj, g, window, dilation)
                   for g, (window, dilation) in enumerate(DIL_PATTERNS) if dilation != 1)
    return dilated_group(proj, dense[0], *DIL_PATTERNS[dense[0]], merge_with=others)


def _gate_mix_kernel(xg_ref, ssq_ref, oa_ref, ob_ref, oc_ref, wg0_ref, wg1_ref, wg2_ref,
                     b0_ref, b1_ref, b2_ref, wb_ref, z_ref):
    xg = xg_ref[...]
    rstd = _rstd(ssq_ref[...], xg.shape[1], NORM_EPS)
    z = None
    for n, (o_ref, wg_ref, b_ref) in enumerate(((oa_ref, wg0_ref, b0_ref),
                                                (ob_ref, wg1_ref, b1_ref),
                                                (oc_ref, wg2_ref, b2_ref))):
        pre = jnp.dot(xg, wg_ref[...].astype(BF16), preferred_element_type=F32)
        gate = jax.nn.sigmoid(pre * rstd + b_ref[...])
        y = jnp.dot(o_ref[...], wb_ref[n].astype(BF16), preferred_element_type=F32)
        z = gate * y if z is None else z + gate * y
    z_ref[...] = z.astype(z_ref.dtype)


def gate_mix(xg, ssq, o_a, o_b, o_c, w_gate_all, b_gate_all, w_branch_all, layer):
    s, d = xg.shape
    assert N_BRANCH == 3
    tm = _tile(s, (1024, 512, 256))
    tn = _tile(d, (256, 128))
    nj = d // tn
    o_spec = _resident((tm, BRANCH_WIDTH), lambda i, j: (i, 0))
    wg_spec = lambda n: pl.BlockSpec((None, d, tn), lambda i, j: (layer, 0, n * nj + j))
    b_spec = lambda n: pl.BlockSpec((None, 1, tn), lambda i, j: (layer, 0, n * nj + j))
    b3d = b_gate_all.reshape(b_gate_all.shape[0], 1, N_BRANCH * d)
    return pl.pallas_call(
        _gate_mix_kernel,
        out_shape=jax.ShapeDtypeStruct((s, d), BF16),
        grid=(s // tm, nj),
        in_specs=[_resident((tm, d), lambda i, j: (i, 0)),
                  _resident((tm, LANES), lambda i, j: (i, 0)),
                  o_spec, o_spec, o_spec,
                  wg_spec(0), wg_spec(1), wg_spec(2), b_spec(0), b_spec(1), b_spec(2),
                  pl.BlockSpec((None, N_BRANCH, BRANCH_WIDTH, tn), lambda i, j: (layer, 0, 0, j))],
        out_specs=pl.BlockSpec((tm, tn), lambda i, j: (i, j)),
        compiler_params=_params(("parallel", "arbitrary")),
        name="gate_mix",
    )(xg, ssq, o_a, o_b, o_c, w_gate_all, w_gate_all, w_gate_all,
      b3d, b3d, b3d, w_branch_all)


RESIDUAL_ROW_PARTS = 4


def _residual_matmul_kernel(a_ref, w_ref, x_ref, g_ref, o_ref, xg_ref, ssq_ref, *, k_chunks):
    @pl.when(pl.program_id(1) == 0)
    def _():
        ssq_ref[...] = jnp.zeros(ssq_ref.shape, F32)

    kc = a_ref.shape[1] // k_chunks
    ws = [w_ref[c * kc:(c + 1) * kc, :].astype(BF16) for c in range(k_chunks)]
    part = a_ref.shape[0] // RESIDUAL_ROW_PARTS
    for r in range(RESIDUAL_ROW_PARTS):
        rows = slice(r * part, (r + 1) * part)
        acc = x_ref[rows, :]
        for c in range(k_chunks):
            acc = acc + jnp.dot(a_ref[rows, c * kc:(c + 1) * kc], ws[c], preferred_element_type=F32)
        o_ref[rows, :] = acc
        xg_ref[rows, :] = (acc * g_ref[...]).astype(BF16)
        ssq_ref[rows, :] += _lane_folded_sumsq(acc)


def residual_matmul(a, w_all, layer, x, g_next, tm_prefs, tn_prefs, k_chunks=1):
    s, k = a.shape
    d = w_all.shape[2]
    tm = _tile(s, tm_prefs)
    tn = _tile(d, tn_prefs)
    assert k % (k_chunks * LANES) == 0
    return pl.pallas_call(
        functools.partial(_residual_matmul_kernel, k_chunks=k_chunks),
        out_shape=(jax.ShapeDtypeStruct((s, d), F32), jax.ShapeDtypeStruct((s, d), BF16),
                   jax.ShapeDtypeStruct((s, LANES), F32)),
        grid=(s // tm, d // tn),
        in_specs=[_activation_spec((tm, k), lambda i, j: (i, 0), a.dtype.itemsize),
                  pl.BlockSpec((None, k, tn), lambda i, j: (layer, 0, j)),
                  pl.BlockSpec((tm, tn), lambda i, j: (i, j)),
                  pl.BlockSpec((1, tn), lambda i, j: (0, j))],
        out_specs=(pl.BlockSpec((tm, tn), lambda i, j: (i, j)),
                   pl.BlockSpec((tm, tn), lambda i, j: (i, j)),
                   pl.BlockSpec((tm, LANES), lambda i, j: (i, 0))),
        input_output_aliases={2: 0},
        compiler_params=_params(("parallel", "arbitrary")),
        name="residual_matmul",
    )(a, w_all, x, g_next.reshape(1, d))


SWIGLU_ROW_PARTS = 8


def _swiglu_kernel(xg_ref, ssq_ref, wg_ref, wu_ref, o_ref):
    d = xg_ref.shape[1]
    wg = wg_ref[...].astype(BF16)
    wu = wu_ref[...].astype(BF16)
    part = xg_ref.shape[0] // SWIGLU_ROW_PARTS
    for r in range(SWIGLU_ROW_PARTS):
        rows = slice(r * part, (r + 1) * part)
        xg = xg_ref[rows, :]
        rstd = _rstd(ssq_ref[rows, :], d, NORM_EPS)
        g = jnp.dot(xg, wg, preferred_element_type=F32) * rstd
        u = jnp.dot(xg, wu, preferred_element_type=F32) * rstd
        o_ref[rows, :] = (g * jax.nn.sigmoid(g) * u).astype(o_ref.dtype)


def swiglu_up(xg, ssq, w_g_all, w_u_all, layer):
    s, d = xg.shape
    f = w_g_all.shape[2]
    tm = _tile(s, (2048, 1024, 512, 256))
    tn = _tile(f, (256, 128))
    w_spec = pl.BlockSpec((None, d, tn), lambda i, j: (layer, 0, j))
    return pl.pallas_call(
        _swiglu_kernel,
        out_shape=jax.ShapeDtypeStruct((s, f), BF16),
        grid=(s // tm, f // tn),
        in_specs=[_activation_spec((tm, d), lambda i, j: (i, 0), xg.dtype.itemsize),
                  _resident((tm, LANES), lambda i, j: (i, 0)),
                  w_spec, w_spec],
        out_specs=pl.BlockSpec((tm, tn), lambda i, j: (i, j)),
        compiler_params=_params(("parallel", "arbitrary")),
        name="swiglu_up",
    )(xg, ssq, w_g_all, w_u_all)


def kernel(x, g_mix, w_in, lam_q1, lam_k1, lam_q2, lam_k2, g_subln, w_gate, b_gate, w_branch,
           w_out, g_ffn, w_ffn_gate, w_ffn_up, w_ffn_down, g_final):
    b, s, d = x.shape
    depth = w_in.shape[0]
    tabs = rope_tables(s)
    down_chunks = 2 if w_ffn_down.shape[1] % (2 * LANES) == 0 else 1
    outs = []
    for bi in range(b):
        xr = x.reshape(s, d) if b == 1 else x[bi]
        xg, ssq = norm_prep(xr, g_mix[0])
        for l in range(depth):
            lam_init = 0.8 - 0.6 * math.exp(-0.3 * l)
            pad = lambda v: jnp.pad(v.astype(F32), (0, LANES - DIFF_QK_DIM))
            lam_params = jnp.stack(
                [pad(lam_q1[l]), pad(lam_k1[l]), pad(lam_q2[l]), pad(lam_k2[l]),
                 jnp.full((LANES,), lam_init, F32)]
                + [jnp.zeros((LANES,), F32)] * 3)
            proj = in_proj(xg, ssq, w_in, l, tabs)
            o_a = diff_attention(proj, lam_params, g_subln[l])
            o_b = moba_attention(proj)
            o_c = dilated_mixture(proj)
            z = gate_mix(xg, ssq, o_a, o_b, o_c, w_gate, b_gate, w_branch, l)
            xr, xg, ssq = residual_matmul(z, w_out, l, xr, g_ffn[l], (2048, 1024, 512, 256), (256, 128))
            u = swiglu_up(xg, ssq, w_ffn_gate, w_ffn_up, l)
            g_next = g_mix[l + 1] if l + 1 < depth else g_final
            xr, xg, ssq = residual_matmul(u, w_ffn_down, l, xr, g_next, (1024, 512, 256), (256, 128),
                                          down_chunks)
        outs.append(rmsnorm(xr, g_final, NORM_EPS, x.dtype))
    return outs[0].reshape(b, s, d) if b == 1 else jnp.stack(outs, axis=0)
```

```python
import functools
import math

import jax
import jax.numpy as jnp
from jax import lax
from jax.experimental import pallas as pl
from jax.experimental.pallas import tpu as pltpu

HEAD_DIM = 128
DIFF_QK_DIM = HEAD_DIM // 2
MOBA_BLOCK = 256
MOBA_TOP_K = 3
DIL_PATTERNS = ((128, 1), (512, 4), (2048, 16))
N_BRANCH = 3
BRANCH_WIDTH = 512
ROPE_THETA = 10000.0
NORM_EPS = 1e-6
SUBLN_EPS = 1e-5
NEG_INF = -1e30
LOG2_E = math.log2(math.e)

HEADS = 4
SLAB_AQ, SLAB_AK, SLAB_AV = 0, 4, 8
SLAB_BQ, SLAB_BK, SLAB_BV = 12, 16, 20
SLAB_CQ, SLAB_CK, SLAB_CV = 24, 36, 48
PROJ_TILE = 512

LANES = 128
BF16_SUBLANES = 16
VMEM_LIMIT_BYTES = 58 * 1024 * 1024

F32 = jnp.float32
BF16 = jnp.bfloat16


def _params(sem):
    return pltpu.CompilerParams(dimension_semantics=sem, vmem_limit_bytes=VMEM_LIMIT_BYTES)


def _tile(n, prefs):
    for p in prefs:
        if n % p == 0:
            return p
    return n


def _resident(block_shape, index_map):
    return pl.BlockSpec(block_shape, index_map, pipeline_mode=pl.Buffered(1))


PREFETCHED_ACT_TILE_BYTES = 16 * 1024 * 1024


def _activation_spec(block_shape, index_map, itemsize):
    if math.prod(block_shape) * itemsize <= PREFETCHED_ACT_TILE_BYTES:
        return pl.BlockSpec(block_shape, index_map)
    return _resident(block_shape, index_map)


def _lane_folded_sumsq(x):
    sq = x * x
    out = sq[:, 0:LANES]
    for c in range(1, x.shape[1] // LANES):
        out = out + sq[:, c * LANES:(c + 1) * LANES]
    return out


def _rstd(ssq, d, eps):
    return lax.rsqrt(jnp.sum(ssq, axis=1, keepdims=True) * (1.0 / d) + eps)


def _norm_prep_kernel(x_ref, g_ref, xg_ref, ssq_ref):
    x = x_ref[...]
    xg_ref[...] = (x * g_ref[...]).astype(BF16)
    ssq_ref[...] = _lane_folded_sumsq(x)


def norm_prep(x, g):
    s, d = x.shape
    tm = _tile(s, (256, 128, 8))
    return pl.pallas_call(
        _norm_prep_kernel,
        out_shape=(jax.ShapeDtypeStruct((s, d), BF16), jax.ShapeDtypeStruct((s, LANES), F32)),
        grid=(s // tm,),
        in_specs=[pl.BlockSpec((tm, d), lambda i: (i, 0)), pl.BlockSpec((1, d), lambda i: (0, 0))],
        out_specs=(pl.BlockSpec((tm, d), lambda i: (i, 0)), pl.BlockSpec((tm, LANES), lambda i: (i, 0))),
        compiler_params=_params(("parallel",)),
        name="norm_prep",
    )(x, g.reshape(1, d))


def _rmsnorm_kernel(x_ref, g_ref, o_ref, *, eps):
    x = x_ref[...]
    ms = jnp.mean(x * x, axis=-1, keepdims=True)
    o_ref[...] = (x * lax.rsqrt(ms + eps) * g_ref[...]).astype(o_ref.dtype)


def rmsnorm(x, g, eps, out_dtype):
    s, d = x.shape
    tm = _tile(s, (256, 128, 8))
    return pl.pallas_call(
        functools.partial(_rmsnorm_kernel, eps=eps),
        out_shape=jax.ShapeDtypeStruct((s, d), out_dtype),
        grid=(s // tm,),
        in_specs=[pl.BlockSpec((tm, d), lambda i: (i, 0)),
                  pl.BlockSpec((1, d), lambda i: (0, 0))],
        out_specs=pl.BlockSpec((tm, d), lambda i: (i, 0)),
        compiler_params=_params(("parallel",)),
        name="rmsnorm",
    )(x, g.reshape(1, d))


KIND_PLAIN, KIND_ROPE128_Q, KIND_ROPE128_K, KIND_ROPE64_Q, KIND_ROPE64_K = range(5)
_TILE_KINDS = (KIND_ROPE64_Q, KIND_ROPE64_K, KIND_PLAIN,
               KIND_ROPE128_Q, KIND_ROPE128_K, KIND_PLAIN,
               KIND_ROPE128_Q, KIND_ROPE128_Q, KIND_ROPE128_Q,
               KIND_ROPE128_K, KIND_ROPE128_K, KIND_ROPE128_K,
               KIND_PLAIN, KIND_PLAIN, KIND_PLAIN)
IN_PROJ_ROW_PARTS = 16


def _tile_kind(j):
    kind = jnp.int32(_TILE_KINDS[-1])
    for t in range(len(_TILE_KINDS) - 2, -1, -1):
        kind = jnp.where(j == t, _TILE_KINDS[t], kind)
    return kind


def _in_proj_kernel(xg_ref, ssq_ref, w_ref, ta_ref, tb_ref, tc_ref, td_ref, o_ref):
    d = xg_ref.shape[1]
    w = w_ref[...].astype(BF16)
    part = xg_ref.shape[0] // IN_PROJ_ROW_PARTS
    for r in range(IN_PROJ_ROW_PARTS):
        rows = slice(r * part, (r + 1) * part)
        acc = jnp.dot(xg_ref[rows, :], w, preferred_element_type=F32)
        acc = acc * _rstd(ssq_ref[rows, :], d, NORM_EPS)
        ta, tb, tc, td = ta_ref[rows, :], tb_ref[rows, :], tc_ref[rows, :], td_ref[rows, :]
        for c in range(PROJ_TILE // LANES):
            xs = acc[:, c * LANES:(c + 1) * LANES]
            y = (xs * ta + pltpu.roll(xs, 64, 1) * tb
                 + pltpu.roll(xs, 32, 1) * tc + pltpu.roll(xs, 96, 1) * td)
            o_ref[c, rows, :] = y.astype(o_ref.dtype)


def in_proj(xg, ssq, w_all, layer, tabs):
    s, d = xg.shape
    n = w_all.shape[2]
    assert n // PROJ_TILE == len(_TILE_KINDS)
    tm = _tile(s, (2048, 1024, 512, 256))
    tab_spec = pl.BlockSpec((None, tm, LANES), lambda i, j: (_tile_kind(j), i, 0))
    return pl.pallas_call(
        _in_proj_kernel,
        out_shape=jax.ShapeDtypeStruct((n // LANES, s, LANES), BF16),
        grid=(s // tm, n // PROJ_TILE),
        in_specs=[_resident((tm, d), lambda i, j: (i, 0)),
                  _resident((tm, LANES), lambda i, j: (i, 0)),
                  pl.BlockSpec((None, d, PROJ_TILE), lambda i, j: (layer, 0, j)),
                  tab_spec, tab_spec, tab_spec, tab_spec],
        out_specs=pl.BlockSpec((PROJ_TILE // LANES, tm, LANES), lambda i, j: (j, i, 0)),
        compiler_params=_params(("parallel", "arbitrary")),
        name="in_proj",
    )(xg, ssq, w_all, *tabs)


def rope_tables(seq):
    def angles(dim):
        inv = ROPE_THETA ** (-jnp.arange(0, dim, 2, dtype=F32) / dim)
        ang = jnp.arange(seq, dtype=F32)[:, None] * inv[None, :]
        reps = LANES // dim
        cos = jnp.tile(jnp.concatenate([jnp.cos(ang)] * 2, axis=1), (1, reps))
        sin = jnp.tile(jnp.concatenate([-jnp.sin(ang), jnp.sin(ang)], axis=1), (1, reps))
        return cos, sin
    c128, s128 = angles(HEAD_DIM)
    c64, s64 = angles(DIFF_QK_DIM)
    zero, one = jnp.zeros((seq, LANES), F32), jnp.ones((seq, LANES), F32)
    low = (jnp.arange(LANES) & (DIFF_QK_DIM // 2)) == 0
    s64_up, s64_down = jnp.where(low, s64, 0.0), jnp.where(low, 0.0, s64)
    q128, q64 = HEAD_DIM ** -0.5 * LOG2_E, DIFF_QK_DIM ** -0.5 * LOG2_E
    kinds = {KIND_PLAIN: (one, zero, zero, zero),
             KIND_ROPE128_Q: (c128 * q128, s128 * q128, zero, zero),
             KIND_ROPE128_K: (c128, s128, zero, zero),
             KIND_ROPE64_Q: (c64 * q64, zero, s64_down * q64, s64_up * q64),
             KIND_ROPE64_K: (c64, zero, s64_down, s64_up)}
    return tuple(jnp.stack([kinds[k][t] for k in range(len(kinds))]) for t in range(4))


VT_ROWS = HEAD_DIM + BF16_SUBLANES


def _transpose_bf16(x):
    return x.astype(F32).T.astype(BF16)


def _dot_nt(a, b):
    return lax.dot_general(a, b, (((1,), (1,)), ((), ())), preferred_element_type=F32)


def _fill_v_transposed(v_ref, vt_ref, chunk):
    seq = v_ref.shape[1]
    for c in range(seq // chunk):
        vt_ref[0:HEAD_DIM, c * chunk:(c + 1) * chunk] = _transpose_bf16(
            v_ref[0, c * chunk:(c + 1) * chunk, :])
    vt_ref[HEAD_DIM:VT_ROWS, :] = jnp.ones((BF16_SUBLANES, seq), BF16)


def _softmax_step(st, vt, carry):
    m_old, acc = carry
    m_new = jnp.maximum(m_old, jnp.max(st, axis=0, keepdims=True))
    alpha = jnp.exp2(m_old - m_new)
    p = jnp.exp2(st - m_new).astype(BF16)
    return m_new, alpha * acc + jnp.dot(vt, p, preferred_element_type=F32)


def _init_carry(nq):
    return jnp.full((1, nq), NEG_INF, F32), jnp.zeros((VT_ROWS, nq), F32)


def _normalized(carry):
    acc = carry[1]
    return acc[0:HEAD_DIM, :] / acc[HEAD_DIM:HEAD_DIM + 1, :]


TILE_UNROLL = 4


def _pipelined_attention(n_past, granule, past_tile, tail_tiles, scores_into, st_ref, carry):
    def sequence(first, tiles, c, issue_after_last):
        for u, tile in enumerate(tiles):
            if u + 1 < len(tiles) or issue_after_last:
                scores_into((u + 1) % 2, first + u + 1)
            c = _softmax_step(*tile(first + u, st_ref[u % 2]), c)
        return c

    assert TILE_UNROLL % 2 == 0 and TILE_UNROLL % granule == 0
    scores_into(0, 0)
    carry = lax.fori_loop(
        0, n_past // TILE_UNROLL,
        lambda t, c: sequence(TILE_UNROLL * t, [past_tile] * TILE_UNROLL, c, True), carry)
    first = (n_past // TILE_UNROLL) * TILE_UNROLL
    leftovers = range(0, TILE_UNROLL, granule)
    return lax.switch((n_past % TILE_UNROLL) // granule,
                      [lambda c, r=r: sequence(first, [past_tile] * r + list(tail_tiles), c, False)
                       for r in leftovers], carry)


def _diff_attn_kernel(q_ref, k_ref, v_ref, lam_ref, gsub_ref, o_ref, vt_ref, qs_ref, st_ref, *, tq):
    i = pl.program_id(1)

    @pl.when(i == 0)
    def _():
        _fill_v_transposed(v_ref, vt_ref, tq)

    q = q_ref[0]
    lane = lax.broadcasted_iota(jnp.int32, q.shape, 1)
    zero = jnp.zeros_like(q)
    qs_ref[0:tq, :] = jnp.where(lane < DIFF_QK_DIM, q, zero)
    qs_ref[tq:2 * tq, :] = jnp.where(lane >= DIFF_QK_DIM, q, zero)

    def scores_into(slot, j):
        start = pl.multiple_of(j * tq, tq)
        st_ref[slot] = _dot_nt(k_ref[0, pl.ds(start, tq), :], qs_ref[...])

    def values(j):
        return vt_ref[:, pl.ds(pl.multiple_of(j * tq, tq), tq)]

    def past(j, st):
        return st, values(j)

    def diagonal(j, st):
        krow = lax.broadcasted_iota(jnp.int32, st.shape, 0)
        qcol = lax.broadcasted_iota(jnp.int32, st.shape, 1)
        qcol = jnp.where(qcol >= tq, qcol - tq, qcol)
        return jnp.where(krow <= qcol, st, NEG_INF), values(j)

    carry = _pipelined_attention(i, 1, past, [diagonal], scores_into, st_ref, _init_carry(2 * tq))

    lp = lam_ref[...]
    lam_init = lp[4:5, 0:1]
    lam = (jnp.exp(jnp.sum(lp[0:1] * lp[1:2], axis=1, keepdims=True))
           - jnp.exp(jnp.sum(lp[2:3] * lp[3:4], axis=1, keepdims=True)) + lam_init)
    o = _normalized(carry)
    od = o[:, 0:tq] - lam * o[:, tq:2 * tq]
    ms = jnp.mean(od * od, axis=0, keepdims=True)
    y = od * lax.rsqrt(ms + SUBLN_EPS) * gsub_ref[...] * (1.0 - lam_init)
    o_ref[...] = y.T.astype(o_ref.dtype)


def diff_attention(proj, lam_params, g_sub):
    _, s, _ = proj.shape
    tq = _tile(s, (512, 256, 128))
    return pl.pallas_call(
        functools.partial(_diff_attn_kernel, tq=tq),
        out_shape=jax.ShapeDtypeStruct((s, BRANCH_WIDTH), BF16),
        grid=(HEADS, s // tq),
        in_specs=[pl.BlockSpec((1, tq, LANES), lambda h, i: (SLAB_AQ + h, i, 0)),
                  pl.BlockSpec((1, s, LANES), lambda h, i: (SLAB_AK + h, 0, 0)),
                  pl.BlockSpec((1, s, LANES), lambda h, i: (SLAB_AV + h, 0, 0)),
                  pl.BlockSpec((8, LANES), lambda h, i: (0, 0)),
                  pl.BlockSpec((HEAD_DIM, 1), lambda h, i: (0, 0))],
        out_specs=pl.BlockSpec((tq, LANES), lambda h, i: (i, h)),
        scratch_shapes=[pltpu.VMEM((VT_ROWS, s), BF16),
                        pltpu.VMEM((2 * tq, LANES), BF16),
                        pltpu.VMEM((2, tq, 2 * tq), F32)],
        compiler_params=_params(("arbitrary", "arbitrary")),
        name="diff_attn",
    )(proj, proj, proj, lam_params, g_sub.reshape(HEAD_DIM, 1))


def _moba_attn_kernel(q_ref, k_ref, v_ref, o_ref, vt_ref, kmean_ref, bias_ref, st_ref, *, tq, tk, nb):
    i = pl.program_id(1)
    blk = MOBA_BLOCK
    blk_shift = blk.bit_length() - 1
    blk_per_tile = tk // blk
    own_tiles = tq // tk

    @pl.when(i == 0)
    def _():
        _fill_v_transposed(v_ref, vt_ref, tk)
        for n in range(nb):
            kmean_ref[n:n + 1, :] = jnp.mean(
                k_ref[0, n * blk:(n + 1) * blk, :].astype(F32), axis=0, keepdims=True)

    q = q_ref[0]
    km = kmean_ref[...]
    km_hi = km.astype(BF16)
    km_lo = (km - km_hi.astype(F32)).astype(BF16)
    gate = _dot_nt(km_hi, q) + _dot_nt(km_lo, q)
    bidx = lax.broadcasted_iota(jnp.int32, gate.shape, 0)
    qpos = i * tq + lax.broadcasted_iota(jnp.int32, gate.shape, 1)
    own = jnp.right_shift(qpos, blk_shift)
    past_blk = bidx < own
    gate = jnp.where(past_blk, gate, NEG_INF)
    cand = gate
    selected = jnp.zeros(gate.shape, jnp.bool_)
    for _ in range(MOBA_TOP_K):
        top = jnp.max(cand, axis=0, keepdims=True)
        first = jnp.min(jnp.where(cand == top, bidx, nb), axis=0, keepdims=True)
        pick = bidx == first
        selected = jnp.logical_or(selected, pick)
        cand = jnp.where(pick, NEG_INF, cand)
    selected = jnp.logical_and(selected, past_blk)
    bias_ref[...] = jnp.where(selected, 0.0, NEG_INF).astype(F32)

    def scores_into(slot, j):
        start = pl.multiple_of(j * tk, tk)
        st_ref[slot] = _dot_nt(k_ref[0, pl.ds(start, tk), :], q)

    def values(j):
        return vt_ref[:, pl.ds(pl.multiple_of(j * tk, tk), tk)]

    def past(j, st):
        st = jnp.concatenate(
            [st[b * blk:(b + 1) * blk, :] + bias_ref[pl.ds(j * blk_per_tile + b, 1), :]
             for b in range(blk_per_tile)], axis=0)
        return st, values(j)

    own_row = jnp.right_shift(i * tq + lax.broadcasted_iota(jnp.int32, (1, tq), 1), blk_shift)

    def own(j, st):
        kpos = j * tk + lax.broadcasted_iota(jnp.int32, st.shape, 0)
        qp = i * tq + lax.broadcasted_iota(jnp.int32, st.shape, 1)
        st = jnp.where(kpos <= qp, st, NEG_INF)
        parts = []
        for b in range(blk_per_tile):
            n = j * blk_per_tile + b
            row_bias = jnp.where(own_row > n, bias_ref[pl.ds(n, 1), :],
                                 jnp.where(own_row == n, 0.0, NEG_INF))
            parts.append(st[b * blk:(b + 1) * blk, :] + row_bias)
        return jnp.concatenate(parts, axis=0), values(j)

    carry = _pipelined_attention(i * own_tiles, own_tiles, past, [own] * own_tiles, scores_into, st_ref,
                                 _init_carry(tq))

    o_ref[...] = _normalized(carry).T.astype(o_ref.dtype)


def moba_attention(proj):
    _, s, _ = proj.shape
    assert s % MOBA_BLOCK == 0 and MOBA_BLOCK & (MOBA_BLOCK - 1) == 0
    nb = s // MOBA_BLOCK
    tq = _tile(s, (1024, 512, 256))
    tk = _tile(tq, (512, 256))
    return pl.pallas_call(
        functools.partial(_moba_attn_kernel, tq=tq, tk=tk, nb=nb),
        out_shape=jax.ShapeDtypeStruct((s, BRANCH_WIDTH), BF16),
        grid=(HEADS, s // tq),
        in_specs=[pl.BlockSpec((1, tq, LANES), lambda h, i: (SLAB_BQ + h, i, 0)),
                  pl.BlockSpec((1, s, LANES), lambda h, i: (SLAB_BK + h, 0, 0)),
                  pl.BlockSpec((1, s, LANES), lambda h, i: (SLAB_BV + h, 0, 0))],
        out_specs=pl.BlockSpec((tq, LANES), lambda h, i: (i, h)),
        scratch_shapes=[pltpu.VMEM((VT_ROWS, s), BF16),
                        pltpu.VMEM((nb, LANES), F32),
                        pltpu.VMEM((nb, tq), F32),
                        pltpu.VMEM((2, tk, tq), F32)],
        compiler_params=_params(("arbitrary", "arbitrary")),
        name="moba_attn",
    )(proj, proj, proj)


CLASS_UNROLL = 4


def _dilated_kernel(q_ref, k_ref, v_ref, kp_ref, vp_ref, *refs, tq, band, dilation, n_merge):
    other_o, other_lse = refs[:n_merge], refs[n_merge:2 * n_merge]
    out_refs = refs[2 * n_merge:len(refs) - 5]
    qf_ref, kf_ref, vf_ref, kpf_ref, vpf_ref = refs[len(refs) - 5:]
    i = pl.program_id(1)
    qf_ref[...] = q_ref[0].astype(F32)
    kf_ref[...] = k_ref[0].astype(F32)
    vf_ref[...] = v_ref[0].astype(F32)
    kpf_ref[...] = kp_ref[0].astype(F32)
    vpf_ref[...] = vp_ref[0].astype(F32)

    def one_class(r, carry):
        cls = lambda ref, n: ref[pl.ds(r, n, stride=dilation), :].astype(BF16)
        q = cls(qf_ref, tq)
        k = jnp.concatenate([cls(kpf_ref, band), cls(kf_ref, tq)], axis=0)
        v = jnp.concatenate([cls(vpf_ref, band), cls(vf_ref, tq)], axis=0)
        outs, lses = [], []
        for u in range(tq // band):
            ku = k[u * band:(u + 2) * band, :]
            vu = v[u * band:(u + 2) * band, :]
            sc = _dot_nt(q[u * band:(u + 1) * band, :], ku)
            qi = lax.broadcasted_iota(jnp.int32, sc.shape, 0)
            kj = lax.broadcasted_iota(jnp.int32, sc.shape, 1)
            ok = jnp.logical_and(kj >= qi, kj <= qi + band)
            if u == 0:
                ok = jnp.logical_and(ok, jnp.logical_or(kj >= band, i > 0))
            sc = jnp.where(ok, sc, NEG_INF)
            m = jnp.max(sc, axis=1, keepdims=True)
            e = jnp.exp2(sc - m)
            den = jnp.sum(e, axis=1, keepdims=True)
            outs.append(jnp.dot(e.astype(BF16), vu, preferred_element_type=F32) / den)
            lses.append(jnp.broadcast_to(m + jnp.log2(den), (band, LANES)))
        o, lse = jnp.concatenate(outs, axis=0), jnp.concatenate(lses, axis=0)
        if n_merge:
            all_o = [o] + [ref[0] for ref in other_o]
            all_lse = [lse] + [ref[0] for ref in other_lse]
            top = functools.reduce(jnp.maximum, all_lse)
            es = [jnp.exp2(x - top) for x in all_lse]
            inv = 1.0 / functools.reduce(lambda a, b: a + b, es)
            mixed = functools.reduce(lambda a, b: a + b, [(e * inv) * x for e, x in zip(es, all_o)])
            out_refs[0][...] = mixed.astype(out_refs[0].dtype)
        else:
            o_ref, lse_ref = out_refs
            o_ref[0, pl.ds(r, tq, stride=dilation), :] = o
            lse_ref[0, pl.ds(r, tq, stride=dilation), :] = lse
        return carry

    unroll = min(dilation, CLASS_UNROLL)

    def classes(t, carry):
        for u in range(unroll):
            one_class(unroll * t + u, carry)
        return carry

    lax.fori_loop(0, dilation // unroll, classes, 0)


def dilated_group(proj, g, window, dilation, merge_with=()):
    _, s, _ = proj.shape
    band = window // dilation
    assert band == LANES and s % (band * dilation) == 0
    assert not merge_with or dilation == 1
    length = s // dilation
    tq = _tile(length, (max(512, 2048 // dilation), 512, 256, 128))
    per = tq // band
    rows, prev_rows = tq * dilation, band * dilation
    sq, sk, sv = SLAB_CQ + g * HEADS, SLAB_CK + g * HEADS, SLAB_CV + g * HEADS
    cur = lambda base: pl.BlockSpec((1, rows, LANES), lambda h, i: (base + h, i, 0))
    prev = lambda base: pl.BlockSpec(
        (1, prev_rows, LANES), lambda h, i: (base + h, jnp.maximum(i * per - 1, 0), 0))
    group_spec = pl.BlockSpec((1, rows, LANES), lambda h, i: (h, i, 0))
    if merge_with:
        out_shape = jax.ShapeDtypeStruct((s, BRANCH_WIDTH), BF16)
        out_specs = pl.BlockSpec((rows, LANES), lambda h, i: (i, h))
    else:
        out_shape = (jax.ShapeDtypeStruct((HEADS, s, LANES), F32),) * 2
        out_specs = (group_spec, group_spec)
    others = [o for o, _ in merge_with] + [lse for _, lse in merge_with]
    return pl.pallas_call(
        functools.partial(_dilated_kernel, tq=tq, band=band, dilation=dilation, n_merge=len(merge_with)),
        out_shape=out_shape,
        grid=(HEADS, length // tq),
        in_specs=[cur(sq), cur(sk), cur(sv), prev(sk), prev(sv)] + [group_spec] * len(others),
        out_specs=out_specs,
        scratch_shapes=[pltpu.VMEM((rows, LANES), F32)] * 3 + [pltpu.VMEM((prev_rows, LANES), F32)] * 2,
        compiler_params=_params(("parallel", "parallel")),
        name=f"dilated_d{dilation}",
    )(proj, proj, proj, proj, proj, *others)


def dilated_mixture(proj):
    dense = [g for g, (_, dilation) in enumerate(DIL_PATTERNS) if dilation == 1]
    assert len(dense) == 1
    others = tuple(dilated_group(proj, g, window, dilation)
                   for g, (window, dilation) in enumerate(DIL_PATTERNS) if dilation != 1)
    return dilated_group(proj, dense[0], *DIL_PATTERNS[dense[0]], merge_with=others)


def _gate_mix_kernel(xg_ref, ssq_ref, oa_ref, ob_ref, oc_ref, wg0_ref, wg1_ref, wg2_ref,
                     b0_ref, b1_ref, b2_ref, wb_ref, z_ref):
    xg = xg_ref[...]
    rstd = _rstd(ssq_ref[...], xg.shape[1], NORM_EPS)
    z = None
    for n, (o_ref, wg_ref, b_ref) in enumerate(((oa_ref, wg0_ref, b0_ref),
                                                (ob_ref, wg1_ref, b1_ref),
                                                (oc_ref, wg2_ref, b2_ref))):
        pre = jnp.dot(xg, wg_ref[...].astype(BF16), preferred_element_type=F32)
        gate = jax.nn.sigmoid(pre * rstd + b_ref[...])
        y = jnp.dot(o_ref[...], wb_ref[n].astype(BF16), preferred_element_type=F32)
        z = gate * y if z is None else z + gate * y
    z_ref[...] = z.astype(z_ref.dtype)


def gate_mix(xg, ssq, o_a, o_b, o_c, w_gate_all, b_gate_all, w_branch_all, layer):
    s, d = xg.shape
    assert N_BRANCH == 3
    tm = _tile(s, (1024, 512, 256))
    tn = _tile(d, (256, 128))
    nj = d // tn
    o_spec = _resident((tm, BRANCH_WIDTH), lambda i, j: (i, 0))
    wg_spec = lambda n: pl.BlockSpec((None, d, tn), lambda i, j: (layer, 0, n * nj + j))
    b_spec = lambda n: pl.BlockSpec((None, 1, tn), lambda i, j: (layer, 0, n * nj + j))
    b3d = b_gate_all.reshape(b_gate_all.shape[0], 1, N_BRANCH * d)
    return pl.pallas_call(
        _gate_mix_kernel,
        out_shape=jax.ShapeDtypeStruct((s, d), BF16),
        grid=(s // tm, nj),
        in_specs=[_resident((tm, d), lambda i, j: (i, 0)),
                  _resident((tm, LANES), lambda i, j: (i, 0)),
                  o_spec, o_spec, o_spec,
                  wg_spec(0), wg_spec(1), wg_spec(2), b_spec(0), b_spec(1), b_spec(2),
                  pl.BlockSpec((None, N_BRANCH, BRANCH_WIDTH, tn), lambda i, j: (layer, 0, 0, j))],
        out_specs=pl.BlockSpec((tm, tn), lambda i, j: (i, j)),
        compiler_params=_params(("parallel", "arbitrary")),
        name="gate_mix",
    )(xg, ssq, o_a, o_b, o_c, w_gate_all, w_gate_all, w_gate_all,
      b3d, b3d, b3d, w_branch_all)


RESIDUAL_ROW_PARTS = 2


def _residual_matmul_kernel(a_ref, w_ref, x_ref, g_ref, o_ref, xg_ref, ssq_ref, *, k_chunks):
    @pl.when(pl.program_id(1) == 0)
    def _():
        ssq_ref[...] = jnp.zeros(ssq_ref.shape, F32)

    kc = a_ref.shape[1] // k_chunks
    ws = [w_ref[c * kc:(c + 1) * kc, :].astype(BF16) for c in range(k_chunks)]
    part = a_ref.shape[0] // RESIDUAL_ROW_PARTS
    for r in range(RESIDUAL_ROW_PARTS):
        rows = slice(r * part, (r + 1) * part)
        acc = x_ref[rows, :]
        for c in range(k_chunks):
            acc = acc + jnp.dot(a_ref[rows, c * kc:(c + 1) * kc], ws[c], preferred_element_type=F32)
        o_ref[rows, :] = acc
        xg_ref[rows, :] = (acc * g_ref[...]).astype(BF16)
        ssq_ref[rows, :] += _lane_folded_sumsq(acc)


def residual_matmul(a, w_all, layer, x, g_next, tm_prefs, tn_prefs, k_chunks=1):
    s, k = a.shape
    d = w_all.shape[2]
    tm = _tile(s, tm_prefs)
    tn = _tile(d, tn_prefs)
    assert k % (k_chunks * LANES) == 0
    return pl.pallas_call(
        functools.partial(_residual_matmul_kernel, k_chunks=k_chunks),
        out_shape=(jax.ShapeDtypeStruct((s, d), F32), jax.ShapeDtypeStruct((s, d), BF16),
                   jax.ShapeDtypeStruct((s, LANES), F32)),
        grid=(s // tm, d // tn),
        in_specs=[_activation_spec((tm, k), lambda i, j: (i, 0), a.dtype.itemsize),
                  pl.BlockSpec((None, k, tn), lambda i, j: (layer, 0, j)),
                  pl.BlockSpec((tm, tn), lambda i, j: (i, j)),
                  pl.BlockSpec((1, tn), lambda i, j: (0, j))],
        out_specs=(pl.BlockSpec((tm, tn), lambda i, j: (i, j)),
                   pl.BlockSpec((tm, tn), lambda i, j: (i, j)),
                   pl.BlockSpec((tm, LANES), lambda i, j: (i, 0))),
        compiler_params=_params(("parallel", "arbitrary")),
        name="residual_matmul",
    )(a, w_all, x, g_next.reshape(1, d))


SWIGLU_ROW_PARTS = 8


def _swiglu_kernel(xg_ref, ssq_ref, wg_ref, wu_ref, o_ref):
    d = xg_ref.shape[1]
    wg = wg_ref[...].astype(BF16)
    wu = wu_ref[...].astype(BF16)
    part = xg_ref.shape[0] // SWIGLU_ROW_PARTS
    for r in range(SWIGLU_ROW_PARTS):
        rows = slice(r * part, (r + 1) * part)
        xg = xg_ref[rows, :]
        rstd = _rstd(ssq_ref[rows, :], d, NORM_EPS)
        g = jnp.dot(xg, wg, preferred_element_type=F32) * rstd
        u = jnp.dot(xg, wu, preferred_element_type=F32) * rstd
        o_ref[rows, :] = (g * jax.nn.sigmoid(g) * u).astype(o_ref.dtype)


def swiglu_up(xg, ssq, w_g_all, w_u_all, layer):
    s, d = xg.shape
    f = w_g_all.shape[2]
    tm = _tile(s, (2048, 1024, 512, 256))
    tn = _tile(f, (256, 128))
    w_spec = pl.BlockSpec((None, d, tn), lambda i, j: (layer, 0, j))
    return pl.pallas_call(
        _swiglu_kernel,
        out_shape=jax.ShapeDtypeStruct((s, f), BF16),
        grid=(s // tm, f // tn),
        in_specs=[_activation_spec((tm, d), lambda i, j: (i, 0), xg.dtype.itemsize),
                  _resident((tm, LANES), lambda i, j: (i, 0)),
                  w_spec, w_spec],
        out_specs=pl.BlockSpec((tm, tn), lambda i, j: (i, j)),
        compiler_params=_params(("parallel", "arbitrary")),
        name="swiglu_up",
    )(xg, ssq, w_g_all, w_u_all)


def kernel(x, g_mix, w_in, lam_q1, lam_k1, lam_q2, lam_k2, g_subln, w_gate, b_gate, w_branch,
           w_out, g_ffn, w_ffn_gate, w_ffn_up, w_ffn_down, g_final):
    b, s, d = x.shape
    depth = w_in.shape[0]
    tabs = rope_tables(s)
    down_chunks = 2 if w_ffn_down.shape[1] % (2 * LANES) == 0 else 1
    outs = []
    for bi in range(b):
        xr = x.reshape(s, d) if b == 1 else x[bi]
        xg, ssq = norm_prep(xr, g_mix[0])
        for l in range(depth):
            lam_init = 0.8 - 0.6 * math.exp(-0.3 * l)
            pad = lambda v: jnp.pad(v.astype(F32), (0, LANES - DIFF_QK_DIM))
            lam_params = jnp.stack(
                [pad(lam_q1[l]), pad(lam_k1[l]), pad(lam_q2[l]), pad(lam_k2[l]),
                 jnp.full((LANES,), lam_init, F32)]
                + [jnp.zeros((LANES,), F32)] * 3)
            proj = in_proj(xg, ssq, w_in, l, tabs)
            o_a = diff_attention(proj, lam_params, g_subln[l])
            o_b = moba_attention(proj)
            o_c = dilated_mixture(proj)
            z = gate_mix(xg, ssq, o_a, o_b, o_c, w_gate, b_gate, w_branch, l)
            xr, xg, ssq = residual_matmul(z, w_out, l, xr, g_ffn[l], (2048, 1024, 512, 256), (256, 128))
            u = swiglu_up(xg, ssq, w_ffn_gate, w_ffn_up, l)
            g_next = g_mix[l + 1] if l + 1 < depth else g_final
            xr, xg, ssq = residual_matmul(u, w_ffn_down, l, xr, g_next, (1024, 512, 256), (256, 128),
                                          down_chunks)
        outs.append(rmsnorm(xr, g_final, NORM_EPS, x.dtype))
    return outs[0].reshape(b, s, d) if b == 1 else jnp.stack(outs, axis=0)
```

```python
import functools
import math

import jax
import jax.numpy as jnp
from jax import lax
from jax.experimental import pallas as pl
from jax.experimental.pallas import tpu as pltpu

HEAD_DIM = 128
DIFF_QK_DIM = HEAD_DIM // 2
MOBA_BLOCK = 256
MOBA_TOP_K = 3
DIL_PATTERNS = ((128, 1), (512, 4), (2048, 16))
N_BRANCH = 3
BRANCH_WIDTH = 512
ROPE_THETA = 10000.0
NORM_EPS = 1e-6
SUBLN_EPS = 1e-5
NEG_INF = -1e30
LOG2_E = math.log2(math.e)

HEADS = 4
SLAB_AQ, SLAB_AK, SLAB_AV = 0, 4, 8
SLAB_BQ, SLAB_BK, SLAB_BV = 12, 16, 20
SLAB_CQ, SLAB_CK, SLAB_CV = 24, 36, 48
PROJ_TILE = 512

LANES = 128
BF16_SUBLANES = 16
VMEM_LIMIT_BYTES = 58 * 1024 * 1024

F32 = jnp.float32
BF16 = jnp.bfloat16


def _params(sem):
    return pltpu.CompilerParams(dimension_semantics=sem, vmem_limit_bytes=VMEM_LIMIT_BYTES)


def _tile(n, prefs):
    for p in prefs:
        if n % p == 0:
            return p
    return n


def _resident(block_shape, index_map):
    return pl.BlockSpec(block_shape, index_map, pipeline_mode=pl.Buffered(1))


PREFETCHED_ACT_TILE_BYTES = 16 * 1024 * 1024


def _activation_spec(block_shape, index_map, itemsize):
    if math.prod(block_shape) * itemsize <= PREFETCHED_ACT_TILE_BYTES:
        return pl.BlockSpec(block_shape, index_map)
    return _resident(block_shape, index_map)


def _lane_folded_sumsq(x):
    sq = x * x
    out = sq[:, 0:LANES]
    for c in range(1, x.shape[1] // LANES):
        out = out + sq[:, c * LANES:(c + 1) * LANES]
    return out


def _rstd(ssq, d, eps):
    return lax.rsqrt(jnp.sum(ssq, axis=1, keepdims=True) * (1.0 / d) + eps)


def _norm_prep_kernel(x_ref, g_ref, xg_ref, ssq_ref):
    x = x_ref[...]
    xg_ref[...] = (x * g_ref[...]).astype(BF16)
    ssq_ref[...] = _lane_folded_sumsq(x)


def norm_prep(x, g):
    s, d = x.shape
    tm = _tile(s, (256, 128, 8))
    return pl.pallas_call(
        _norm_prep_kernel,
        out_shape=(jax.ShapeDtypeStruct((s, d), BF16), jax.ShapeDtypeStruct((s, LANES), F32)),
        grid=(s // tm,),
        in_specs=[pl.BlockSpec((tm, d), lambda i: (i, 0)), pl.BlockSpec((1, d), lambda i: (0, 0))],
        out_specs=(pl.BlockSpec((tm, d), lambda i: (i, 0)), pl.BlockSpec((tm, LANES), lambda i: (i, 0))),
        compiler_params=_params(("parallel",)),
        name="norm_prep",
    )(x, g.reshape(1, d))


def _rmsnorm_kernel(x_ref, g_ref, o_ref, *, eps):
    x = x_ref[...]
    ms = jnp.mean(x * x, axis=-1, keepdims=True)
    o_ref[...] = (x * lax.rsqrt(ms + eps) * g_ref[...]).astype(o_ref.dtype)


def rmsnorm(x, g, eps, out_dtype):
    s, d = x.shape
    tm = _tile(s, (256, 128, 8))
    return pl.pallas_call(
        functools.partial(_rmsnorm_kernel, eps=eps),
        out_shape=jax.ShapeDtypeStruct((s, d), out_dtype),
        grid=(s // tm,),
        in_specs=[pl.BlockSpec((tm, d), lambda i: (i, 0)),
                  pl.BlockSpec((1, d), lambda i: (0, 0))],
        out_specs=pl.BlockSpec((tm, d), lambda i: (i, 0)),
        compiler_params=_params(("parallel",)),
        name="rmsnorm",
    )(x, g.reshape(1, d))


KIND_PLAIN, KIND_ROPE128_Q, KIND_ROPE128_K, KIND_ROPE64_Q, KIND_ROPE64_K = range(5)
_TILE_KINDS = (KIND_ROPE64_Q, KIND_ROPE64_K, KIND_PLAIN,
               KIND_ROPE128_Q, KIND_ROPE128_K, KIND_PLAIN,
               KIND_ROPE128_Q, KIND_ROPE128_Q, KIND_ROPE128_Q,
               KIND_ROPE128_K, KIND_ROPE128_K, KIND_ROPE128_K,
               KIND_PLAIN, KIND_PLAIN, KIND_PLAIN)
IN_PROJ_ROW_PARTS = 16


def _tile_kind(j):
    kind = jnp.int32(_TILE_KINDS[-1])
    for t in range(len(_TILE_KINDS) - 2, -1, -1):
        kind = jnp.where(j == t, _TILE_KINDS[t], kind)
    return kind


def _in_proj_kernel(xg_ref, ssq_ref, w_ref, ta_ref, tb_ref, tc_ref, td_ref, o_ref):
    d = xg_ref.shape[1]
    w = w_ref[...].astype(BF16)
    part = xg_ref.shape[0] // IN_PROJ_ROW_PARTS
    for r in range(IN_PROJ_ROW_PARTS):
        rows = slice(r * part, (r + 1) * part)
        acc = jnp.dot(xg_ref[rows, :], w, preferred_element_type=F32)
        acc = acc * _rstd(ssq_ref[rows, :], d, NORM_EPS)
        ta, tb, tc, td = ta_ref[rows, :], tb_ref[rows, :], tc_ref[rows, :], td_ref[rows, :]
        for c in range(PROJ_TILE // LANES):
            xs = acc[:, c * LANES:(c + 1) * LANES]
            y = (xs * ta + pltpu.roll(xs, 64, 1) * tb
                 + pltpu.roll(xs, 32, 1) * tc + pltpu.roll(xs, 96, 1) * td)
            o_ref[c, rows, :] = y.astype(o_ref.dtype)


def in_proj(xg, ssq, w_all, layer, tabs):
    s, d = xg.shape
    n = w_all.shape[2]
    assert n // PROJ_TILE == len(_TILE_KINDS)
    tm = _tile(s, (2048, 1024, 512, 256))
    tab_spec = pl.BlockSpec((None, tm, LANES), lambda i, j: (_tile_kind(j), i, 0))
    return pl.pallas_call(
        _in_proj_kernel,
        out_shape=jax.ShapeDtypeStruct((n // LANES, s, LANES), BF16),
        grid=(s // tm, n // PROJ_TILE),
        in_specs=[_resident((tm, d), lambda i, j: (i, 0)),
                  _resident((tm, LANES), lambda i, j: (i, 0)),
                  pl.BlockSpec((None, d, PROJ_TILE), lambda i, j: (layer, 0, j)),
                  tab_spec, tab_spec, tab_spec, tab_spec],
        out_specs=pl.BlockSpec((PROJ_TILE // LANES, tm, LANES), lambda i, j: (j, i, 0)),
        compiler_params=_params(("parallel", "arbitrary")),
        name="in_proj",
    )(xg, ssq, w_all, *tabs)


def rope_tables(seq):
    def angles(dim):
        inv = ROPE_THETA ** (-jnp.arange(0, dim, 2, dtype=F32) / dim)
        ang = jnp.arange(seq, dtype=F32)[:, None] * inv[None, :]
        reps = LANES // dim
        cos = jnp.tile(jnp.concatenate([jnp.cos(ang)] * 2, axis=1), (1, reps))
        sin = jnp.tile(jnp.concatenate([-jnp.sin(ang), jnp.sin(ang)], axis=1), (1, reps))
        return cos, sin
    c128, s128 = angles(HEAD_DIM)
    c64, s64 = angles(DIFF_QK_DIM)
    zero, one = jnp.zeros((seq, LANES), F32), jnp.ones((seq, LANES), F32)
    low = (jnp.arange(LANES) & (DIFF_QK_DIM // 2)) == 0
    s64_up, s64_down = jnp.where(low, s64, 0.0), jnp.where(low, 0.0, s64)
    q128, q64 = HEAD_DIM ** -0.5 * LOG2_E, DIFF_QK_DIM ** -0.5 * LOG2_E
    kinds = {KIND_PLAIN: (one, zero, zero, zero),
             KIND_ROPE128_Q: (c128 * q128, s128 * q128, zero, zero),
             KIND_ROPE128_K: (c128, s128, zero, zero),
             KIND_ROPE64_Q: (c64 * q64, zero, s64_down * q64, s64_up * q64),
             KIND_ROPE64_K: (c64, zero, s64_down, s64_up)}
    return tuple(jnp.stack([kinds[k][t] for k in range(len(kinds))]) for t in range(4))


VT_ROWS = HEAD_DIM + BF16_SUBLANES


def _transpose_bf16(x):
    return x.astype(F32).T.astype(BF16)


def _dot_nt(a, b):
    return lax.dot_general(a, b, (((1,), (1,)), ((), ())), preferred_element_type=F32)


def _fill_v_transposed(v_ref, vt_ref, chunk):
    seq = v_ref.shape[1]
    for c in range(seq // chunk):
        vt_ref[0:HEAD_DIM, c * chunk:(c + 1) * chunk] = _transpose_bf16(
            v_ref[0, c * chunk:(c + 1) * chunk, :])
    vt_ref[HEAD_DIM:VT_ROWS, :] = jnp.ones((BF16_SUBLANES, seq), BF16)


def _softmax_step(st, vt, carry):
    m_old, acc = carry
    m_new = jnp.maximum(m_old, jnp.max(st, axis=0, keepdims=True))
    alpha = jnp.exp2(m_old - m_new)
    p = jnp.exp2(st - m_new).astype(BF16)
    return m_new, alpha * acc + jnp.dot(vt, p, preferred_element_type=F32)


def _init_carry(nq):
    return jnp.full((1, nq), NEG_INF, F32), jnp.zeros((VT_ROWS, nq), F32)


def _normalized(carry):
    acc = carry[1]
    return acc[0:HEAD_DIM, :] / acc[HEAD_DIM:HEAD_DIM + 1, :]


TILE_UNROLL = 4


def _pipelined_attention(n_past, granule, past_tile, tail_tiles, scores_into, st_ref, carry):
    def sequence(first, tiles, c, issue_after_last):
        for u, tile in enumerate(tiles):
            if u + 1 < len(tiles) or issue_after_last:
                scores_into((u + 1) % 2, first + u + 1)
            c = _softmax_step(*tile(first + u, st_ref[u % 2]), c)
        return c

    assert TILE_UNROLL % 2 == 0 and TILE_UNROLL % granule == 0
    scores_into(0, 0)
    carry = lax.fori_loop(
        0, n_past // TILE_UNROLL,
        lambda t, c: sequence(TILE_UNROLL * t, [past_tile] * TILE_UNROLL, c, True), carry)
    first = (n_past // TILE_UNROLL) * TILE_UNROLL
    leftovers = range(0, TILE_UNROLL, granule)
    return lax.switch((n_past % TILE_UNROLL) // granule,
                      [lambda c, r=r: sequence(first, [past_tile] * r + list(tail_tiles), c, False)
                       for r in leftovers], carry)


def _diff_attn_kernel(q_ref, k_ref, v_ref, lam_ref, gsub_ref, o_ref, vt_ref, qs_ref, st_ref, *, tq):
    i = pl.program_id(1)

    @pl.when(i == 0)
    def _():
        _fill_v_transposed(v_ref, vt_ref, tq)

    q = q_ref[0]
    lane = lax.broadcasted_iota(jnp.int32, q.shape, 1)
    zero = jnp.zeros_like(q)
    qs_ref[0:tq, :] = jnp.where(lane < DIFF_QK_DIM, q, zero)
    qs_ref[tq:2 * tq, :] = jnp.where(lane >= DIFF_QK_DIM, q, zero)

    def scores_into(slot, j):
        start = pl.multiple_of(j * tq, tq)
        st_ref[slot] = _dot_nt(k_ref[0, pl.ds(start, tq), :], qs_ref[...])

    def values(j):
        return vt_ref[:, pl.ds(pl.multiple_of(j * tq, tq), tq)]

    def past(j, st):
        return st, values(j)

    def diagonal(j, st):
        krow = lax.broadcasted_iota(jnp.int32, st.shape, 0)
        qcol = lax.broadcasted_iota(jnp.int32, st.shape, 1)
        qcol = jnp.where(qcol >= tq, qcol - tq, qcol)
        return jnp.where(krow <= qcol, st, NEG_INF), values(j)

    carry = _pipelined_attention(i, 1, past, [diagonal], scores_into, st_ref, _init_carry(2 * tq))

    lp = lam_ref[...]
    lam_init = lp[4:5, 0:1]
    lam = (jnp.exp(jnp.sum(lp[0:1] * lp[1:2], axis=1, keepdims=True))
           - jnp.exp(jnp.sum(lp[2:3] * lp[3:4], axis=1, keepdims=True)) + lam_init)
    o = _normalized(carry)
    od = o[:, 0:tq] - lam * o[:, tq:2 * tq]
    ms = jnp.mean(od * od, axis=0, keepdims=True)
    y = od * lax.rsqrt(ms + SUBLN_EPS) * gsub_ref[...] * (1.0 - lam_init)
    o_ref[...] = y.T.astype(o_ref.dtype)


def diff_attention(proj, lam_params, g_sub):
    _, s, _ = proj.shape
    tq = _tile(s, (512, 256, 128))
    return pl.pallas_call(
        functools.partial(_diff_attn_kernel, tq=tq),
        out_shape=jax.ShapeDtypeStruct((s, BRANCH_WIDTH), BF16),
        grid=(HEADS, s // tq),
        in_specs=[pl.BlockSpec((1, tq, LANES), lambda h, i: (SLAB_AQ + h, i, 0)),
                  pl.BlockSpec((1, s, LANES), lambda h, i: (SLAB_AK + h, 0, 0)),
                  pl.BlockSpec((1, s, LANES), lambda h, i: (SLAB_AV + h, 0, 0)),
                  pl.BlockSpec((8, LANES), lambda h, i: (0, 0)),
                  pl.BlockSpec((HEAD_DIM, 1), lambda h, i: (0, 0))],
        out_specs=pl.BlockSpec((tq, LANES), lambda h, i: (i, h)),
        scratch_shapes=[pltpu.VMEM((VT_ROWS, s), BF16),
                        pltpu.VMEM((2 * tq, LANES), BF16),
                        pltpu.VMEM((2, tq, 2 * tq), F32)],
        compiler_params=_params(("arbitrary", "arbitrary")),
        name="diff_attn",
    )(proj, proj, proj, lam_params, g_sub.reshape(HEAD_DIM, 1))


def _moba_attn_kernel(q_ref, k_ref, v_ref, o_ref, vt_ref, kmean_ref, bias_ref, st_ref, *, tq, tk, nb):
    i = pl.program_id(1)
    blk = MOBA_BLOCK
    blk_shift = blk.bit_length() - 1
    blk_per_tile = tk // blk
    own_tiles = tq // tk

    @pl.when(i == 0)
    def _():
        _fill_v_transposed(v_ref, vt_ref, tk)
        for n in range(nb):
            kmean_ref[n:n + 1, :] = jnp.mean(
                k_ref[0, n * blk:(n + 1) * blk, :].astype(F32), axis=0, keepdims=True)

    q = q_ref[0]
    km = kmean_ref[...]
    km_hi = km.astype(BF16)
    km_lo = (km - km_hi.astype(F32)).astype(BF16)
    gate = _dot_nt(km_hi, q) + _dot_nt(km_lo, q)
    bidx = lax.broadcasted_iota(jnp.int32, gate.shape, 0)
    qpos = i * tq + lax.broadcasted_iota(jnp.int32, gate.shape, 1)
    own = jnp.right_shift(qpos, blk_shift)
    past_blk = bidx < own
    gate = jnp.where(past_blk, gate, NEG_INF)
    cand = gate
    selected = jnp.zeros(gate.shape, jnp.bool_)
    for _ in range(MOBA_TOP_K):
        top = jnp.max(cand, axis=0, keepdims=True)
        first = jnp.min(jnp.where(cand == top, bidx, nb), axis=0, keepdims=True)
        pick = bidx == first
        selected = jnp.logical_or(selected, pick)
        cand = jnp.where(pick, NEG_INF, cand)
    selected = jnp.logical_and(selected, past_blk)
    bias_ref[...] = jnp.where(selected, 0.0, NEG_INF).astype(F32)

    def scores_into(slot, j):
        start = pl.multiple_of(j * tk, tk)
        st_ref[slot] = _dot_nt(k_ref[0, pl.ds(start, tk), :], q)

    def values(j):
        return vt_ref[:, pl.ds(pl.multiple_of(j * tk, tk), tk)]

    def past(j, st):
        st = jnp.concatenate(
            [st[b * blk:(b + 1) * blk, :] + bias_ref[pl.ds(j * blk_per_tile + b, 1), :]
             for b in range(blk_per_tile)], axis=0)
        return st, values(j)

    own_row = jnp.right_shift(i * tq + lax.broadcasted_iota(jnp.int32, (1, tq), 1), blk_shift)

    def own(j, st):
        kpos = j * tk + lax.broadcasted_iota(jnp.int32, st.shape, 0)
        qp = i * tq + lax.broadcasted_iota(jnp.int32, st.shape, 1)
        st = jnp.where(kpos <= qp, st, NEG_INF)
        parts = []
        for b in range(blk_per_tile):
            n = j * blk_per_tile + b
            row_bias = jnp.where(own_row > n, bias_ref[pl.ds(n, 1), :],
                                 jnp.where(own_row == n, 0.0, NEG_INF))
            parts.append(st[b * blk:(b + 1) * blk, :] + row_bias)
        return jnp.concatenate(parts, axis=0), values(j)

    carry = _pipelined_attention(i * own_tiles, own_tiles, past, [own] * own_tiles, scores_into, st_ref,
                                 _init_carry(tq))

    o_ref[...] = _normalized(carry).T.astype(o_ref.dtype)


def moba_attention(proj):
    _, s, _ = proj.shape
    assert s % MOBA_BLOCK == 0 and MOBA_BLOCK & (MOBA_BLOCK - 1) == 0
    nb = s // MOBA_BLOCK
    tq = _tile(s, (1024, 512, 256))
    tk = _tile(tq, (512, 256))
    return pl.pallas_call(
        functools.partial(_moba_attn_kernel, tq=tq, tk=tk, nb=nb),
        out_shape=jax.ShapeDtypeStruct((s, BRANCH_WIDTH), BF16),
        grid=(HEADS, s // tq),
        in_specs=[pl.BlockSpec((1, tq, LANES), lambda h, i: (SLAB_BQ + h, i, 0)),
                  pl.BlockSpec((1, s, LANES), lambda h, i: (SLAB_BK + h, 0, 0)),
                  pl.BlockSpec((1, s, LANES), lambda h, i: (SLAB_BV + h, 0, 0))],
        out_specs=pl.BlockSpec((tq, LANES), lambda h, i: (i, h)),
        scratch_shapes=[pltpu.VMEM((VT_ROWS, s), BF16),
                        pltpu.VMEM((nb, LANES), F32),
                        pltpu.VMEM((nb, tq), F32),
                        pltpu.VMEM((2, tk, tq), F32)],
        compiler_params=_params(("arbitrary", "arbitrary")),
        name="moba_attn",
    )(proj, proj, proj)


CLASS_UNROLL = 4


def _dilated_kernel(q_ref, k_ref, v_ref, kp_ref, vp_ref, *refs, tq, band, dilation, n_merge):
    other_o, other_lse = refs[:n_merge], refs[n_merge:2 * n_merge]
    out_refs = refs[2 * n_merge:len(refs) - 5]
    qf_ref, kf_ref, vf_ref, kpf_ref, vpf_ref = refs[len(refs) - 5:]
    i = pl.program_id(1)
    qf_ref[...] = q_ref[0].astype(F32)
    kf_ref[...] = k_ref[0].astype(F32)
    vf_ref[...] = v_ref[0].astype(F32)
    kpf_ref[...] = kp_ref[0].astype(F32)
    vpf_ref[...] = vp_ref[0].astype(F32)

    def one_class(r, carry):
        cls = lambda ref, n: ref[pl.ds(r, n, stride=dilation), :].astype(BF16)
        q = cls(qf_ref, tq)
        k = jnp.concatenate([cls(kpf_ref, band), cls(kf_ref, tq)], axis=0)
        v = jnp.concatenate([cls(vpf_ref, band), cls(vf_ref, tq)], axis=0)
        outs, lses = [], []
        for u in range(tq // band):
            ku = k[u * band:(u + 2) * band, :]
            vu = v[u * band:(u + 2) * band, :]
            sc = _dot_nt(q[u * band:(u + 1) * band, :], ku)
            qi = lax.broadcasted_iota(jnp.int32, sc.shape, 0)
            kj = lax.broadcasted_iota(jnp.int32, sc.shape, 1)
            ok = jnp.logical_and(kj >= qi, kj <= qi + band)
            if u == 0:
                ok = jnp.logical_and(ok, jnp.logical_or(kj >= band, i > 0))
            sc = jnp.where(ok, sc, NEG_INF)
            m = jnp.max(sc, axis=1, keepdims=True)
            e = jnp.exp2(sc - m)
            den = jnp.sum(e, axis=1, keepdims=True)
            outs.append(jnp.dot(e.astype(BF16), vu, preferred_element_type=F32) / den)
            lses.append(jnp.broadcast_to(m + jnp.log2(den), (band, LANES)))
        o, lse = jnp.concatenate(outs, axis=0), jnp.concatenate(lses, axis=0)
        if n_merge:
            all_o = [o] + [ref[0] for ref in other_o]
            all_lse = [lse] + [ref[0] for ref in other_lse]
            top = functools.reduce(jnp.maximum, all_lse)
            es = [jnp.exp2(x - top) for x in all_lse]
            inv = 1.0 / functools.reduce(lambda a, b: a + b, es)
            mixed = functools.reduce(lambda a, b: a + b, [(e * inv) * x for e, x in zip(es, all_o)])
            out_refs[0][...] = mixed.astype(out_refs[0].dtype)
        else:
            o_ref, lse_ref = out_refs
            o_ref[0, pl.ds(r, tq, stride=dilation), :] = o
            lse_ref[0, pl.ds(r, tq, stride=dilation), :] = lse
        return carry

    unroll = min(dilation, CLASS_UNROLL)

    def classes(t, carry):
        for u in range(unroll):
            one_class(unroll * t + u, carry)
        return carry

    lax.fori_loop(0, dilation // unroll, classes, 0)


def dilated_group(proj, g, window, dilation, merge_with=()):
    _, s, _ = proj.shape
    band = window // dilation
    assert band == LANES and s % (band * dilation) == 0
    assert not merge_with or dilation == 1
    length = s // dilation
    tq = _tile(length, (max(512, 2048 // dilation), 512, 256, 128))
    per = tq // band
    rows, prev_rows = tq * dilation, band * dilation
    sq, sk, sv = SLAB_CQ + g * HEADS, SLAB_CK + g * HEADS, SLAB_CV + g * HEADS
    cur = lambda base: pl.BlockSpec((1, rows, LANES), lambda h, i: (base + h, i, 0))
    prev = lambda base: pl.BlockSpec(
        (1, prev_rows, LANES), lambda h, i: (base + h, jnp.maximum(i * per - 1, 0), 0))
    group_spec = pl.BlockSpec((1, rows, LANES), lambda h, i: (h, i, 0))
    if merge_with:
        out_shape = jax.ShapeDtypeStruct((s, BRANCH_WIDTH), BF16)
        out_specs = pl.BlockSpec((rows, LANES), lambda h, i: (i, h))
    else:
        out_shape = (jax.ShapeDtypeStruct((HEADS, s, LANES), F32),) * 2
        out_specs = (group_spec, group_spec)
    others = [o for o, _ in merge_with] + [lse for _, lse in merge_with]
    return pl.pallas_call(
        functools.partial(_dilated_kernel, tq=tq, band=band, dilation=dilation, n_merge=len(merge_with)),
        out_shape=out_shape,
        grid=(HEADS, length // tq),
        in_specs=[cur(sq), cur(sk), cur(sv), prev(sk), prev(sv)] + [group_spec] * len(others),
        out_specs=out_specs,
        scratch_shapes=[pltpu.VMEM((rows, LANES), F32)] * 3 + [pltpu.VMEM((prev_rows, LANES), F32)] * 2,
        compiler_params=_params(("parallel", "parallel")),
        name=f"dilated_d{dilation}",
    )(proj, proj, proj, proj, proj, *others)


def dilated_mixture(proj):
    dense = [g for g, (_, dilation) in enumerate(DIL_PATTERNS) if dilation == 1]
    assert len(dense) == 1
    others = tuple(dilated_group(proj, g, window, dilation)
                   for g, (window, dilation) in enumerate(DIL_PATTERNS) if dilation != 1)
    return dilated_group(proj, dense[0], *DIL_PATTERNS[dense[0]], merge_with=others)


def _gate_mix_kernel(xg_ref, ssq_ref, oa_ref, ob_ref, oc_ref, wg0_ref, wg1_ref, wg2_ref,
                     b0_ref, b1_ref, b2_ref, wb_ref, z_ref):
    xg = xg_ref[...]
    rstd = _rstd(ssq_ref[...], xg.shape[1], NORM_EPS)
    z = None
    for n, (o_ref, wg_ref, b_ref) in enumerate(((oa_ref, wg0_ref, b0_ref),
                                                (ob_ref, wg1_ref, b1_ref),
                                                (oc_ref, wg2_ref, b2_ref))):
        pre = jnp.dot(xg, wg_ref[...].astype(BF16), preferred_element_type=F32)
        gate = jax.nn.sigmoid(pre * rstd + b_ref[...])
        y = jnp.dot(o_ref[...], wb_ref[n].astype(BF16), preferred_element_type=F32)
        z = gate * y if z is None else z + gate * y
    z_ref[...] = z.astype(z_ref.dtype)


def gate_mix(xg, ssq, o_a, o_b, o_c, w_gate_all, b_gate_all, w_branch_all, layer):
    s, d = xg.shape
    assert N_BRANCH == 3
    tm = _tile(s, (1024, 512, 256))
    tn = _tile(d, (256, 128))
    nj = d // tn
    o_spec = _resident((tm, BRANCH_WIDTH), lambda i, j: (i, 0))
    wg_spec = lambda n: pl.BlockSpec((None, d, tn), lambda i, j: (layer, 0, n * nj + j))
    b_spec = lambda n: pl.BlockSpec((None, 1, tn), lambda i, j: (layer, 0, n * nj + j))
    b3d = b_gate_all.reshape(b_gate_all.shape[0], 1, N_BRANCH * d)
    return pl.pallas_call(
        _gate_mix_kernel,
        out_shape=jax.ShapeDtypeStruct((s, d), BF16),
        grid=(s // tm, nj),
        in_specs=[_activation_spec((tm, d), lambda i, j: (i, 0), xg.dtype.itemsize),
                  _resident((tm, LANES), lambda i, j: (i, 0)),
                  o_spec, o_spec, o_spec,
                  wg_spec(0), wg_spec(1), wg_spec(2), b_spec(0), b_spec(1), b_spec(2),
                  pl.BlockSpec((None, N_BRANCH, BRANCH_WIDTH, tn), lambda i, j: (layer, 0, 0, j))],
        out_specs=pl.BlockSpec((tm, tn), lambda i, j: (i, j)),
        compiler_params=_params(("parallel", "arbitrary")),
        name="gate_mix",
    )(xg, ssq, o_a, o_b, o_c, w_gate_all, w_gate_all, w_gate_all,
      b3d, b3d, b3d, w_branch_all)


RESIDUAL_ROW_PARTS = 2


def _residual_matmul_kernel(a_ref, w_ref, x_ref, g_ref, o_ref, xg_ref, ssq_ref, *, k_chunks):
    @pl.when(pl.program_id(1) == 0)
    def _():
        ssq_ref[...] = jnp.zeros(ssq_ref.shape, F32)

    kc = a_ref.shape[1] // k_chunks
    ws = [w_ref[c * kc:(c + 1) * kc, :].astype(BF16) for c in range(k_chunks)]
    part = a_ref.shape[0] // RESIDUAL_ROW_PARTS
    for r in range(RESIDUAL_ROW_PARTS):
        rows = slice(r * part, (r + 1) * part)
        acc = x_ref[rows, :]
        for c in range(k_chunks):
            acc = acc + jnp.dot(a_ref[rows, c * kc:(c + 1) * kc], ws[c], preferred_element_type=F32)
        o_ref[rows, :] = acc
        xg_ref[rows, :] = (acc * g_ref[...]).astype(BF16)
        ssq_ref[rows, :] += _lane_folded_sumsq(acc)


def residual_matmul(a, w_all, layer, x, g_next, tm_prefs, tn_prefs, k_chunks=1):
    s, k = a.shape
    d = w_all.shape[2]
    tm = _tile(s, tm_prefs)
    tn = _tile(d, tn_prefs)
    assert k % (k_chunks * LANES) == 0
    return pl.pallas_call(
        functools.partial(_residual_matmul_kernel, k_chunks=k_chunks),
        out_shape=(jax.ShapeDtypeStruct((s, d), F32), jax.ShapeDtypeStruct((s, d), BF16),
                   jax.ShapeDtypeStruct((s, LANES), F32)),
        grid=(s // tm, d // tn),
        in_specs=[_activation_spec((tm, k), lambda i, j: (i, 0), a.dtype.itemsize),
                  pl.BlockSpec((None, k, tn), lambda i, j: (layer, 0, j)),
                  pl.BlockSpec((tm, tn), lambda i, j: (i, j)),
                  pl.BlockSpec((1, tn), lambda i, j: (0, j))],
        out_specs=(pl.BlockSpec((tm, tn), lambda i, j: (i, j)),
                   pl.BlockSpec((tm, tn), lambda i, j: (i, j)),
                   pl.BlockSpec((tm, LANES), lambda i, j: (i, 0))),
        compiler_params=_params(("parallel", "arbitrary")),
        name="residual_matmul",
    )(a, w_all, x, g_next.reshape(1, d))


SWIGLU_ROW_PARTS = 8


def _swiglu_kernel(xg_ref, ssq_ref, wg_ref, wu_ref, o_ref):
    d = xg_ref.shape[1]
    wg = wg_ref[...].astype(BF16)
    wu = wu_ref[...].astype(BF16)
    part = xg_ref.shape[0] // SWIGLU_ROW_PARTS
    for r in range(SWIGLU_ROW_PARTS):
        rows = slice(r * part, (r + 1) * part)
        xg = xg_ref[rows, :]
        rstd = _rstd(ssq_ref[rows, :], d, NORM_EPS)
        g = jnp.dot(xg, wg, preferred_element_type=F32) * rstd
        u = jnp.dot(xg, wu, preferred_element_type=F32) * rstd
        o_ref[rows, :] = (g * jax.nn.sigmoid(g) * u).astype(o_ref.dtype)


def swiglu_up(xg, ssq, w_g_all, w_u_all, layer):
    s, d = xg.shape
    f = w_g_all.shape[2]
    tm = _tile(s, (2048, 1024, 512, 256))
    tn = _tile(f, (256, 128))
    w_spec = pl.BlockSpec((None, d, tn), lambda i, j: (layer, 0, j))
    return pl.pallas_call(
        _swiglu_kernel,
        out_shape=jax.ShapeDtypeStruct((s, f), BF16),
        grid=(s // tm, f // tn),
        in_specs=[_activation_spec((tm, d), lambda i, j: (i, 0), xg.dtype.itemsize),
                  _resident((tm, LANES), lambda i, j: (i, 0)),
                  w_spec, w_spec],
        out_specs=pl.BlockSpec((tm, tn), lambda i, j: (i, j)),
        compiler_params=_params(("parallel", "arbitrary")),
        name="swiglu_up",
    )(xg, ssq, w_g_all, w_u_all)


def kernel(x, g_mix, w_in, lam_q1, lam_k1, lam_q2, lam_k2, g_subln, w_gate, b_gate, w_branch,
           w_out, g_ffn, w_ffn_gate, w_ffn_up, w_ffn_down, g_final):
    b, s, d = x.shape
    depth = w_in.shape[0]
    tabs = rope_tables(s)
    down_chunks = 2 if w_ffn_down.shape[1] % (2 * LANES) == 0 else 1
    outs = []
    for bi in range(b):
        xr = x.reshape(s, d) if b == 1 else x[bi]
        xg, ssq = norm_prep(xr, g_mix[0])
        for l in range(depth):
            lam_init = 0.8 - 0.6 * math.exp(-0.3 * l)
            pad = lambda v: jnp.pad(v.astype(F32), (0, LANES - DIFF_QK_DIM))
            lam_params = jnp.stack(
                [pad(lam_q1[l]), pad(lam_k1[l]), pad(lam_q2[l]), pad(lam_k2[l]),
                 jnp.full((LANES,), lam_init, F32)]
                + [jnp.zeros((LANES,), F32)] * 3)
            proj = in_proj(xg, ssq, w_in, l, tabs)
            o_a = diff_attention(proj, lam_params, g_subln[l])
            o_b = moba_attention(proj)
            o_c = dilated_mixture(proj)
            z = gate_mix(xg, ssq, o_a, o_b, o_c, w_gate, b_gate, w_branch, l)
            xr, xg, ssq = residual_matmul(z, w_out, l, xr, g_ffn[l], (2048, 1024, 512, 256), (256, 128))
            u = swiglu_up(xg, ssq, w_ffn_gate, w_ffn_up, l)
            g_next = g_mix[l + 1] if l + 1 < depth else g_final
            xr, xg, ssq = residual_matmul(u, w_ffn_down, l, xr, g_next, (1024, 512, 256), (256, 128),
                                          down_chunks)
        outs.append(rmsnorm(xr, g_final, NORM_EPS, x.dtype))
    return outs[0].reshape(b, s, d) if b == 1 else jnp.stack(outs, axis=0)
```
